```python
import jax
import jax.numpy as jnp
from jax import lax
import numpy as np

D_MODEL = 1024
BATCH = 4
SEQ = 4096
DEPTH = 2
DEC_BATCH = 32
DEC_SEQ = 1
PAST_LEN = 16384
PAGE_SIZE = 128

W_A = D_MODEL // 2
A_GROUP_DIM = 128
G_A = W_A // A_GROUP_DIM
CHUNK = 128
W_B = D_MODEL - W_A
HD_B = 64
H_B = W_B // HD_B
DIL_PATTERNS = ((128, 1), (512, 4), (2048, 16))
WIN_MAX = max(w for w, _ in DIL_PATTERNS)
ATT_BLOCK = 128
N_IN_EVEN = 2 * W_A + 3 * W_B
D_RNN = D_MODEL
RG_BLOCK = 128
RG_HEADS = D_RNN // RG_BLOCK
RG_CONV = 4
RG_C = 8.0
D_FF = 2816
FFN_CONV = 3
N_EVEN = (DEPTH + 1) // 2
N_ODD = DEPTH // 2
EPS = 1e-6
NEG_INF = -1e30

kernel_name = 'hybrid_sgu_dilattn_rglru_convffn_step'


def _rmsnorm(x):
    xf = x.astype(jnp.float32)
    return (xf * lax.rsqrt(jnp.mean(xf * xf, -1, keepdims=True) + EPS)).astype(x.dtype)


def _layernorm(x, g, b):
    xf = x.astype(jnp.float32)
    mu = jnp.mean(xf, -1, keepdims=True)
    var = jnp.mean(jnp.square(xf - mu), -1, keepdims=True)
    return ((xf - mu) * lax.rsqrt(var + EPS)).astype(x.dtype) * g + b


def _causal_dwconv(x, buf, w, b):
    K = w.shape[0]
    T = x.shape[1]
    xp = jnp.concatenate([buf.astype(x.dtype), x], axis=1)
    y = b + w[0] * xp[:, :T]
    for k in range(1, K):
        y = y + w[k] * xp[:, k:k + T]
    return y, xp[:, xp.shape[1] - (K - 1):]


def _chunk_sgu(u, v, w_s, b_s):
    N, T = u.shape[:2]
    nc = -(-T // CHUNK)
    pad = nc * CHUNK - T
    vp = jnp.pad(v, ((0, 0), (0, pad), (0, 0), (0, 0))).reshape(N, nc, CHUNK, G_A, A_GROUP_DIM)
    w_causal = jnp.tril(w_s)
    mix = jnp.einsum('gts,ncsgd->nctgd', w_causal, vp) + b_s.T[None, None, :, :, None]
    mix = mix.reshape(N, nc * CHUNK, G_A, A_GROUP_DIM)[:, :T]
    return u * mix


def _band_attn(q, k, v, n_back):
    N, M, H, Dh = q.shape
    nb = -(-M // ATT_BLOCK)
    pad = nb * ATT_BLOCK - M

    def blocks(t):
        t = jnp.pad(t, ((0, 0), (0, pad), (0, 0), (0, 0)))
        return t.reshape(N, nb, ATT_BLOCK, H, Dh)

    def with_prev(t):
        prev = jnp.concatenate([jnp.zeros_like(t[:, :1]), t[:, :-1]], axis=1)
        return jnp.concatenate([prev, t], axis=2)

    qb = blocks(q)
    kc = with_prev(blocks(k))
    vc = with_prev(blocks(v))
    s = jnp.einsum('nbqhd,nbkhd->nbhqk', qb, kc).astype(jnp.float32) * (Dh ** -0.5)
    qi = jnp.arange(ATT_BLOCK)[:, None]
    kj = jnp.arange(2 * ATT_BLOCK)[None, :]
    dist = qi + ATT_BLOCK - kj
    key_pos = (jnp.arange(nb) * ATT_BLOCK)[:, None, None] + kj - ATT_BLOCK
    valid = (dist >= 0) & (dist <= n_back) & (key_pos >= 0)
    s = jnp.where(valid[None, :, None], s, NEG_INF)
    m = jnp.max(s, -1, keepdims=True)
    p = jnp.exp(s - m)
    den = jnp.sum(p, -1)
    o = jnp.einsum('nbhqk,nbkhd->nbqhd', p.astype(v.dtype), vc)
    o = o / jnp.swapaxes(den, 2, 3)[..., None].astype(o.dtype)
    lse = jnp.swapaxes(m[..., 0] + jnp.log(den), 2, 3)
    return (o.reshape(N, nb * ATT_BLOCK, H, Dh)[:, :M],
            lse.reshape(N, nb * ATT_BLOCK, H)[:, :M])


def _mix_dilations(outs, lses):
    wts = jax.nn.softmax(jnp.stack(lses, 0), axis=0)
    return jnp.einsum('pnth,pnthd->nthd', wts.astype(outs[0].dtype), jnp.stack(outs, 0))


def _to_strided(t, d):
    N, T, H, Dh = t.shape
    return t.reshape(N, T // d, d, H, Dh).transpose(0, 2, 1, 3, 4).reshape(N * d, T // d, H, Dh)


def _dilated_prompt(q, k, v):
    N, T, H, Dh = q.shape
    outs, lses = [], []
    for w, d in DIL_PATTERNS:
        o, lse = _band_attn(_to_strided(q, d), _to_strided(k, d), _to_strided(v, d), w // d)
        outs.append(o.reshape(N, d, T // d, H, Dh).transpose(0, 2, 1, 3, 4).reshape(N, T, H, Dh))
        lses.append(lse.reshape(N, d, T // d, H).transpose(0, 2, 1, 3).reshape(N, T, H))
    return _mix_dilations(outs, lses)


def _dilated_sample(q, kc, vc):
    N, T, H, Dh = q.shape
    L = kc.shape[1] - T
    t_idx = jnp.arange(T)[:, None]
    outs, lses = [], []
    for w, d in DIL_PATTERNS:
        j = jnp.arange(w // d + 1)[None, :]
        idx = L + t_idx - j * d
        valid = idx >= 0
        idx = jnp.maximum(idx, 0)
        kg = kc[:, idx]
        vg = vc[:, idx]
        s = jnp.einsum('nthd,ntjhd->nthj', q, kg).astype(jnp.float32) * (Dh ** -0.5)
        s = jnp.where(valid[None, :, None, :], s, NEG_INF)
        m = jnp.max(s, -1, keepdims=True)
        p = jnp.exp(s - m)
        den = jnp.sum(p, -1)
        o = jnp.einsum('nthj,ntjhd->nthd', p.astype(vc.dtype), vg) / den[..., None].astype(vc.dtype)
        outs.append(o)
        lses.append(m[..., 0] + jnp.log(den))
    return _mix_dilations(outs, lses)


def _even_mixer(h, w_in, ln_g, ln_b, w_s, b_s, w_out, k_buf, v_buf):
    N, T, _ = h.shape
    z = h @ w_in
    u_a = jax.nn.gelu(z[..., :W_A])
    v_a = _layernorm(jax.nn.gelu(z[..., W_A:2 * W_A]), ln_g, ln_b)
    q = z[..., 2 * W_A:2 * W_A + W_B].reshape(N, T, H_B, HD_B)
    k = z[..., 2 * W_A + W_B:2 * W_A + 2 * W_B].reshape(N, T, H_B, HD_B)
    v = z[..., 2 * W_A + 2 * W_B:].reshape(N, T, H_B, HD_B)
    a_out = _chunk_sgu(u_a.reshape(N, T, G_A, A_GROUP_DIM), v_a.reshape(N, T, G_A, A_GROUP_DIM),
                       w_s, b_s).reshape(N, T, W_A)
    if k_buf is None:
        b_out = _dilated_prompt(q, k, v)
        keep = min(WIN_MAX, T)
        k_new, v_new = k[:, T - keep:], v[:, T - keep:]
    else:
        kc = jnp.concatenate([k_buf.astype(k.dtype), k], axis=1)
        vc = jnp.concatenate([v_buf.astype(v.dtype), v], axis=1)
        b_out = _dilated_sample(q, kc, vc)
        k_new, v_new = kc[:, T:], vc[:, T:]
    out = jnp.concatenate([a_out, b_out.reshape(N, T, W_B)], axis=-1) @ w_out
    return out, v_a, k_new, v_new


def _rglru_mixer(h, w_in, conv_w, conv_b, w_a, b_a, w_x, b_x, lam, w_out, conv_buf, h0):
    N, T, _ = h.shape
    z = h @ w_in
    gate = z[..., :D_RNN]
    xr = z[..., D_RNN:]
    xc, new_buf = _causal_dwconv(xr, conv_buf, conv_w, conv_b)
    xb = xc.reshape(N, T, RG_HEADS, RG_BLOCK)
    r = jax.nn.sigmoid(jnp.einsum('ntgi,gio->ntgo', xb, w_a).reshape(N, T, D_RNN) + b_a)
    i = jax.nn.sigmoid(jnp.einsum('ntgi,gio->ntgo', xb, w_x).reshape(N, T, D_RNN) + b_x)
    log_a = -RG_C * r.astype(jnp.float32) * jax.nn.softplus(-lam.astype(jnp.float32))
    a = jnp.exp(log_a)
    bx = jnp.sqrt(-jnp.expm1(2.0 * log_a)) * (i * xc).astype(jnp.float32)
    bx = bx.at[:, 0].add(a[:, 0] * h0.astype(jnp.float32))

    def comb(lhs, rhs):
        return (lhs[0] * rhs[0], rhs[0] * lhs[1] + rhs[1])

    _, hs = lax.associative_scan(comb, (a, bx), axis=1)
    y = jax.nn.gelu(gate) * hs.astype(h.dtype)
    return y @ w_out, new_buf, hs[:, -1].astype(h0.dtype)


def _conv_ffn(h, w_up, conv_w, conv_b, w_down, buf):
    up = h @ w_up
    uc, new_buf = _causal_dwconv(up, buf, conv_w, conv_b)
    a, b = jnp.split(uc, 2, axis=-1)
    return (jax.nn.silu(a) * b) @ w_down, new_buf


def _trunk(x, c, win_k, win_v, rg_conv, rg_h, ffn_buf, P):
    n = x.shape[0]
    win_k_new, win_v_new, chunk_v, rg_conv_new, rg_h_new, ffn_new = [], [], [], [], [], []
    for l in range(DEPTH):
        mod = (jax.nn.silu(c) @ P['w_ada'][l] + P['b_ada'][l])[:, None, :]
        sh1, sc1, g1, sh2, sc2, g2 = jnp.split(mod, 6, axis=-1)
        h = _rmsnorm(x) * (1 + sc1) + sh1
        if l % 2 == 0:
            e = l // 2
            kb = None if win_k is None else win_k[e]
            vb = None if win_v is None else win_v[e]
            mo, v_a, kn, vn = _even_mixer(h, P['w_in_even'][e], P['ln_v_g'][e], P['ln_v_b'][e],
                                          P['w_sgu'][e], P['b_sgu'][e], P['w_out_even'][e], kb, vb)
            win_k_new.append(kn)
            win_v_new.append(vn)
            chunk_v.append(v_a)
        else:
            o = l // 2
            cb = jnp.zeros((n, RG_CONV - 1, D_RNN), x.dtype) if rg_conv is None else rg_conv[o]
            h0 = jnp.zeros((n, D_RNN), jnp.float32) if rg_h is None else rg_h[o]
            mo, cn, hn = _rglru_mixer(h, P['w_in_odd'][o], P['rg_conv_w'][o], P['rg_conv_b'][o],
                                      P['rg_w_a'][o], P['rg_b_a'][o], P['rg_w_x'][o], P['rg_b_x'][o],
                                      P['rg_lambda'][o], P['w_out_odd'][o], cb, h0)
            rg_conv_new.append(cn)
            rg_h_new.append(hn)
        x = x + g1 * mo
        h = _rmsnorm(x) * (1 + sc2) + sh2
        fb = jnp.zeros((n, FFN_CONV - 1, 2 * D_FF), x.dtype) if ffn_buf is None else ffn_buf[l]
        fo, fn = _conv_ffn(h, P['ffn_w_up'][l], P['ffn_conv_w'][l], P['ffn_conv_b'][l],
                           P['ffn_w_down'][l], fb)
        ffn_new.append(fn)
        x = x + g2 * fo
    y = _rmsnorm(x) * P['final_g']
    return (y, jnp.stack(win_k_new), jnp.stack(win_v_new), jnp.stack(chunk_v),
            jnp.stack(rg_conv_new), jnp.stack(rg_h_new), jnp.stack(ffn_new))


def setup_inputs(seed: int = 0) -> dict:
    key = jax.random.key(seed)
    ks = iter(jax.random.split(key, 48))

    def nrm(shape, s):
        return jax.random.normal(next(ks), shape, jnp.float32) * s

    L = min(WIN_MAX, PAST_LEN)
    F2 = 2 * D_FF
    a_init = jax.random.uniform(next(ks), (N_ODD, D_RNN), jnp.float32, 0.9, 0.999)
    return {
        'x_prompt': nrm((BATCH, SEQ, D_MODEL), 1.0),
        'x_sample': nrm((DEC_BATCH, DEC_SEQ, D_MODEL), 1.0),
        'cache_win_k': nrm((N_EVEN, DEC_BATCH, L, H_B, HD_B), 1.0),
        'cache_win_v': nrm((N_EVEN, DEC_BATCH, L, H_B, HD_B), 1.0),
        'state_rglru_conv': nrm((N_ODD, DEC_BATCH, RG_CONV - 1, D_RNN), 0.5),
        'state_rglru_h': nrm((N_ODD, DEC_BATCH, D_RNN), 0.5),
        'state_ffn_conv': nrm((DEPTH, DEC_BATCH, FFN_CONV - 1, F2), 0.5),
        'c_prompt': nrm((BATCH, D_MODEL), 1.0),
        'c_sample': nrm((DEC_BATCH, D_MODEL), 1.0),
        'w_ada': nrm((DEPTH, D_MODEL, 6 * D_MODEL), 0.5 * D_MODEL ** -0.5),
        'b_ada': nrm((DEPTH, 6 * D_MODEL), 0.02),
        'w_in_even': nrm((N_EVEN, D_MODEL, N_IN_EVEN), D_MODEL ** -0.5),
        'ln_v_g': 1.0 + nrm((N_EVEN, W_A), 0.02),
        'ln_v_b': nrm((N_EVEN, W_A), 0.02),
        'w_sgu': nrm((N_EVEN, G_A, CHUNK, CHUNK), CHUNK ** -0.5),
        'b_sgu': 1.0 + nrm((N_EVEN, G_A, CHUNK), 0.02),
        'w_out_even': nrm((N_EVEN, W_A + W_B, D_MODEL), (W_A + W_B) ** -0.5),
        'w_in_odd': nrm((N_ODD, D_MODEL, 2 * D_RNN), D_MODEL ** -0.5),
        'rg_conv_w': nrm((N_ODD, RG_CONV, D_RNN), RG_CONV ** -0.5),
        'rg_conv_b': nrm((N_ODD, D_RNN), 0.02),
        'rg_w_a': nrm((N_ODD, RG_HEADS, RG_BLOCK, RG_BLOCK), RG_BLOCK ** -0.5),
        'rg_b_a': nrm((N_ODD, D_RNN), 0.02),
        'rg_w_x': nrm((N_ODD, RG_HEADS, RG_BLOCK, RG_BLOCK), RG_BLOCK ** -0.5),
        'rg_b_x': nrm((N_ODD, D_RNN), 0.02),
        'rg_lambda': jnp.log(a_init) - jnp.log1p(-a_init),
        'w_out_odd': nrm((N_ODD, D_RNN, D_MODEL), D_RNN ** -0.5),
        'ffn_w_up': nrm((DEPTH, D_MODEL, F2), D_MODEL ** -0.5),
        'ffn_conv_w': nrm((DEPTH, FFN_CONV, F2), FFN_CONV ** -0.5),
        'ffn_conv_b': nrm((DEPTH, F2), 0.02),
        'ffn_w_down': nrm((DEPTH, D_FF, D_MODEL), D_FF ** -0.5),
        'final_g': 1.0 + nrm((D_MODEL,), 0.02),
    }


def reference(x_prompt, x_sample, cache_win_k, cache_win_v, state_rglru_conv, state_rglru_h,
              state_ffn_conv, c_prompt, c_sample, w_ada, b_ada, w_in_even, ln_v_g, ln_v_b,
              w_sgu, b_sgu, w_out_even, w_in_odd, rg_conv_w, rg_conv_b, rg_w_a, rg_b_a,
              rg_w_x, rg_b_x, rg_lambda, w_out_odd, ffn_w_up, ffn_conv_w, ffn_conv_b,
              ffn_w_down, final_g):
    P = dict(w_ada=w_ada, b_ada=b_ada, w_in_even=w_in_even, ln_v_g=ln_v_g, ln_v_b=ln_v_b,
             w_sgu=w_sgu, b_sgu=b_sgu, w_out_even=w_out_even, w_in_odd=w_in_odd,
             rg_conv_w=rg_conv_w, rg_conv_b=rg_conv_b, rg_w_a=rg_w_a, rg_b_a=rg_b_a,
             rg_w_x=rg_w_x, rg_b_x=rg_b_x, rg_lambda=rg_lambda, w_out_odd=w_out_odd,
             ffn_w_up=ffn_w_up, ffn_conv_w=ffn_conv_w, ffn_conv_b=ffn_conv_b,
             ffn_w_down=ffn_w_down, final_g=final_g)
    (y_prompt, win_k_prompt, win_v_prompt, _unused_prompt_chunk, rglru_conv_prompt,
     rglru_h_prompt, ffn_conv_prompt) = _trunk(x_prompt, c_prompt, None, None, None, None, None, P)
    (y_sample, win_k_sample, win_v_sample, chunk_v_sample, rglru_conv_sample,
     rglru_h_sample, ffn_conv_sample) = _trunk(x_sample, c_sample, cache_win_k, cache_win_v,
                                               state_rglru_conv, state_rglru_h, state_ffn_conv, P)
    return (y_prompt, y_sample, win_k_prompt, win_v_prompt, rglru_conv_prompt, rglru_h_prompt,
            ffn_conv_prompt, chunk_v_sample, win_k_sample, win_v_sample, rglru_conv_sample,
            rglru_h_sample, ffn_conv_sample)
```

```python
import functools

import jax
import jax.numpy as jnp
from jax import lax
from jax.experimental import pallas as pl
from jax.experimental.pallas import tpu as pltpu

F32 = jnp.float32
BF16 = jnp.bfloat16

D_MODEL = 1024
BATCH = 4
SEQ = 4096
DEC_BATCH = 32
W_A = 512
A_GROUP = 128
G_A = 4
CHUNK = 128
W_B = 512
HD = 64
N_HEADS = 8
DILATIONS = (1, 4, 16)
N_BACK = 128
WIN = 2048
N_IN_EVEN = 2 * W_A + 3 * W_B
D_RNN = 1024
RG_BLOCK = 128
RG_HEADS = 8
RG_C = 8.0
D_FF = 2816
EPS = 1e-6
NEG_INF = -1e30

LANES = 128
SUBLANES = 8
TM = 512
TILES_PER_SEQ = SEQ // TM
N_TILES = BATCH * TILES_PER_SEQ
FF_CHUNK = 256
N_FF_CHUNKS = D_FF // FF_CHUNK
N_SLABS = W_B // LANES
VMEM_LIMIT = 56 * 1024 * 1024


def _cparams(*sem):
    return pltpu.CompilerParams(dimension_semantics=sem, vmem_limit_bytes=VMEM_LIMIT)


def _const_spec(shape):
    nd = len(shape)
    return pl.BlockSpec(shape, lambda *_: (0,) * nd, pipeline_mode=pl.Buffered(1))


def _rms_mod(x, scale, shift):
    xn = x * lax.rsqrt(jnp.mean(x * x, axis=-1, keepdims=True) + EPS)
    return xn * (1.0 + scale) + shift


def _rmsnorm(x):
    return x * lax.rsqrt(jnp.mean(x * x, axis=-1, keepdims=True) + EPS)


def _layernorm(x, g, b):
    mu = jnp.mean(x, axis=-1, keepdims=True)
    xc = x - mu
    var = jnp.mean(xc * xc, axis=-1, keepdims=True)
    return xc * lax.rsqrt(var + EPS) * g + b


def _dot(a, b):
    return jnp.dot(a, b, preferred_element_type=F32)


ADA_ROWS = 40
ADA_TN = 1024


def _ada_kernel(c_ref, w_ref, b_ref, o_ref):
    c = c_ref[...]
    s = (c * jax.nn.sigmoid(c)).astype(BF16)
    o_ref[0] = _dot(s, w_ref[0].astype(BF16)) + b_ref[0]


def _ada(c_all, w_ada, b_ada):
    depth = w_ada.shape[0]
    n_out = w_ada.shape[2]
    return pl.pallas_call(
        _ada_kernel,
        grid=(depth, n_out // ADA_TN),
        in_specs=[
            pl.BlockSpec((ADA_ROWS, D_MODEL), lambda l, j: (0, 0)),
            pl.BlockSpec((1, D_MODEL, ADA_TN), lambda l, j: (l, 0, j)),
            pl.BlockSpec((1, 1, ADA_TN), lambda l, j: (l, 0, j)),
        ],
        out_specs=pl.BlockSpec((1, ADA_ROWS, ADA_TN), lambda l, j: (l, 0, j)),
        out_shape=jax.ShapeDtypeStruct((depth, ADA_ROWS, n_out), F32),
        compiler_params=_cparams("arbitrary", "arbitrary"),
        name="ada_mod",
    )(c_all, w_ada, b_ada.reshape(depth, 1, n_out))


def _even_in_kernel(x_ref, mod_ref, w_ref, lng_ref, lnb_ref, ws_ref, bst_ref,
                    a_ref, q_ref, k_ref, v_ref, kt_ref, vt_ref):
    i = pl.program_id(0)
    m = mod_ref[0]
    h = _rms_mod(x_ref[...], m[1:2], m[0:1]).astype(BF16)
    u = jax.nn.gelu(_dot(h, w_ref[:, 0:W_A]))
    va = _layernorm(jax.nn.gelu(_dot(h, w_ref[:, W_A:2 * W_A])), lng_ref[...], lnb_ref[...])
    vab = va.astype(BF16)
    row = lax.broadcasted_iota(jnp.int32, (CHUNK, CHUNK), 0)
    col = lax.broadcasted_iota(jnp.int32, (CHUNK, CHUNK), 1)
    causal = col <= row
    bst = bst_ref[...]
    for g in range(G_A):
        wg = jnp.where(causal, ws_ref[g], 0.0).astype(BF16)
        bias = bst[:, g:g + 1]
        lo, hi = g * A_GROUP, (g + 1) * A_GROUP
        for c in range(TM // CHUNK):
            r0, r1 = c * CHUNK, (c + 1) * CHUNK
            mix = _dot(wg, vab[r0:r1, lo:hi]) + bias
            a_ref[r0:r1, lo:hi] = (u[r0:r1, lo:hi] * mix).astype(BF16)
    base = 2 * W_A
    q = _dot(h, w_ref[:, base:base + W_B])
    k = _dot(h, w_ref[:, base + W_B:base + 2 * W_B])
    v = _dot(h, w_ref[:, base + 2 * W_B:base + 3 * W_B])
    for s in range(N_SLABS):
        q_ref[s] = q[:, s * LANES:(s + 1) * LANES]
        k_ref[s] = k[:, s * LANES:(s + 1) * LANES]
        v_ref[s] = v[:, s * LANES:(s + 1) * LANES]

    @pl.when(i % TILES_PER_SEQ >= TILES_PER_SEQ - WIN // TM)
    def _():
        kt_ref[0] = k.T
        vt_ref[0] = v.T


def _even_in(x2d, mod, w_in, ln_g, ln_b, w_s, b_s_t):
    n_tok = x2d.shape[0]
    first_win_tile = TILES_PER_SEQ - WIN // TM
    qkv_shape = jax.ShapeDtypeStruct((N_SLABS, n_tok, LANES), F32)
    qkv_spec = pl.BlockSpec((N_SLABS, TM, LANES), lambda i: (0, i, 0))
    win_shape = jax.ShapeDtypeStruct((BATCH, W_B, WIN), F32)
    win_spec = pl.BlockSpec(
        (1, W_B, TM),
        lambda i: (i // TILES_PER_SEQ, 0, jnp.maximum(i % TILES_PER_SEQ - first_win_tile, 0)))
    return pl.pallas_call(
        _even_in_kernel,
        grid=(N_TILES,),
        in_specs=[
            pl.BlockSpec((TM, D_MODEL), lambda i: (i, 0)),
            pl.BlockSpec((1, 6, D_MODEL), lambda i: (i // TILES_PER_SEQ, 0, 0)),
            _const_spec((D_MODEL, N_IN_EVEN)),
            _const_spec((1, W_A)),
            _const_spec((1, W_A)),
            _const_spec((G_A, CHUNK, CHUNK)),
            _const_spec((CHUNK, G_A)),
        ],
        out_specs=[
            pl.BlockSpec((TM, W_A), lambda i: (i, 0)),
            qkv_spec, qkv_spec, qkv_spec, win_spec, win_spec,
        ],
        out_shape=[
            jax.ShapeDtypeStruct((n_tok, W_A), BF16),
            qkv_shape, qkv_shape, qkv_shape, win_shape, win_shape,
        ],
        compiler_params=_cparams("arbitrary"),
        name="even_in",
    )(x2d, mod, w_in, ln_g, ln_b, w_s, b_s_t)


ATT_BLK = 128


def _attn_kernel(q_ref, k_ref, v_ref, o_ref, acc_ref, lse_ref):
    q2, k2, v2 = q_ref.at[0], k_ref.at[0], v_ref.at[0]
    qi = lax.broadcasted_iota(jnp.int32, (ATT_BLK, 2 * ATT_BLK), 0)
    kj = lax.broadcasted_iota(jnp.int32, (ATT_BLK, 2 * ATT_BLK), 1)
    dist = qi + ATT_BLK - kj
    band = (dist >= 0) & (dist <= N_BACK)
    bias_full = jnp.where(band, 0.0, NEG_INF).astype(F32)
    bias_first = jnp.where(band & (kj >= ATT_BLK), 0.0, NEG_INF).astype(F32)
    lane = lax.broadcasted_iota(jnp.int32, (ATT_BLK, LANES), 1)
    head0 = lane < HD

    for p, d in enumerate(DILATIONS):
        n_blk = SEQ // (d * ATT_BLK)

        def unit(idx, carry, p=p, d=d, n_blk=n_blk):
            c = idx // n_blk
            b = idx % n_blk
            own = c + d * ATT_BLK * b
            prev = c + d * ATT_BLK * jnp.maximum(b - 1, 0)
            if d == 1:
                sl = lambda s: pl.ds(pl.multiple_of(s, ATT_BLK), ATT_BLK)
            else:
                sl = lambda s: pl.ds(s, ATT_BLK, stride=d)
            qb = q2[sl(own), :] * (HD ** -0.5)
            kb = jnp.concatenate([k2[sl(prev), :], k2[sl(own), :]], axis=0).astype(BF16)
            vb = jnp.concatenate([v2[sl(prev), :], v2[sl(own), :]], axis=0).astype(BF16)
            bias = jnp.where(b == 0, bias_first, bias_full)
            outs, lses = [], []
            for hh in range(2):
                sel = head0 if hh == 0 else jnp.logical_not(head0)
                qm = jnp.where(sel, qb, 0.0).astype(BF16)
                s = lax.dot_general(qm, kb, (((1,), (1,)), ((), ())),
                                    preferred_element_type=F32) + bias
                mx = jnp.max(s, axis=-1, keepdims=True)
                e = jnp.exp(s - mx)
                den = jnp.sum(e, axis=-1, keepdims=True)
                outs.append(_dot(e.astype(BF16), vb) / den)
                lses.append(mx + jnp.log(den))
            acc_ref[p, sl(own), :] = jnp.where(head0, outs[0], outs[1])
            lse_ref[p, sl(own), :] = jnp.where(head0, lses[0], lses[1])
            return carry

        lax.fori_loop(0, SEQ // ATT_BLK, unit, 0)

    def mix(t, carry):
        rows = pl.ds(pl.multiple_of(t * TM, TM), TM)
        l0, l1, l2 = lse_ref[0, rows, :], lse_ref[1, rows, :], lse_ref[2, rows, :]
        mx = jnp.maximum(jnp.maximum(l0, l1), l2)
        e0, e1, e2 = jnp.exp(l0 - mx), jnp.exp(l1 - mx), jnp.exp(l2 - mx)
        num = e0 * acc_ref[0, rows, :] + e1 * acc_ref[1, rows, :] + e2 * acc_ref[2, rows, :]
        o_ref[rows, :] = (num / (e0 + e1 + e2)).astype(BF16)
        return carry

    lax.fori_loop(0, SEQ // TM, mix, 0)


def _attn(q, k, v):
    n_tok = q.shape[1]
    spec = pl.BlockSpec((1, SEQ, LANES), lambda n, s: (s, n, 0))
    return pl.pallas_call(
        _attn_kernel,
        grid=(BATCH, N_SLABS),
        in_specs=[spec, spec, spec],
        out_specs=pl.BlockSpec((SEQ, LANES), lambda n, s: (n, s)),
        out_shape=jax.ShapeDtypeStruct((n_tok, W_B), BF16),
        scratch_shapes=[
            pltpu.VMEM((len(DILATIONS), SEQ, LANES), F32),
            pltpu.VMEM((len(DILATIONS), SEQ, LANES), F32),
        ],
        compiler_params=_cparams("arbitrary", "arbitrary"),
        name="dil_attn",
    )(q, k, v)


def _even_out_kernel(x_ref, mod_ref, a_ref, b_ref, w_ref, o_ref):
    m = mod_ref[0]
    mo = _dot(a_ref[...], w_ref[0:W_A, :]) + _dot(b_ref[...], w_ref[W_A:W_A + W_B, :])
    o_ref[...] = x_ref[...] + m[2:3] * mo


def _even_out(x2d, mod, a_out, b_out, w_out):
    n_tok = x2d.shape[0]
    return pl.pallas_call(
        _even_out_kernel,
        grid=(n_tok // TM,),
        in_specs=[
            pl.BlockSpec((TM, D_MODEL), lambda i: (i, 0)),
            pl.BlockSpec((1, 6, D_MODEL), lambda i: (i // TILES_PER_SEQ, 0, 0)),
            pl.BlockSpec((TM, W_A), lambda i: (i, 0)),
            pl.BlockSpec((TM, W_B), lambda i: (i, 0)),
            _const_spec((W_A + W_B, D_MODEL)),
        ],
        out_specs=pl.BlockSpec((TM, D_MODEL), lambda i: (i, 0)),
        out_shape=jax.ShapeDtypeStruct((n_tok, D_MODEL), F32),
        compiler_params=_cparams("arbitrary"),
        name="even_out",
    )(x2d, mod, a_out, b_out, w_out)


def _ffn_kernel(x_ref, mod_ref, wup_ref, cw_ref, cb_ref, wdn_ref, fg_ref,
                o_ref, st_ref, carry_ref, buf_ref, act_ref, *, final):
    i = pl.program_id(0)

    @pl.when(i % TILES_PER_SEQ == 0)
    def _():
        carry_ref[...] = jnp.zeros_like(carry_ref)

    x = x_ref[...]
    m = mod_ref[0]
    h = _rms_mod(x, m[4:5], m[3:4]).astype(BF16)
    cw = cw_ref[...]
    cb = cb_ref[...]
    for j in range(N_FF_CHUNKS):
        conv = []
        for half in range(2):
            c0 = half * D_FF + j * FF_CHUNK
            cols = slice(c0, c0 + FF_CHUNK)
            up = _dot(h, wup_ref[:, cols])
            buf_ref[half, 0:SUBLANES, :] = carry_ref[:, cols]
            buf_ref[half, SUBLANES:, :] = up
            carry_ref[:, cols] = up[TM - SUBLANES:, :]
            prev1 = buf_ref[half, SUBLANES - 1:SUBLANES - 1 + TM, :]
            prev2 = buf_ref[half, SUBLANES - 2:SUBLANES - 2 + TM, :]
            conv.append(cb[:, cols] + cw[0:1, cols] * prev2 + cw[1:2, cols] * prev1
                        + cw[2:3, cols] * up)
        act = conv[0] * jax.nn.sigmoid(conv[0]) * conv[1]
        act_ref[:, j * FF_CHUNK:(j + 1) * FF_CHUNK] = act.astype(BF16)
    st_ref[0] = carry_ref[...]
    y = x + m[5:6] * _dot(act_ref[...], wdn_ref[...])
    if final:
        y = _rmsnorm(y) * fg_ref[...]
    o_ref[...] = y


def _ffn(x2d, mod, w_up, conv_w, conv_b, w_down, final_g, final):
    n_tok = x2d.shape[0]
    f2 = 2 * D_FF
    return pl.pallas_call(
        functools.partial(_ffn_kernel, final=final),
        grid=(n_tok // TM,),
        in_specs=[
            pl.BlockSpec((TM, D_MODEL), lambda i: (i, 0)),
            pl.BlockSpec((1, 6, D_MODEL), lambda i: (i // TILES_PER_SEQ, 0, 0)),
            _const_spec((D_MODEL, f2)),
            _const_spec((3, f2)),
            _const_spec((1, f2)),
            _const_spec((D_FF, D_MODEL)),
            _const_spec((1, D_MODEL)),
        ],
        out_specs=[
            pl.BlockSpec((TM, D_MODEL), lambda i: (i, 0)),
            pl.BlockSpec((1, SUBLANES, f2), lambda i: (i // TILES_PER_SEQ, 0, 0)),
        ],
        out_shape=[
            jax.ShapeDtypeStruct((n_tok, D_MODEL), F32),
            jax.ShapeDtypeStruct((BATCH, SUBLANES, f2), F32),
        ],
        scratch_shapes=[
            pltpu.VMEM((SUBLANES, f2), F32),
            pltpu.VMEM((2, TM + SUBLANES, FF_CHUNK), F32),
            pltpu.VMEM((TM, D_FF), BF16),
        ],
        compiler_params=_cparams("arbitrary"),
        name="conv_ffn_final" if final else "conv_ffn",
    )(x2d, mod, w_up, conv_w, conv_b, w_down, final_g)


N_GROUPS = TM // SUBLANES


def _rg_gates(xc, wa_ref, ba, wx_ref, bx, lam):
    xcb = xc.astype(BF16)
    r_parts, i_parts = [], []
    for g in range(RG_HEADS):
        blk = xcb[:, g * RG_BLOCK:(g + 1) * RG_BLOCK]
        r_parts.append(_dot(blk, wa_ref[g]))
        i_parts.append(_dot(blk, wx_ref[g]))
    r = jax.nn.sigmoid(jnp.concatenate(r_parts, axis=-1) + ba)
    ig = jax.nn.sigmoid(jnp.concatenate(i_parts, axis=-1) + bx)
    log_a = (-RG_C) * r * jax.nn.softplus(-lam)
    a = jnp.exp(log_a)
    b = jnp.sqrt(1.0 - a * a) * (ig * xc)
    return a, b


def _rglru_kernel(x_ref, mod_ref, win_ref, cw_ref, cb_ref, wa_ref, ba_ref, wx_ref, bx_ref,
                  lam_ref, wout_ref, o_ref, cst_ref, hst_ref,
                  xbuf_ref, hcar_ref, a_ref, b_ref, hs_ref):
    i = pl.program_id(0)

    @pl.when(i % TILES_PER_SEQ == 0)
    def _():
        xbuf_ref[0:SUBLANES, :] = jnp.zeros((SUBLANES, D_RNN), F32)
        hcar_ref[...] = jnp.zeros_like(hcar_ref)

    x = x_ref[...]
    m = mod_ref[0]
    h = _rms_mod(x, m[1:2], m[0:1]).astype(BF16)
    gate = _dot(h, win_ref[:, 0:D_RNN])
    xr = _dot(h, win_ref[:, D_RNN:2 * D_RNN])
    xbuf_ref[SUBLANES:, :] = xr
    cw = cw_ref[...]
    xc = (cb_ref[...] + cw[3:4] * xr
          + cw[2:3] * xbuf_ref[SUBLANES - 1:SUBLANES - 1 + TM, :]
          + cw[1:2] * xbuf_ref[SUBLANES - 2:SUBLANES - 2 + TM, :]
          + cw[0:1] * xbuf_ref[SUBLANES - 3:SUBLANES - 3 + TM, :])
    last = xr[TM - SUBLANES:, :]
    xbuf_ref[0:SUBLANES, :] = last
    cst_ref[0] = last

    a, b = _rg_gates(xc, wa_ref, ba_ref[...], wx_ref, bx_ref[...], lam_ref[...])

    a3 = a.reshape(N_GROUPS, SUBLANES, D_RNN)
    b3 = b.reshape(N_GROUPS, SUBLANES, D_RNN)
    r_id = lax.broadcasted_iota(jnp.int32, (N_GROUPS, SUBLANES, D_RNN), 1)
    for s in (1, 2, 4):
        keep = r_id >= s
        a_sh = jnp.where(keep, pltpu.roll(a3, s, 1), 1.0)
        b_sh = jnp.where(keep, pltpu.roll(b3, s, 1), 0.0)
        b3 = a3 * b_sh + b3
        a3 = a3 * a_sh
    a_ref[...] = a3
    b_ref[...] = b3

    def step(g, hc):
        hs = a_ref[g] * hc + b_ref[g]
        hs_ref[g] = hs
        return hs[SUBLANES - 1:SUBLANES, :]

    hc = lax.fori_loop(0, N_GROUPS, step, hcar_ref[...])
    hcar_ref[...] = hc
    hst_ref[0] = hc

    y = (jax.nn.gelu(gate) * hs_ref[...].reshape(TM, D_RNN)).astype(BF16)
    o_ref[...] = x + m[2:3] * _dot(y, wout_ref[...])


def _rglru(x2d, mod, w_in, conv_w, conv_b, w_a, b_a, w_x, b_x, lam, w_out):
    n_tok = x2d.shape[0]
    return pl.pallas_call(
        _rglru_kernel,
        grid=(n_tok // TM,),
        in_specs=[
            pl.BlockSpec((TM, D_MODEL), lambda i: (i, 0)),
            pl.BlockSpec((1, 6, D_MODEL), lambda i: (i // TILES_PER_SEQ, 0, 0)),
            _const_spec((D_MODEL, 2 * D_RNN)),
            _const_spec((4, D_RNN)),
            _const_spec((1, D_RNN)),
            _const_spec((RG_HEADS, RG_BLOCK, RG_BLOCK)),
            _const_spec((1, D_RNN)),
            _const_spec((RG_HEADS, RG_BLOCK, RG_BLOCK)),
            _const_spec((1, D_RNN)),
            _const_spec((1, D_RNN)),
            _const_spec((D_RNN, D_MODEL)),
        ],
        out_specs=[
            pl.BlockSpec((TM, D_MODEL), lambda i: (i, 0)),
            pl.BlockSpec((1, SUBLANES, D_RNN), lambda i: (i // TILES_PER_SEQ, 0, 0)),
            pl.BlockSpec((1, 1, D_RNN), lambda i: (i // TILES_PER_SEQ, 0, 0)),
        ],
        out_shape=[
            jax.ShapeDtypeStruct((n_tok, D_MODEL), F32),
            jax.ShapeDtypeStruct((BATCH, SUBLANES, D_RNN), F32),
            jax.ShapeDtypeStruct((BATCH, 1, D_RNN), F32),
        ],
        scratch_shapes=[
            pltpu.VMEM((TM + SUBLANES, D_RNN), F32),
            pltpu.VMEM((1, D_RNN), F32),
            pltpu.VMEM((N_GROUPS, SUBLANES, D_RNN), F32),
            pltpu.VMEM((N_GROUPS, SUBLANES, D_RNN), F32),
            pltpu.VMEM((N_GROUPS, SUBLANES, D_RNN), F32),
        ],
        compiler_params=_cparams("arbitrary"),
        name="rglru",
    )(x2d, mod, w_in, conv_w, conv_b, w_a, b_a, w_x, b_x, lam, w_out)


def _s_even_in_kernel(x_ref, mod_ref, w_ref, lng_ref, lnb_ref, ws0_ref, bs0_ref,
                      va_ref, a_ref, q_ref, k_ref, v_ref):
    mod = mod_ref[...]
    h = _rms_mod(x_ref[...], mod[:, D_MODEL:2 * D_MODEL], mod[:, 0:D_MODEL]).astype(BF16)
    u = jax.nn.gelu(_dot(h, w_ref[:, 0:W_A]))
    va = _layernorm(jax.nn.gelu(_dot(h, w_ref[:, W_A:2 * W_A])), lng_ref[...], lnb_ref[...])
    va_ref[...] = va
    a_ref[...] = (u * (ws0_ref[...] * va + bs0_ref[...])).astype(BF16)
    base = 2 * W_A
    q_ref[...] = _dot(h, w_ref[:, base:base + W_B])
    k_ref[...] = _dot(h, w_ref[:, base + W_B:base + 2 * W_B])
    v_ref[...] = _dot(h, w_ref[:, base + 2 * W_B:base + 3 * W_B])


def _s_even_in(x, mod, w_in, ln_g, ln_b, ws0, bs0):
    n = x.shape[0]
    f = jax.ShapeDtypeStruct((n, W_B), F32)
    return pl.pallas_call(
        _s_even_in_kernel,
        out_shape=[f, jax.ShapeDtypeStruct((n, W_A), BF16), f, f, f],
        compiler_params=pltpu.CompilerParams(vmem_limit_bytes=VMEM_LIMIT),
        name="s_even_in",
    )(x, mod, w_in, ln_g, ln_b, ws0, bs0)


def _s_attn_kernel(q_ref, kn_ref, vn_ref, kt_ref, vt_ref, o_ref, kto_ref, vto_ref):
    n_hd = N_HEADS * HD
    qrow = q_ref[0] * (HD ** -0.5)
    knrow = kn_ref[0]
    vnrow = vn_ref[0]
    hrow = lax.broadcasted_iota(jnp.int32, (N_HEADS, n_hd), 0)
    hcol = lax.broadcasted_iota(jnp.int32, (N_HEADS, n_hd), 1) // HD
    own = hrow == hcol
    qbd = jnp.where(own, jnp.broadcast_to(qrow, (N_HEADS, n_hd)), 0.0)
    kt = kt_ref[0].reshape(n_hd, WIN)
    vt = vt_ref[0].reshape(n_hd, WIN)
    s = _dot(qbd.astype(BF16), kt.astype(BF16))
    s_new = jnp.sum(qbd * knrow, axis=-1, keepdims=True)
    t = lax.broadcasted_iota(jnp.int32, (N_HEADS, WIN), 1)
    dist = WIN - t
    vtb = vt.astype(BF16)
    num = jnp.zeros((N_HEADS, n_hd), F32)
    outs, lses = [], []
    for d in DILATIONS:
        valid = ((dist & (d - 1)) == 0) & (dist <= N_BACK * d)
        sm = jnp.where(valid, s, NEG_INF)
        mx = jnp.maximum(jnp.max(sm, axis=-1, keepdims=True), s_new)
        e = jnp.where(valid, jnp.exp(sm - mx), 0.0)
        e_new = jnp.exp(s_new - mx)
        den = jnp.sum(e, axis=-1, keepdims=True) + e_new
        o = lax.dot_general(e.astype(BF16), vtb, (((1,), (1,)), ((), ())),
                            preferred_element_type=F32)
        outs.append((o + e_new * vnrow) / den)
        lses.append(mx + jnp.log(den))
    mx = jnp.maximum(jnp.maximum(lses[0], lses[1]), lses[2])
    ws = [jnp.exp(l - mx) for l in lses]
    num = ws[0] * outs[0] + ws[1] * outs[1] + ws[2] * outs[2]
    mixed = num / (ws[0] + ws[1] + ws[2])
    o_ref[0] = jnp.sum(jnp.where(own, mixed, 0.0), axis=0, keepdims=True)

    r2 = lax.broadcasted_iota(jnp.int32, (n_hd, n_hd), 0)
    c2 = lax.broadcasted_iota(jnp.int32, (n_hd, n_hd), 1)
    diag = r2 == c2
    kcol = jnp.sum(jnp.where(diag, jnp.broadcast_to(knrow, (n_hd, n_hd)), 0.0), axis=-1, keepdims=True)
    vcol = jnp.sum(jnp.where(diag, jnp.broadcast_to(vnrow, (n_hd, n_hd)), 0.0), axis=-1, keepdims=True)
    tt = lax.broadcasted_iota(jnp.int32, (n_hd, WIN), 1)
    is_last = tt == WIN - 1
    kto_ref[0] = jnp.where(is_last, kcol, pltpu.roll(kt, WIN - 1, 1)).reshape(N_HEADS, HD, WIN)
    vto_ref[0] = jnp.where(is_last, vcol, pltpu.roll(vt, WIN - 1, 1)).reshape(N_HEADS, HD, WIN)


def _s_attn(q, k_new, v_new, kt, vt):
    n = q.shape[0]
    n_hd = N_HEADS * HD
    row_spec = pl.BlockSpec((1, 1, n_hd), lambda i: (i, 0, 0))
    cache_spec = pl.BlockSpec((1, N_HEADS, HD, WIN), lambda i: (i, 0, 0, 0))
    cache_shape = jax.ShapeDtypeStruct((n, N_HEADS, HD, WIN), F32)
    return pl.pallas_call(
        _s_attn_kernel,
        grid=(n,),
        in_specs=[row_spec, row_spec, row_spec, cache_spec, cache_spec],
        out_specs=[row_spec, cache_spec, cache_spec],
        out_shape=[jax.ShapeDtypeStruct((n, 1, n_hd), F32), cache_shape, cache_shape],
        compiler_params=_cparams("arbitrary"),
        name="s_attn",
    )(q.reshape(n, 1, n_hd), k_new.reshape(n, 1, n_hd), v_new.reshape(n, 1, n_hd), kt, vt)


def _s_even_out_kernel(x_ref, mod_ref, a_ref, b_ref, w_ref, o_ref):
    g1 = mod_ref[:, 2 * D_MODEL:3 * D_MODEL]
    mo = _dot(a_ref[...], w_ref[0:W_A, :]) + _dot(b_ref[...].astype(BF16), w_ref[W_A:W_A + W_B, :])
    o_ref[...] = x_ref[...] + g1 * mo


def _s_even_out(x, mod, a_out, b_out, w_out):
    return pl.pallas_call(
        _s_even_out_kernel,
        out_shape=jax.ShapeDtypeStruct(x.shape, F32),
        compiler_params=pltpu.CompilerParams(vmem_limit_bytes=VMEM_LIMIT),
        name="s_even_out",
    )(x, mod, a_out, b_out, w_out)


def _s_ffn_kernel(x_ref, mod_ref, p2_ref, p1_ref, wup_ref, cw_ref, cb_ref, wdn_ref, fg_ref,
                  o_ref, up_ref, *, final):
    x = x_ref[...]
    mod = mod_ref[...]
    h = _rms_mod(x, mod[:, 4 * D_MODEL:5 * D_MODEL], mod[:, 3 * D_MODEL:4 * D_MODEL]).astype(BF16)
    up = _dot(h, wup_ref[...])
    up_ref[...] = up
    cw = cw_ref[...]
    conv = cb_ref[...] + cw[0:1] * p2_ref[...] + cw[1:2] * p1_ref[...] + cw[2:3] * up
    ca, cg = conv[:, 0:D_FF], conv[:, D_FF:2 * D_FF]
    act = (ca * jax.nn.sigmoid(ca) * cg).astype(BF16)
    y = x + mod[:, 5 * D_MODEL:6 * D_MODEL] * _dot(act, wdn_ref[...])
    if final:
        y = _rmsnorm(y) * fg_ref[...]
    o_ref[...] = y


def _s_ffn(x, mod, p2, p1, w_up, conv_w, conv_b, w_down, final_g, final):
    n = x.shape[0]
    return pl.pallas_call(
        functools.partial(_s_ffn_kernel, final=final),
        out_shape=[jax.ShapeDtypeStruct(x.shape, F32), jax.ShapeDtypeStruct((n, 2 * D_FF), F32)],
        compiler_params=pltpu.CompilerParams(vmem_limit_bytes=VMEM_LIMIT),
        name="s_conv_ffn_final" if final else "s_conv_ffn",
    )(x, mod, p2, p1, w_up, conv_w, conv_b, w_down, final_g)


def _s_rglru_kernel(x_ref, mod_ref, c0_ref, c1_ref, c2_ref, h0_ref, win_ref, cw_ref, cb_ref,
                    wa_ref, ba_ref, wx_ref, bx_ref, lam_ref, wout_ref, o_ref, xr_ref, hn_ref):
    x = x_ref[...]
    mod = mod_ref[...]
    h = _rms_mod(x, mod[:, D_MODEL:2 * D_MODEL], mod[:, 0:D_MODEL]).astype(BF16)
    gate = _dot(h, win_ref[:, 0:D_RNN])
    xr = _dot(h, win_ref[:, D_RNN:2 * D_RNN])
    xr_ref[...] = xr
    cw = cw_ref[...]
    xc = (cb_ref[...] + cw[0:1] * c0_ref[...] + cw[1:2] * c1_ref[...] + cw[2:3] * c2_ref[...]
          + cw[3:4] * xr)
    a, b = _rg_gates(xc, wa_ref, ba_ref[...], wx_ref, bx_ref[...], lam_ref[...])
    hn = a * h0_ref[...] + b
    hn_ref[...] = hn
    y = (jax.nn.gelu(gate) * hn).astype(BF16)
    o_ref[...] = x + mod[:, 2 * D_MODEL:3 * D_MODEL] * _dot(y, wout_ref[...])


def _s_rglru(x, mod, c0, c1, c2, h0, w_in, conv_w, conv_b, w_a, b_a, w_x, b_x, lam, w_out):
    f = jax.ShapeDtypeStruct(x.shape, F32)
    return pl.pallas_call(
        _s_rglru_kernel,
        out_shape=[f, f, f],
        compiler_params=pltpu.CompilerParams(vmem_limit_bytes=VMEM_LIMIT),
        name="s_rglru",
    )(x, mod, c0, c1, c2, h0, w_in, conv_w, conv_b, w_a, b_a, w_x, b_x, lam, w_out)


def kernel(x_prompt, x_sample, cache_win_k, cache_win_v, state_rglru_conv, state_rglru_h, state_ffn_conv, c_prompt, c_sample, w_ada, b_ada, w_in_even, ln_v_g, ln_v_b, w_sgu, b_sgu, w_out_even, w_in_odd, rg_conv_w, rg_conv_b, rg_w_a, rg_b_a, rg_w_x, rg_b_x, rg_lambda, w_out_odd, ffn_w_up, ffn_conv_w, ffn_conv_b, ffn_w_down, final_g):
    w_in_even_b = w_in_even[0].astype(BF16)
    w_out_even_b = w_out_even[0].astype(BF16)
    w_in_odd_b = w_in_odd[0].astype(BF16)
    w_out_odd_b = w_out_odd[0].astype(BF16)
    rg_w_a_b = rg_w_a[0].astype(BF16)
    rg_w_x_b = rg_w_x[0].astype(BF16)
    w_up_b = ffn_w_up.astype(BF16)
    w_down_b = ffn_w_down.astype(BF16)
    final_g2 = final_g.reshape(1, D_MODEL)

    pad = jnp.zeros((ADA_ROWS - BATCH - DEC_BATCH, D_MODEL), F32)
    mod = _ada(jnp.concatenate([c_prompt, c_sample, pad], axis=0), w_ada, b_ada)
    mod_p = mod[:, :BATCH].reshape(2, BATCH, 6, D_MODEL)
    mod_s = mod[:, BATCH:BATCH + DEC_BATCH]

    x = x_prompt.reshape(BATCH * SEQ, D_MODEL)
    a_out, q, k, v, kt_p, vt_p = _even_in(x, mod_p[0], w_in_even_b, ln_v_g, ln_v_b, w_sgu[0],
                                          b_sgu[0].T)
    b_out = _attn(q, k, v)
    x = _even_out(x, mod_p[0], a_out, b_out, w_out_even_b)
    x, ffn_st0 = _ffn(x, mod_p[0], w_up_b[0], ffn_conv_w[0], ffn_conv_b[0:1], w_down_b[0],
                      final_g2, False)
    x, rg_cst, rg_hst = _rglru(x, mod_p[1], w_in_odd_b, rg_conv_w[0], rg_conv_b, rg_w_a_b, rg_b_a,
                               rg_w_x_b, rg_b_x, rg_lambda, w_out_odd_b)
    y_p, ffn_st1 = _ffn(x, mod_p[1], w_up_b[1], ffn_conv_w[1], ffn_conv_b[1:2], w_down_b[1],
                        final_g2, True)

    y_prompt = y_p.reshape(BATCH, SEQ, D_MODEL)
    to_win = lambda t: t.reshape(1, BATCH, N_HEADS, HD, WIN).transpose(0, 1, 4, 2, 3)
    win_k_prompt = to_win(kt_p)
    win_v_prompt = to_win(vt_p)
    rglru_conv_prompt = rg_cst[None, :, SUBLANES - 3:, :]
    rglru_h_prompt = rg_hst.reshape(1, BATCH, D_RNN)
    ffn_conv_prompt = jnp.stack([ffn_st0[:, SUBLANES - 2:, :], ffn_st1[:, SUBLANES - 2:, :]])

    xs = x_sample.reshape(DEC_BATCH, D_MODEL)
    ws0 = jnp.repeat(w_sgu[0, :, 0, 0], A_GROUP).reshape(1, W_A)
    bs0 = jnp.repeat(b_sgu[0, :, 0], A_GROUP).reshape(1, W_A)
    va_s, a_s, q_s, k_s, v_s = _s_even_in(xs, mod_s[0], w_in_even_b, ln_v_g, ln_v_b, ws0, bs0)
    kt_c = cache_win_k[0].transpose(0, 2, 3, 1)
    vt_c = cache_win_v[0].transpose(0, 2, 3, 1)
    b_s, kt_n, vt_n = _s_attn(q_s, k_s, v_s, kt_c, vt_c)
    xs = _s_even_out(xs, mod_s[0], a_s, b_s.reshape(DEC_BATCH, W_B), w_out_even_b)
    st0 = state_ffn_conv[0]
    xs, up0 = _s_ffn(xs, mod_s[0], st0[:, 0], st0[:, 1], w_up_b[0], ffn_conv_w[0], ffn_conv_b[0:1],
                     w_down_b[0], final_g2, False)
    cst = state_rglru_conv[0]
    xs, xr_s, hn_s = _s_rglru(xs, mod_s[1], cst[:, 0], cst[:, 1], cst[:, 2], state_rglru_h[0],
                              w_in_odd_b, rg_conv_w[0], rg_conv_b, rg_w_a_b, rg_b_a, rg_w_x_b,
                              rg_b_x, rg_lambda, w_out_odd_b)
    st1 = state_ffn_conv[1]
    ys, up1 = _s_ffn(xs, mod_s[1], st1[:, 0], st1[:, 1], w_up_b[1], ffn_conv_w[1], ffn_conv_b[1:2],
                     w_down_b[1], final_g2, True)

    y_sample = ys.reshape(DEC_BATCH, 1, D_MODEL)
    chunk_v_sample = va_s.reshape(1, DEC_BATCH, 1, W_A)
    win_k_sample = kt_n.transpose(0, 3, 1, 2)[None]
    win_v_sample = vt_n.transpose(0, 3, 1, 2)[None]
    rglru_conv_sample = jnp.stack([cst[:, 1], cst[:, 2], xr_s], axis=1)[None]
    rglru_h_sample = hn_s[None]
    ffn_conv_sample = jnp.stack([jnp.stack([st0[:, 1], up0], axis=1),
                                 jnp.stack([st1[:, 1], up1], axis=1)])

    return (y_prompt, y_sample, win_k_prompt, win_v_prompt, rglru_conv_prompt, rglru_h_prompt,
            ffn_conv_prompt, chunk_v_sample, win_k_sample, win_v_sample, rglru_conv_sample,
            rglru_h_sample, ffn_conv_sample)
```

```python
import functools

import jax
import jax.numpy as jnp
from jax import lax
from jax.experimental import pallas as pl
from jax.experimental.pallas import tpu as pltpu

F32 = jnp.float32
BF16 = jnp.bfloat16

D_MODEL = 1024
BATCH = 4
SEQ = 4096
DEC_BATCH = 32
W_A = 512
A_GROUP = 128
G_A = 4
CHUNK = 128
W_B = 512
HD = 64
N_HEADS = 8
DILATIONS = (1, 4, 16)
N_BACK = 128
WIN = 2048
N_IN_EVEN = 2 * W_A + 3 * W_B
D_RNN = 1024
RG_BLOCK = 128
RG_HEADS = 8
RG_C = 8.0
D_FF = 2816
EPS = 1e-6
NEG_INF = -1e30
LOG2_E = 1.4426950408889634
LN_2 = 0.6931471805599453

LANES = 128
SUBLANES = 8
TM = 512
TILES_PER_SEQ = SEQ // TM
N_TILES = BATCH * TILES_PER_SEQ
FF_CHUNK = 256
N_FF_CHUNKS = D_FF // FF_CHUNK
N_SLABS = W_B // LANES
VMEM_LIMIT = 56 * 1024 * 1024


def _cparams(*sem):
    return pltpu.CompilerParams(dimension_semantics=sem, vmem_limit_bytes=VMEM_LIMIT)


def _const_spec(shape):
    nd = len(shape)
    return pl.BlockSpec(shape, lambda *_: (0,) * nd, pipeline_mode=pl.Buffered(1))


def _rms_mod(x, scale, shift):
    xn = x * lax.rsqrt(jnp.mean(x * x, axis=-1, keepdims=True) + EPS)
    return xn * (1.0 + scale) + shift


def _rmsnorm(x):
    return x * lax.rsqrt(jnp.mean(x * x, axis=-1, keepdims=True) + EPS)


def _layernorm(x, g, b):
    mu = jnp.mean(x, axis=-1, keepdims=True)
    xc = x - mu
    var = jnp.mean(xc * xc, axis=-1, keepdims=True)
    return xc * lax.rsqrt(var + EPS) * g + b


def _dot(a, b):
    return jnp.dot(a, b, preferred_element_type=F32)


ADA_ROWS = 40
ADA_TN = 1024


def _ada_kernel(c_ref, w_ref, b_ref, o_ref):
    c = c_ref[...]
    s = (c * jax.nn.sigmoid(c)).astype(BF16)
    o_ref[0] = _dot(s, w_ref[0].astype(BF16)) + b_ref[0]


def _ada(c_all, w_ada, b_ada):
    depth = w_ada.shape[0]
    n_out = w_ada.shape[2]
    return pl.pallas_call(
        _ada_kernel,
        grid=(depth, n_out // ADA_TN),
        in_specs=[
            pl.BlockSpec((ADA_ROWS, D_MODEL), lambda l, j: (0, 0)),
            pl.BlockSpec((1, D_MODEL, ADA_TN), lambda l, j: (l, 0, j)),
            pl.BlockSpec((1, 1, ADA_TN), lambda l, j: (l, 0, j)),
        ],
        out_specs=pl.BlockSpec((1, ADA_ROWS, ADA_TN), lambda l, j: (l, 0, j)),
        out_shape=jax.ShapeDtypeStruct((depth, ADA_ROWS, n_out), F32),
        compiler_params=_cparams("arbitrary", "arbitrary"),
        name="ada_mod",
    )(c_all, w_ada, b_ada.reshape(depth, 1, n_out))


def _even_in_kernel(x_ref, mod_ref, w_ref, lng_ref, lnb_ref, ws_ref, bst_ref,
                    a_ref, q_ref, k_ref, v_ref, kt_ref, vt_ref):
    i = pl.program_id(0)
    m = mod_ref[0]
    h = _rms_mod(x_ref[...], m[1:2], m[0:1]).astype(BF16)
    u = jax.nn.gelu(_dot(h, w_ref[:, 0:W_A]))
    va = _layernorm(jax.nn.gelu(_dot(h, w_ref[:, W_A:2 * W_A])), lng_ref[...], lnb_ref[...])
    vab = va.astype(BF16)
    row = lax.broadcasted_iota(jnp.int32, (CHUNK, CHUNK), 0)
    col = lax.broadcasted_iota(jnp.int32, (CHUNK, CHUNK), 1)
    causal = col <= row
    bst = bst_ref[...]
    for g in range(G_A):
        wg = jnp.where(causal, ws_ref[g], 0.0).astype(BF16)
        bias = bst[:, g:g + 1]
        lo, hi = g * A_GROUP, (g + 1) * A_GROUP
        for c in range(TM // CHUNK):
            r0, r1 = c * CHUNK, (c + 1) * CHUNK
            mix = _dot(wg, vab[r0:r1, lo:hi]) + bias
            a_ref[r0:r1, lo:hi] = (u[r0:r1, lo:hi] * mix).astype(BF16)
    base = 2 * W_A
    q = _dot(h, w_ref[:, base:base + W_B])
    k = _dot(h, w_ref[:, base + W_B:base + 2 * W_B])
    v = _dot(h, w_ref[:, base + 2 * W_B:base + 3 * W_B])
    for s in range(N_SLABS):
        q_ref[s] = q[:, s * LANES:(s + 1) * LANES]
        k_ref[s] = k[:, s * LANES:(s + 1) * LANES]
        v_ref[s] = v[:, s * LANES:(s + 1) * LANES]

    @pl.when(i % TILES_PER_SEQ >= TILES_PER_SEQ - WIN // TM)
    def _():
        kt_ref[0] = k.T
        vt_ref[0] = v.T


def _even_in(x2d, mod, w_in, ln_g, ln_b, w_s, b_s_t):
    n_tok = x2d.shape[0]
    first_win_tile = TILES_PER_SEQ - WIN // TM
    qkv_shape = jax.ShapeDtypeStruct((N_SLABS, n_tok, LANES), F32)
    qkv_spec = pl.BlockSpec((N_SLABS, TM, LANES), lambda i: (0, i, 0))
    win_shape = jax.ShapeDtypeStruct((BATCH, W_B, WIN), F32)
    win_spec = pl.BlockSpec(
        (1, W_B, TM),
        lambda i: (i // TILES_PER_SEQ, 0, jnp.maximum(i % TILES_PER_SEQ - first_win_tile, 0)))
    return pl.pallas_call(
        _even_in_kernel,
        grid=(N_TILES,),
        in_specs=[
            pl.BlockSpec((TM, D_MODEL), lambda i: (i, 0)),
            pl.BlockSpec((1, 6, D_MODEL), lambda i: (i // TILES_PER_SEQ, 0, 0)),
            _const_spec((D_MODEL, N_IN_EVEN)),
            _const_spec((1, W_A)),
            _const_spec((1, W_A)),
            _const_spec((G_A, CHUNK, CHUNK)),
            _const_spec((CHUNK, G_A)),
        ],
        out_specs=[
            pl.BlockSpec((TM, W_A), lambda i: (i, 0)),
            qkv_spec, qkv_spec, qkv_spec, win_spec, win_spec,
        ],
        out_shape=[
            jax.ShapeDtypeStruct((n_tok, W_A), BF16),
            qkv_shape, qkv_shape, qkv_shape, win_shape, win_shape,
        ],
        compiler_params=_cparams("arbitrary"),
        name="even_in",
    )(x2d, mod, w_in, ln_g, ln_b, w_s, b_s_t)


ATT_BLK = 128
ATT_UNROLL = 8


def _attn_kernel(q_ref, k_ref, v_ref, o_ref, acc_ref, max_ref, den_ref):
    q2, k2, v2 = q_ref.at[0], k_ref.at[0], v_ref.at[0]
    qi = lax.broadcasted_iota(jnp.int32, (ATT_BLK, 2 * ATT_BLK), 0)
    kj = lax.broadcasted_iota(jnp.int32, (ATT_BLK, 2 * ATT_BLK), 1)
    dist = qi + ATT_BLK - kj
    band = (dist >= 0) & (dist <= N_BACK)
    bias_full = jnp.where(band, 0.0, NEG_INF).astype(F32)
    bias_first = jnp.where(band & (kj >= ATT_BLK), 0.0, NEG_INF).astype(F32)
    lane = lax.broadcasted_iota(jnp.int32, (ATT_BLK, LANES), 1)
    head0 = lane < HD

    for p, d in enumerate(DILATIONS):
        n_blk = SEQ // (d * ATT_BLK)

        def unit(idx, carry, p=p, d=d, n_blk=n_blk):
            k_prev, v_prev = carry
            c = idx // n_blk
            b = idx % n_blk
            own = c + d * ATT_BLK * b
            if d == 1:
                rows = pl.ds(pl.multiple_of(own, ATT_BLK), ATT_BLK)
            else:
                rows = pl.ds(own, ATT_BLK, stride=d)
            qb = q2[rows, :] * (HD ** -0.5 * LOG2_E)
            k_own = k2[rows, :].astype(BF16)
            v_own = v2[rows, :].astype(BF16)
            kb = jnp.concatenate([k_prev, k_own], axis=0)
            vb = jnp.concatenate([v_prev, v_own], axis=0)
            bias = jnp.where(b == 0, bias_first, bias_full)
            outs, mxs, dens = [], [], []
            for hh in range(2):
                sel = head0 if hh == 0 else jnp.logical_not(head0)
                qm = jnp.where(sel, qb, 0.0).astype(BF16)
                s = lax.dot_general(qm, kb, (((1,), (1,)), ((), ())),
                                    preferred_element_type=F32) + bias
                mx = jnp.max(s, axis=-1, keepdims=True)
                e = jnp.exp2(s - mx)
                dens.append(jnp.sum(e, axis=-1, keepdims=True))
                mxs.append(mx)
                outs.append(_dot(e.astype(BF16), vb))
            acc_ref[p, rows, :] = jnp.where(head0, outs[0], outs[1])
            max_ref[p, rows, :] = jnp.where(head0, mxs[0], mxs[1])
            den_ref[p, rows, :] = jnp.where(head0, dens[0], dens[1])
            return k_own, v_own

        zeros = jnp.zeros((ATT_BLK, LANES), BF16)
        lax.fori_loop(0, SEQ // ATT_BLK, unit, (zeros, zeros), unroll=ATT_UNROLL)

    def mix(t, carry):
        rows = pl.ds(pl.multiple_of(t * TM, TM), TM)
        m0, m1, m2 = max_ref[0, rows, :], max_ref[1, rows, :], max_ref[2, rows, :]
        mx = jnp.maximum(jnp.maximum(m0, m1), m2)
        e0, e1, e2 = jnp.exp2(m0 - mx), jnp.exp2(m1 - mx), jnp.exp2(m2 - mx)
        num = e0 * acc_ref[0, rows, :] + e1 * acc_ref[1, rows, :] + e2 * acc_ref[2, rows, :]
        den = e0 * den_ref[0, rows, :] + e1 * den_ref[1, rows, :] + e2 * den_ref[2, rows, :]
        o_ref[rows, :] = (num / den).astype(BF16)
        return carry

    lax.fori_loop(0, SEQ // TM, mix, 0)


def _attn(q, k, v):
    n_tok = q.shape[1]
    spec = pl.BlockSpec((1, SEQ, LANES), lambda n, s: (s, n, 0))
    return pl.pallas_call(
        _attn_kernel,
        grid=(BATCH, N_SLABS),
        in_specs=[spec, spec, spec],
        out_specs=pl.BlockSpec((SEQ, LANES), lambda n, s: (n, s)),
        out_shape=jax.ShapeDtypeStruct((n_tok, W_B), BF16),
        scratch_shapes=[pltpu.VMEM((len(DILATIONS), SEQ, LANES), F32)] * 3,
        compiler_params=_cparams("arbitrary", "arbitrary"),
        name="dil_attn",
    )(q, k, v)


def _even_out_kernel(x_ref, mod_ref, a_ref, b_ref, w_ref, o_ref):
    m = mod_ref[0]
    mo = _dot(a_ref[...], w_ref[0:W_A, :]) + _dot(b_ref[...], w_ref[W_A:W_A + W_B, :])
    o_ref[...] = x_ref[...] + m[2:3] * mo


def _even_out(x2d, mod, a_out, b_out, w_out):
    n_tok = x2d.shape[0]
    return pl.pallas_call(
        _even_out_kernel,
        grid=(n_tok // TM,),
        in_specs=[
            pl.BlockSpec((TM, D_MODEL), lambda i: (i, 0)),
            pl.BlockSpec((1, 6, D_MODEL), lambda i: (i // TILES_PER_SEQ, 0, 0)),
            pl.BlockSpec((TM, W_A), lambda i: (i, 0)),
            pl.BlockSpec((TM, W_B), lambda i: (i, 0)),
            _const_spec((W_A + W_B, D_MODEL)),
        ],
        out_specs=pl.BlockSpec((TM, D_MODEL), lambda i: (i, 0)),
        out_shape=jax.ShapeDtypeStruct((n_tok, D_MODEL), F32),
        compiler_params=_cparams("arbitrary"),
        name="even_out",
    )(x2d, mod, a_out, b_out, w_out)


def _ffn_kernel(x_ref, mod_ref, wup_ref, cw_ref, cb_ref, wdn_ref, fg_ref,
                o_ref, st_ref, carry_ref, buf_ref, act_ref, *, final):
    i = pl.program_id(0)

    @pl.when(i % TILES_PER_SEQ == 0)
    def _():
        carry_ref[...] = jnp.zeros_like(carry_ref)

    x = x_ref[...]
    m = mod_ref[0]
    h = _rms_mod(x, m[4:5], m[3:4]).astype(BF16)
    cw = cw_ref[...]
    cb = cb_ref[...]
    for j in range(N_FF_CHUNKS):
        conv = []
        for half in range(2):
            c0 = half * D_FF + j * FF_CHUNK
            cols = slice(c0, c0 + FF_CHUNK)
            up = _dot(h, wup_ref[:, cols])
            buf_ref[half, 0:SUBLANES, :] = carry_ref[:, cols]
            buf_ref[half, SUBLANES:, :] = up
            carry_ref[:, cols] = up[TM - SUBLANES:, :]
            prev1 = buf_ref[half, SUBLANES - 1:SUBLANES - 1 + TM, :]
            prev2 = buf_ref[half, SUBLANES - 2:SUBLANES - 2 + TM, :]
            conv.append(cb[:, cols] + cw[0:1, cols] * prev2 + cw[1:2, cols] * prev1
                        + cw[2:3, cols] * up)
        act = conv[0] * jax.nn.sigmoid(conv[0]) * conv[1]
        act_ref[:, j * FF_CHUNK:(j + 1) * FF_CHUNK] = act.astype(BF16)
    st_ref[0] = carry_ref[...]
    y = x + m[5:6] * _dot(act_ref[...], wdn_ref[...])
    if final:
        y = _rmsnorm(y) * fg_ref[...]
    o_ref[...] = y


def _ffn(x2d, mod, w_up, conv_w, conv_b, w_down, final_g, final):
    n_tok = x2d.shape[0]
    f2 = 2 * D_FF
    return pl.pallas_call(
        functools.partial(_ffn_kernel, final=final),
        grid=(n_tok // TM,),
        in_specs=[
            pl.BlockSpec((TM, D_MODEL), lambda i: (i, 0)),
            pl.BlockSpec((1, 6, D_MODEL), lambda i: (i // TILES_PER_SEQ, 0, 0)),
            _const_spec((D_MODEL, f2)),
            _const_spec((3, f2)),
            _const_spec((1, f2)),
            _const_spec((D_FF, D_MODEL)),
            _const_spec((1, D_MODEL)),
        ],
        out_specs=[
            pl.BlockSpec((TM, D_MODEL), lambda i: (i, 0)),
            pl.BlockSpec((1, SUBLANES, f2), lambda i: (i // TILES_PER_SEQ, 0, 0)),
        ],
        out_shape=[
            jax.ShapeDtypeStruct((n_tok, D_MODEL), F32),
            jax.ShapeDtypeStruct((BATCH, SUBLANES, f2), F32),
        ],
        scratch_shapes=[
            pltpu.VMEM((SUBLANES, f2), F32),
            pltpu.VMEM((2, TM + SUBLANES, FF_CHUNK), F32),
            pltpu.VMEM((TM, D_FF), BF16),
        ],
        compiler_params=_cparams("arbitrary"),
        name="conv_ffn_final" if final else "conv_ffn",
    )(x2d, mod, w_up, conv_w, conv_b, w_down, final_g)


N_GROUPS = TM // SUBLANES


def _rg_gates(xc, wa_ref, ba, wx_ref, bx, lam):
    xcb = xc.astype(BF16)
    r_parts, i_parts = [], []
    for g in range(RG_HEADS):
        blk = xcb[:, g * RG_BLOCK:(g + 1) * RG_BLOCK]
        r_parts.append(_dot(blk, wa_ref[g]))
        i_parts.append(_dot(blk, wx_ref[g]))
    r = jax.nn.sigmoid(jnp.concatenate(r_parts, axis=-1) + ba)
    ig = jax.nn.sigmoid(jnp.concatenate(i_parts, axis=-1) + bx)
    log_a = (-RG_C) * r * jax.nn.softplus(-lam)
    a = jnp.exp(log_a)
    b = jnp.sqrt(1.0 - a * a) * (ig * xc)
    return a, b


def _rglru_kernel(x_ref, mod_ref, win_ref, cw_ref, cb_ref, wa_ref, ba_ref, wx_ref, bx_ref,
                  lam_ref, wout_ref, o_ref, cst_ref, hst_ref,
                  xbuf_ref, hcar_ref, a_ref, b_ref, hs_ref):
    i = pl.program_id(0)

    @pl.when(i % TILES_PER_SEQ == 0)
    def _():
        xbuf_ref[0:SUBLANES, :] = jnp.zeros((SUBLANES, D_RNN), F32)
        hcar_ref[...] = jnp.zeros_like(hcar_ref)

    x = x_ref[...]
    m = mod_ref[0]
    h = _rms_mod(x, m[1:2], m[0:1]).astype(BF16)
    gate = _dot(h, win_ref[:, 0:D_RNN])
    xr = _dot(h, win_ref[:, D_RNN:2 * D_RNN])
    xbuf_ref[SUBLANES:, :] = xr
    cw = cw_ref[...]
    xc = (cb_ref[...] + cw[3:4] * xr
          + cw[2:3] * xbuf_ref[SUBLANES - 1:SUBLANES - 1 + TM, :]
          + cw[1:2] * xbuf_ref[SUBLANES - 2:SUBLANES - 2 + TM, :]
          + cw[0:1] * xbuf_ref[SUBLANES - 3:SUBLANES - 3 + TM, :])
    last = xr[TM - SUBLANES:, :]
    xbuf_ref[0:SUBLANES, :] = last
    cst_ref[0] = last

    a, b = _rg_gates(xc, wa_ref, ba_ref[...], wx_ref, bx_ref[...], lam_ref[...])

    a3 = a.reshape(N_GROUPS, SUBLANES, D_RNN)
    b3 = b.reshape(N_GROUPS, SUBLANES, D_RNN)
    r_id = lax.broadcasted_iota(jnp.int32, (N_GROUPS, SUBLANES, D_RNN), 1)
    for s in (1, 2, 4):
        keep = r_id >= s
        a_sh = jnp.where(keep, pltpu.roll(a3, s, 1), 1.0)
        b_sh = jnp.where(keep, pltpu.roll(b3, s, 1), 0.0)
        b3 = a3 * b_sh + b3
        a3 = a3 * a_sh
    a_ref[...] = a3
    b_ref[...] = b3

    def step(g, hc):
        hs = a_ref[g] * hc + b_ref[g]
        hs_ref[g] = hs
        return hs[SUBLANES - 1:SUBLANES, :]

    hc = lax.fori_loop(0, N_GROUPS, step, hcar_ref[...])
    hcar_ref[...] = hc
    hst_ref[0] = hc

    y = (jax.nn.gelu(gate) * hs_ref[...].reshape(TM, D_RNN)).astype(BF16)
    o_ref[...] = x + m[2:3] * _dot(y, wout_ref[...])


def _rglru(x2d, mod, w_in, conv_w, conv_b, w_a, b_a, w_x, b_x, lam, w_out):
    n_tok = x2d.shape[0]
    return pl.pallas_call(
        _rglru_kernel,
        grid=(n_tok // TM,),
        in_specs=[
            pl.BlockSpec((TM, D_MODEL), lambda i: (i, 0)),
            pl.BlockSpec((1, 6, D_MODEL), lambda i: (i // TILES_PER_SEQ, 0, 0)),
            _const_spec((D_MODEL, 2 * D_RNN)),
            _const_spec((4, D_RNN)),
            _const_spec((1, D_RNN)),
            _const_spec((RG_HEADS, RG_BLOCK, RG_BLOCK)),
            _const_spec((1, D_RNN)),
            _const_spec((RG_HEADS, RG_BLOCK, RG_BLOCK)),
            _const_spec((1, D_RNN)),
            _const_spec((1, D_RNN)),
            _const_spec((D_RNN, D_MODEL)),
        ],
        out_specs=[
            pl.BlockSpec((TM, D_MODEL), lambda i: (i, 0)),
            pl.BlockSpec((1, SUBLANES, D_RNN), lambda i: (i // TILES_PER_SEQ, 0, 0)),
            pl.BlockSpec((1, 1, D_RNN), lambda i: (i // TILES_PER_SEQ, 0, 0)),
        ],
        out_shape=[
            jax.ShapeDtypeStruct((n_tok, D_MODEL), F32),
            jax.ShapeDtypeStruct((BATCH, SUBLANES, D_RNN), F32),
            jax.ShapeDtypeStruct((BATCH, 1, D_RNN), F32),
        ],
        scratch_shapes=[
            pltpu.VMEM((TM + SUBLANES, D_RNN), F32),
            pltpu.VMEM((1, D_RNN), F32),
            pltpu.VMEM((N_GROUPS, SUBLANES, D_RNN), F32),
            pltpu.VMEM((N_GROUPS, SUBLANES, D_RNN), F32),
            pltpu.VMEM((N_GROUPS, SUBLANES, D_RNN), F32),
        ],
        compiler_params=_cparams("arbitrary"),
        name="rglru",
    )(x2d, mod, w_in, conv_w, conv_b, w_a, b_a, w_x, b_x, lam, w_out)


def _s_even_in_kernel(x_ref, mod_ref, w_ref, lng_ref, lnb_ref, ws0_ref, bs0_ref,
                      va_ref, a_ref, q_ref, k_ref, v_ref):
    mod = mod_ref[...]
    h = _rms_mod(x_ref[...], mod[:, D_MODEL:2 * D_MODEL], mod[:, 0:D_MODEL]).astype(BF16)
    u = jax.nn.gelu(_dot(h, w_ref[:, 0:W_A]))
    va = _layernorm(jax.nn.gelu(_dot(h, w_ref[:, W_A:2 * W_A])), lng_ref[...], lnb_ref[...])
    va_ref[...] = va
    a_ref[...] = (u * (ws0_ref[...] * va + bs0_ref[...])).astype(BF16)
    base = 2 * W_A
    q_ref[...] = _dot(h, w_ref[:, base:base + W_B])
    k_ref[...] = _dot(h, w_ref[:, base + W_B:base + 2 * W_B])
    v_ref[...] = _dot(h, w_ref[:, base + 2 * W_B:base + 3 * W_B])


def _s_even_in(x, mod, w_in, ln_g, ln_b, ws0, bs0):
    n = x.shape[0]
    f = jax.ShapeDtypeStruct((n, W_B), F32)
    return pl.pallas_call(
        _s_even_in_kernel,
        out_shape=[f, jax.ShapeDtypeStruct((n, W_A), BF16), f, f, f],
        compiler_params=pltpu.CompilerParams(vmem_limit_bytes=VMEM_LIMIT),
        name="s_even_in",
    )(x, mod, w_in, ln_g, ln_b, ws0, bs0)


def _s_attn_kernel(q_ref, kn_ref, vn_ref, kt_ref, vt_ref, o_ref, kto_ref, vto_ref):
    n_hd = N_HEADS * HD
    qrow = q_ref[0] * (HD ** -0.5)
    knrow = kn_ref[0]
    vnrow = vn_ref[0]
    hrow = lax.broadcasted_iota(jnp.int32, (N_HEADS, n_hd), 0)
    hcol = lax.broadcasted_iota(jnp.int32, (N_HEADS, n_hd), 1) // HD
    own = hrow == hcol
    qbd = jnp.where(own, jnp.broadcast_to(qrow, (N_HEADS, n_hd)), 0.0)
    kt = kt_ref[0].reshape(n_hd, WIN)
    vt = vt_ref[0].reshape(n_hd, WIN)
    s = _dot(qbd.astype(BF16), kt.astype(BF16))
    s_new = jnp.sum(qbd * knrow, axis=-1, keepdims=True)
    t = lax.broadcasted_iota(jnp.int32, (N_HEADS, WIN), 1)
    dist = WIN - t
    vtb = vt.astype(BF16)
    num = jnp.zeros((N_HEADS, n_hd), F32)
    outs, lses = [], []
    for d in DILATIONS:
        valid = ((dist & (d - 1)) == 0) & (dist <= N_BACK * d)
        sm = jnp.where(valid, s, NEG_INF)
        mx = jnp.maximum(jnp.max(sm, axis=-1, keepdims=True), s_new)
        e = jnp.where(valid, jnp.exp(sm - mx), 0.0)
        e_new = jnp.exp(s_new - mx)
        den = jnp.sum(e, axis=-1, keepdims=True) + e_new
        o = lax.dot_general(e.astype(BF16), vtb, (((1,), (1,)), ((), ())),
                            preferred_element_type=F32)
        outs.append((o + e_new * vnrow) / den)
        lses.append(mx + jnp.log(den))
    mx = jnp.maximum(jnp.maximum(lses[0], lses[1]), lses[2])
    ws = [jnp.exp(l - mx) for l in lses]
    num = ws[0] * outs[0] + ws[1] * outs[1] + ws[2] * outs[2]
    mixed = num / (ws[0] + ws[1] + ws[2])
    o_ref[0] = jnp.sum(jnp.where(own, mixed, 0.0), axis=0, keepdims=True)

    r2 = lax.broadcasted_iota(jnp.int32, (n_hd, n_hd), 0)
    c2 = lax.broadcasted_iota(jnp.int32, (n_hd, n_hd), 1)
    diag = r2 == c2
    kcol = jnp.sum(jnp.where(diag, jnp.broadcast_to(knrow, (n_hd, n_hd)), 0.0), axis=-1, keepdims=True)
    vcol = jnp.sum(jnp.where(diag, jnp.broadcast_to(vnrow, (n_hd, n_hd)), 0.0), axis=-1, keepdims=True)
    tt = lax.broadcasted_iota(jnp.int32, (n_hd, WIN), 1)
    is_last = tt == WIN - 1
    kto_ref[0] = jnp.where(is_last, kcol, pltpu.roll(kt, WIN - 1, 1)).reshape(N_HEADS, HD, WIN)
    vto_ref[0] = jnp.where(is_last, vcol, pltpu.roll(vt, WIN - 1, 1)).reshape(N_HEADS, HD, WIN)


def _s_attn(q, k_new, v_new, kt, vt):
    n = q.shape[0]
    n_hd = N_HEADS * HD
    row_spec = pl.BlockSpec((1, 1, n_hd), lambda i: (i, 0, 0))
    cache_spec = pl.BlockSpec((1, N_HEADS, HD, WIN), lambda i: (i, 0, 0, 0))
    cache_shape = jax.ShapeDtypeStruct((n, N_HEADS, HD, WIN), F32)
    return pl.pallas_call(
        _s_attn_kernel,
        grid=(n,),
        in_specs=[row_spec, row_spec, row_spec, cache_spec, cache_spec],
        out_specs=[row_spec, cache_spec, cache_spec],
        out_shape=[jax.ShapeDtypeStruct((n, 1, n_hd), F32), cache_shape, cache_shape],
        compiler_params=_cparams("arbitrary"),
        name="s_attn",
    )(q.reshape(n, 1, n_hd), k_new.reshape(n, 1, n_hd), v_new.reshape(n, 1, n_hd), kt, vt)


def _s_even_out_kernel(x_ref, mod_ref, a_ref, b_ref, w_ref, o_ref):
    g1 = mod_ref[:, 2 * D_MODEL:3 * D_MODEL]
    mo = _dot(a_ref[...], w_ref[0:W_A, :]) + _dot(b_ref[...].astype(BF16), w_ref[W_A:W_A + W_B, :])
    o_ref[...] = x_ref[...] + g1 * mo


def _s_even_out(x, mod, a_out, b_out, w_out):
    return pl.pallas_call(
        _s_even_out_kernel,
        out_shape=jax.ShapeDtypeStruct(x.shape, F32),
        compiler_params=pltpu.CompilerParams(vmem_limit_bytes=VMEM_LIMIT),
        name="s_even_out",
    )(x, mod, a_out, b_out, w_out)


def _s_ffn_kernel(x_ref, mod_ref, p2_ref, p1_ref, wup_ref, cw_ref, cb_ref, wdn_ref, fg_ref,
                  o_ref, up_ref, *, final):
    x = x_ref[...]
    mod = mod_ref[...]
    h = _rms_mod(x, mod[:, 4 * D_MODEL:5 * D_MODEL], mod[:, 3 * D_MODEL:4 * D_MODEL]).astype(BF16)
    up = _dot(h, wup_ref[...])
    up_ref[...] = up
    cw = cw_ref[...]
    conv = cb_ref[...] + cw[0:1] * p2_ref[...] + cw[1:2] * p1_ref[...] + cw[2:3] * up
    ca, cg = conv[:, 0:D_FF], conv[:, D_FF:2 * D_FF]
    act = (ca * jax.nn.sigmoid(ca) * cg).astype(BF16)
    y = x + mod[:, 5 * D_MODEL:6 * D_MODEL] * _dot(act, wdn_ref[...])
    if final:
        y = _rmsnorm(y) * fg_ref[...]
    o_ref[...] = y


def _s_ffn(x, mod, p2, p1, w_up, conv_w, conv_b, w_down, final_g, final):
    n = x.shape[0]
    return pl.pallas_call(
        functools.partial(_s_ffn_kernel, final=final),
        out_shape=[jax.ShapeDtypeStruct(x.shape, F32), jax.ShapeDtypeStruct((n, 2 * D_FF), F32)],
        compiler_params=pltpu.CompilerParams(vmem_limit_bytes=VMEM_LIMIT),
        name="s_conv_ffn_final" if final else "s_conv_ffn",
    )(x, mod, p2, p1, w_up, conv_w, conv_b, w_down, final_g)


def _s_rglru_kernel(x_ref, mod_ref, c0_ref, c1_ref, c2_ref, h0_ref, win_ref, cw_ref, cb_ref,
                    wa_ref, ba_ref, wx_ref, bx_ref, lam_ref, wout_ref, o_ref, xr_ref, hn_ref):
    x = x_ref[...]
    mod = mod_ref[...]
    h = _rms_mod(x, mod[:, D_MODEL:2 * D_MODEL], mod[:, 0:D_MODEL]).astype(BF16)
    gate = _dot(h, win_ref[:, 0:D_RNN])
    xr = _dot(h, win_ref[:, D_RNN:2 * D_RNN])
    xr_ref[...] = xr
    cw = cw_ref[...]
    xc = (cb_ref[...] + cw[0:1] * c0_ref[...] + cw[1:2] * c1_ref[...] + cw[2:3] * c2_ref[...]
          + cw[3:4] * xr)
    a, b = _rg_gates(xc, wa_ref, ba_ref[...], wx_ref, bx_ref[...], lam_ref[...])
    hn = a * h0_ref[...] + b
    hn_ref[...] = hn
    y = (jax.nn.gelu(gate) * hn).astype(BF16)
    o_ref[...] = x + mod[:, 2 * D_MODEL:3 * D_MODEL] * _dot(y, wout_ref[...])


def _s_rglru(x, mod, c0, c1, c2, h0, w_in, conv_w, conv_b, w_a, b_a, w_x, b_x, lam, w_out):
    f = jax.ShapeDtypeStruct(x.shape, F32)
    return pl.pallas_call(
        _s_rglru_kernel,
        out_shape=[f, f, f],
        compiler_params=pltpu.CompilerParams(vmem_limit_bytes=VMEM_LIMIT),
        name="s_rglru",
    )(x, mod, c0, c1, c2, h0, w_in, conv_w, conv_b, w_a, b_a, w_x, b_x, lam, w_out)


def kernel(x_prompt, x_sample, cache_win_k, cache_win_v, state_rglru_conv, state_rglru_h, state_ffn_conv, c_prompt, c_sample, w_ada, b_ada, w_in_even, ln_v_g, ln_v_b, w_sgu, b_sgu, w_out_even, w_in_odd, rg_conv_w, rg_conv_b, rg_w_a, rg_b_a, rg_w_x, rg_b_x, rg_lambda, w_out_odd, ffn_w_up, ffn_conv_w, ffn_conv_b, ffn_w_down, final_g):
    w_in_even_b = w_in_even[0].astype(BF16)
    w_out_even_b = w_out_even[0].astype(BF16)
    w_in_odd_b = w_in_odd[0].astype(BF16)
    w_out_odd_b = w_out_odd[0].astype(BF16)
    rg_w_a_b = rg_w_a[0].astype(BF16)
    rg_w_x_b = rg_w_x[0].astype(BF16)
    w_up_b = ffn_w_up.astype(BF16)
    w_down_b = ffn_w_down.astype(BF16)
    final_g2 = final_g.reshape(1, D_MODEL)

    pad = jnp.zeros((ADA_ROWS - BATCH - DEC_BATCH, D_MODEL), F32)
    mod = _ada(jnp.concatenate([c_prompt, c_sample, pad], axis=0), w_ada, b_ada)
    mod_p = mod[:, :BATCH].reshape(2, BATCH, 6, D_MODEL)
    mod_s = mod[:, BATCH:BATCH + DEC_BATCH]

    x = x_prompt.reshape(BATCH * SEQ, D_MODEL)
    a_out, q, k, v, kt_p, vt_p = _even_in(x, mod_p[0], w_in_even_b, ln_v_g, ln_v_b, w_sgu[0],
                                          b_sgu[0].T)
    b_out = _attn(q, k, v)
    x = _even_out(x, mod_p[0], a_out, b_out, w_out_even_b)
    x, ffn_st0 = _ffn(x, mod_p[0], w_up_b[0], ffn_conv_w[0], ffn_conv_b[0:1], w_down_b[0],
                      final_g2, False)
    x, rg_cst, rg_hst = _rglru(x, mod_p[1], w_in_odd_b, rg_conv_w[0], rg_conv_b, rg_w_a_b, rg_b_a,
                               rg_w_x_b, rg_b_x, rg_lambda, w_out_odd_b)
    y_p, ffn_st1 = _ffn(x, mod_p[1], w_up_b[1], ffn_conv_w[1], ffn_conv_b[1:2], w_down_b[1],
                        final_g2, True)

    y_prompt = y_p.reshape(BATCH, SEQ, D_MODEL)
    to_win = lambda t: t.reshape(1, BATCH, N_HEADS, HD, WIN).transpose(0, 1, 4, 2, 3)
    win_k_prompt = to_win(kt_p)
    win_v_prompt = to_win(vt_p)
    rglru_conv_prompt = rg_cst[None, :, SUBLANES - 3:, :]
    rglru_h_prompt = rg_hst.reshape(1, BATCH, D_RNN)
    ffn_conv_prompt = jnp.stack([ffn_st0[:, SUBLANES - 2:, :], ffn_st1[:, SUBLANES - 2:, :]])

    xs = x_sample.reshape(DEC_BATCH, D_MODEL)
    ws0 = jnp.repeat(w_sgu[0, :, 0, 0], A_GROUP).reshape(1, W_A)
    bs0 = jnp.repeat(b_sgu[0, :, 0], A_GROUP).reshape(1, W_A)
    va_s, a_s, q_s, k_s, v_s = _s_even_in(xs, mod_s[0], w_in_even_b, ln_v_g, ln_v_b, ws0, bs0)
    kt_c = cache_win_k[0].transpose(0, 2, 3, 1)
    vt_c = cache_win_v[0].transpose(0, 2, 3, 1)
    b_s, kt_n, vt_n = _s_attn(q_s, k_s, v_s, kt_c, vt_c)
    xs = _s_even_out(xs, mod_s[0], a_s, b_s.reshape(DEC_BATCH, W_B), w_out_even_b)
    st0 = state_ffn_conv[0]
    xs, up0 = _s_ffn(xs, mod_s[0], st0[:, 0], st0[:, 1], w_up_b[0], ffn_conv_w[0], ffn_conv_b[0:1],
                     w_down_b[0], final_g2, False)
    cst = state_rglru_conv[0]
    xs, xr_s, hn_s = _s_rglru(xs, mod_s[1], cst[:, 0], cst[:, 1], cst[:, 2], state_rglru_h[0],
                              w_in_odd_b, rg_conv_w[0], rg_conv_b, rg_w_a_b, rg_b_a, rg_w_x_b,
                              rg_b_x, rg_lambda, w_out_odd_b)
    st1 = state_ffn_conv[1]
    ys, up1 = _s_ffn(xs, mod_s[1], st1[:, 0], st1[:, 1], w_up_b[1], ffn_conv_w[1], ffn_conv_b[1:2],
                     w_down_b[1], final_g2, True)

    y_sample = ys.reshape(DEC_BATCH, 1, D_MODEL)
    chunk_v_sample = va_s.reshape(1, DEC_BATCH, 1, W_A)
    win_k_sample = kt_n.transpose(0, 3, 1, 2)[None]
    win_v_sample = vt_n.transpose(0, 3, 1, 2)[None]
    rglru_conv_sample = jnp.stack([cst[:, 1], cst[:, 2], xr_s], axis=1)[None]
    rglru_h_sample = hn_s[None]
    ffn_conv_sample = jnp.stack([jnp.stack([st0[:, 1], up0], axis=1),
                                 jnp.stack([st1[:, 1], up1], axis=1)])

    return (y_prompt, y_sample, win_k_prompt, win_v_prompt, rglru_conv_prompt, rglru_h_prompt,
            ffn_conv_prompt, chunk_v_sample, win_k_sample, win_v_sample, rglru_conv_sample,
            rglru_h_sample, ffn_conv_sample)
```

```python
import functools

import jax
import jax.numpy as jnp
from jax import lax
from jax.experimental import pallas as pl
from jax.experimental.pallas import tpu as pltpu

F32 = jnp.float32
BF16 = jnp.bfloat16

D_MODEL = 1024
BATCH = 4
SEQ = 4096
DEC_BATCH = 32
W_A = 512
A_GROUP = 128
G_A = 4
CHUNK = 128
W_B = 512
HD = 64
N_HEADS = 8
DILATIONS = (1, 4, 16)
N_BACK = 128
WIN = 2048
N_IN_EVEN = 2 * W_A + 3 * W_B
D_RNN = 1024
RG_BLOCK = 128
RG_HEADS = 8
RG_C = 8.0
D_FF = 2816
EPS = 1e-6
NEG_INF = -1e30
LOG2_E = 1.4426950408889634
LN_2 = 0.6931471805599453

LANES = 128
SUBLANES = 8
TM = 512
TILES_PER_SEQ = SEQ // TM
N_TILES = BATCH * TILES_PER_SEQ
FF_CHUNK = 256
N_FF_CHUNKS = D_FF // FF_CHUNK
FFN_SUB = 1
TM_FFN = FFN_SUB * TM
FFN_TILES_PER_SEQ = SEQ // TM_FFN
FFN_VMEM_LIMIT = 56 * 1024 * 1024
N_SLABS = W_B // LANES
VMEM_LIMIT = 56 * 1024 * 1024


def _cparams(*sem):
    return pltpu.CompilerParams(dimension_semantics=sem, vmem_limit_bytes=VMEM_LIMIT)


def _const_spec(shape):
    nd = len(shape)
    return pl.BlockSpec(shape, lambda *_: (0,) * nd, pipeline_mode=pl.Buffered(1))


def _rms_mod(x, scale, shift):
    xn = x * lax.rsqrt(jnp.mean(x * x, axis=-1, keepdims=True) + EPS)
    return xn * (1.0 + scale) + shift


def _rmsnorm(x):
    return x * lax.rsqrt(jnp.mean(x * x, axis=-1, keepdims=True) + EPS)


def _layernorm(x, g, b):
    mu = jnp.mean(x, axis=-1, keepdims=True)
    xc = x - mu
    var = jnp.mean(xc * xc, axis=-1, keepdims=True)
    return xc * lax.rsqrt(var + EPS) * g + b


def _dot(a, b):
    return jnp.dot(a, b, preferred_element_type=F32)


ADA_ROWS = 40
ADA_TN = 1024


def _ada_kernel(c_ref, w_ref, b_ref, o_ref):
    c = c_ref[...]
    s = (c * jax.nn.sigmoid(c)).astype(BF16)
    o_ref[0] = _dot(s, w_ref[0].astype(BF16)) + b_ref[0]


def _ada(c_all, w_ada, b_ada):
    depth = w_ada.shape[0]
    n_out = w_ada.shape[2]
    return pl.pallas_call(
        _ada_kernel,
        grid=(depth, n_out // ADA_TN),
        in_specs=[
            pl.BlockSpec((ADA_ROWS, D_MODEL), lambda l, j: (0, 0)),
            pl.BlockSpec((1, D_MODEL, ADA_TN), lambda l, j: (l, 0, j)),
            pl.BlockSpec((1, 1, ADA_TN), lambda l, j: (l, 0, j)),
        ],
        out_specs=pl.BlockSpec((1, ADA_ROWS, ADA_TN), lambda l, j: (l, 0, j)),
        out_shape=jax.ShapeDtypeStruct((depth, ADA_ROWS, n_out), F32),
        compiler_params=_cparams("arbitrary", "arbitrary"),
        name="ada_mod",
    )(c_all, w_ada, b_ada.reshape(depth, 1, n_out))


def _even_in_kernel(x_ref, mod_ref, w_ref, lng_ref, lnb_ref, ws_ref, bst_ref,
                    a_ref, q_ref, k_ref, v_ref, kt_ref, vt_ref):
    i = pl.program_id(0)
    m = mod_ref[0]
    h = _rms_mod(x_ref[...], m[1:2], m[0:1]).astype(BF16)
    u = jax.nn.gelu(_dot(h, w_ref[:, 0:W_A]))
    va = _layernorm(jax.nn.gelu(_dot(h, w_ref[:, W_A:2 * W_A])), lng_ref[...], lnb_ref[...])
    vab = va.astype(BF16)
    row = lax.broadcasted_iota(jnp.int32, (CHUNK, CHUNK), 0)
    col = lax.broadcasted_iota(jnp.int32, (CHUNK, CHUNK), 1)
    causal = col <= row
    bst = bst_ref[...]
    for g in range(G_A):
        wg = jnp.where(causal, ws_ref[g], 0.0).astype(BF16)
        bias = bst[:, g:g + 1]
        lo, hi = g * A_GROUP, (g + 1) * A_GROUP
        for c in range(TM // CHUNK):
            r0, r1 = c * CHUNK, (c + 1) * CHUNK
            mix = _dot(wg, vab[r0:r1, lo:hi]) + bias
            a_ref[r0:r1, lo:hi] = (u[r0:r1, lo:hi] * mix).astype(BF16)
    base = 2 * W_A
    q = _dot(h, w_ref[:, base:base + W_B])
    k = _dot(h, w_ref[:, base + W_B:base + 2 * W_B])
    v = _dot(h, w_ref[:, base + 2 * W_B:base + 3 * W_B])
    for s in range(N_SLABS):
        q_ref[s] = q[:, s * LANES:(s + 1) * LANES]
        k_ref[s] = k[:, s * LANES:(s + 1) * LANES]
        v_ref[s] = v[:, s * LANES:(s + 1) * LANES]

    @pl.when(i % TILES_PER_SEQ >= TILES_PER_SEQ - WIN // TM)
    def _():
        kt_ref[0] = k.T
        vt_ref[0] = v.T


def _even_in(x2d, mod, w_in, ln_g, ln_b, w_s, b_s_t):
    n_tok = x2d.shape[0]
    first_win_tile = TILES_PER_SEQ - WIN // TM
    qkv_shape = jax.ShapeDtypeStruct((N_SLABS, n_tok, LANES), F32)
    qkv_spec = pl.BlockSpec((N_SLABS, TM, LANES), lambda i: (0, i, 0))
    win_shape = jax.ShapeDtypeStruct((BATCH, W_B, WIN), F32)
    win_spec = pl.BlockSpec(
        (1, W_B, TM),
        lambda i: (i // TILES_PER_SEQ, 0, jnp.maximum(i % TILES_PER_SEQ - first_win_tile, 0)))
    return pl.pallas_call(
        _even_in_kernel,
        grid=(N_TILES,),
        in_specs=[
            pl.BlockSpec((TM, D_MODEL), lambda i: (i, 0)),
            pl.BlockSpec((1, 6, D_MODEL), lambda i: (i // TILES_PER_SEQ, 0, 0)),
            _const_spec((D_MODEL, N_IN_EVEN)),
            _const_spec((1, W_A)),
            _const_spec((1, W_A)),
            _const_spec((G_A, CHUNK, CHUNK)),
            _const_spec((CHUNK, G_A)),
        ],
        out_specs=[
            pl.BlockSpec((TM, W_A), lambda i: (i, 0)),
            qkv_spec, qkv_spec, qkv_spec, win_spec, win_spec,
        ],
        out_shape=[
            jax.ShapeDtypeStruct((n_tok, W_A), BF16),
            qkv_shape, qkv_shape, qkv_shape, win_shape, win_shape,
        ],
        compiler_params=_cparams("arbitrary"),
        name="even_in",
    )(x2d, mod, w_in, ln_g, ln_b, w_s, b_s_t)


ATT_BLK = 128
ATT_UNROLL = 8


def _attn_kernel(q_ref, k_ref, v_ref, o_ref, acc_ref, max_ref, den_ref):
    q2, k2, v2 = q_ref.at[0], k_ref.at[0], v_ref.at[0]
    qi = lax.broadcasted_iota(jnp.int32, (ATT_BLK, 2 * ATT_BLK), 0)
    kj = lax.broadcasted_iota(jnp.int32, (ATT_BLK, 2 * ATT_BLK), 1)
    dist = qi + ATT_BLK - kj
    band = (dist >= 0) & (dist <= N_BACK)
    bias_full = jnp.where(band, 0.0, NEG_INF).astype(F32)
    bias_first = jnp.where(band & (kj >= ATT_BLK), 0.0, NEG_INF).astype(F32)
    lane = lax.broadcasted_iota(jnp.int32, (ATT_BLK, LANES), 1)
    head0 = lane < HD

    for p, d in enumerate(DILATIONS):
        n_blk = SEQ // (d * ATT_BLK)

        def unit(idx, carry, p=p, d=d, n_blk=n_blk):
            k_prev, v_prev = carry
            c = idx // n_blk
            b = idx % n_blk
            own = c + d * ATT_BLK * b
            if d == 1:
                rows = pl.ds(pl.multiple_of(own, ATT_BLK), ATT_BLK)
            else:
                rows = pl.ds(own, ATT_BLK, stride=d)
            qb = q2[rows, :] * (HD ** -0.5 * LOG2_E)
            k_own = k2[rows, :].astype(BF16)
            v_own = v2[rows, :].astype(BF16)
            kb = jnp.concatenate([k_prev, k_own], axis=0)
            vb = jnp.concatenate([v_prev, v_own], axis=0)
            bias = jnp.where(b == 0, bias_first, bias_full)
            outs, mxs, dens = [], [], []
            for hh in range(2):
                sel = head0 if hh == 0 else jnp.logical_not(head0)
                qm = jnp.where(sel, qb, 0.0).astype(BF16)
                s = lax.dot_general(qm, kb, (((1,), (1,)), ((), ())),
                                    preferred_element_type=F32) + bias
                mx = jnp.max(s, axis=-1, keepdims=True)
                e = jnp.exp2(s - mx)
                dens.append(jnp.sum(e, axis=-1, keepdims=True))
                mxs.append(mx)
                outs.append(_dot(e.astype(BF16), vb))
            acc_ref[p, rows, :] = jnp.where(head0, outs[0], outs[1])
            max_ref[p, rows, :] = jnp.where(head0, mxs[0], mxs[1])
            den_ref[p, rows, :] = jnp.where(head0, dens[0], dens[1])
            return k_own, v_own

        zeros = jnp.zeros((ATT_BLK, LANES), BF16)
        lax.fori_loop(0, SEQ // ATT_BLK, unit, (zeros, zeros), unroll=ATT_UNROLL)

    def mix(t, carry):
        rows = pl.ds(pl.multiple_of(t * TM, TM), TM)
        m0, m1, m2 = max_ref[0, rows, :], max_ref[1, rows, :], max_ref[2, rows, :]
        mx = jnp.maximum(jnp.maximum(m0, m1), m2)
        e0, e1, e2 = jnp.exp2(m0 - mx), jnp.exp2(m1 - mx), jnp.exp2(m2 - mx)
        num = e0 * acc_ref[0, rows, :] + e1 * acc_ref[1, rows, :] + e2 * acc_ref[2, rows, :]
        den = e0 * den_ref[0, rows, :] + e1 * den_ref[1, rows, :] + e2 * den_ref[2, rows, :]
        o_ref[rows, :] = (num / den).astype(BF16)
        return carry

    lax.fori_loop(0, SEQ // TM, mix, 0)


def _attn(q, k, v):
    n_tok = q.shape[1]
    spec = pl.BlockSpec((1, SEQ, LANES), lambda n, s: (s, n, 0))
    return pl.pallas_call(
        _attn_kernel,
        grid=(BATCH, N_SLABS),
        in_specs=[spec, spec, spec],
        out_specs=pl.BlockSpec((SEQ, LANES), lambda n, s: (n, s)),
        out_shape=jax.ShapeDtypeStruct((n_tok, W_B), BF16),
        scratch_shapes=[pltpu.VMEM((len(DILATIONS), SEQ, LANES), F32)] * 3,
        compiler_params=_cparams("arbitrary", "arbitrary"),
        name="dil_attn",
    )(q, k, v)


def _ffn_kernel(*refs, pre, final):
    if pre:
        (x_ref, mod_ref, a_ref, b_ref, wo_ref, wup_ref, cw_ref, cb_ref, wdn_ref, fg_ref,
         o_ref, st_ref, carry_ref, buf_ref, act_ref) = refs
    else:
        (x_ref, mod_ref, wup_ref, cw_ref, cb_ref, wdn_ref, fg_ref,
         o_ref, st_ref, carry_ref, buf_ref, act_ref) = refs
    i = pl.program_id(0)

    @pl.when(i % FFN_TILES_PER_SEQ == 0)
    def _():
        carry_ref[...] = jnp.zeros_like(carry_ref)

    m = mod_ref[0]
    cw = cw_ref[...]
    cb = cb_ref[...]

    for t in range(FFN_SUB):
        rows = slice(t * TM, (t + 1) * TM)
        x = x_ref[rows, :]
        if pre:
            mo = (_dot(a_ref[rows, :], wo_ref[0:W_A, :])
                  + _dot(b_ref[rows, :], wo_ref[W_A:W_A + W_B, :]))
            x = x + m[2:3] * mo
        o_ref[rows, :] = x
        h = _rms_mod(x, m[4:5], m[3:4]).astype(BF16)
        for j in range(N_FF_CHUNKS):
            conv = []
            for half in range(2):
                c0 = half * D_FF + j * FF_CHUNK
                cols = slice(c0, c0 + FF_CHUNK)
                up = _dot(h, wup_ref[:, cols])
                buf_ref[half, 0:SUBLANES, :] = carry_ref[:, cols]
                buf_ref[half, SUBLANES:, :] = up
                carry_ref[:, cols] = up[TM - SUBLANES:, :]
                prev1 = buf_ref[half, SUBLANES - 1:SUBLANES - 1 + TM, :]
                prev2 = buf_ref[half, SUBLANES - 2:SUBLANES - 2 + TM, :]
                conv.append(cb[:, cols] + cw[0:1, cols] * prev2 + cw[1:2, cols] * prev1
                            + cw[2:3, cols] * up)
            act = conv[0] * jax.nn.sigmoid(conv[0]) * conv[1]
            act_ref[t, :, j * FF_CHUNK:(j + 1) * FF_CHUNK] = act.astype(BF16)
    st_ref[0] = carry_ref[...]
    for t in range(FFN_SUB):
        rows = slice(t * TM, (t + 1) * TM)
        y = o_ref[rows, :] + m[5:6] * _dot(act_ref[t], wdn_ref[...])
        if final:
            y = _rmsnorm(y) * fg_ref[...]
        o_ref[rows, :] = y


def _ffn(x2d, mod, w_up, conv_w, conv_b, w_down, final_g, final, pre=None):
    n_tok = x2d.shape[0]
    f2 = 2 * D_FF
    row_spec = pl.BlockSpec((TM_FFN, D_MODEL), lambda i: (i, 0))
    mod_spec = pl.BlockSpec((1, 6, D_MODEL), lambda i: (i // FFN_TILES_PER_SEQ, 0, 0))
    args, specs = [x2d, mod], [row_spec, mod_spec]
    if pre is not None:
        args += list(pre)
        specs += [pl.BlockSpec((TM_FFN, W_A), lambda i: (i, 0)),
                  pl.BlockSpec((TM_FFN, W_B), lambda i: (i, 0)),
                  _const_spec((W_A + W_B, D_MODEL))]
    args += [w_up, conv_w, conv_b, w_down, final_g]
    specs += [_const_spec((D_MODEL, f2)), _const_spec((3, f2)), _const_spec((1, f2)),
              _const_spec((D_FF, D_MODEL)), _const_spec((1, D_MODEL))]
    return pl.pallas_call(
        functools.partial(_ffn_kernel, pre=pre is not None, final=final),
        grid=(n_tok // TM_FFN,),
        in_specs=specs,
        out_specs=[
            row_spec,
            pl.BlockSpec((1, SUBLANES, f2), lambda i: (i // FFN_TILES_PER_SEQ, 0, 0)),
        ],
        out_shape=[
            jax.ShapeDtypeStruct((n_tok, D_MODEL), F32),
            jax.ShapeDtypeStruct((BATCH, SUBLANES, f2), F32),
        ],
        scratch_shapes=[
            pltpu.VMEM((SUBLANES, f2), F32),
            pltpu.VMEM((2, TM + SUBLANES, FF_CHUNK), F32),
            pltpu.VMEM((FFN_SUB, TM, D_FF), BF16),
        ],
        compiler_params=pltpu.CompilerParams(dimension_semantics=("arbitrary",),
                                             vmem_limit_bytes=FFN_VMEM_LIMIT),
        name="conv_ffn_final" if final else "conv_ffn",
    )(*args)


N_GROUPS = TM // SUBLANES


def _rg_gates(xc, wa_ref, ba, wx_ref, bx, lam):
    xcb = xc.astype(BF16)
    r_parts, i_parts = [], []
    for g in range(RG_HEADS):
        blk = xcb[:, g * RG_BLOCK:(g + 1) * RG_BLOCK]
        r_parts.append(_dot(blk, wa_ref[g]))
        i_parts.append(_dot(blk, wx_ref[g]))
    r = jax.nn.sigmoid(jnp.concatenate(r_parts, axis=-1) + ba)
    ig = jax.nn.sigmoid(jnp.concatenate(i_parts, axis=-1) + bx)
    log_a = (-RG_C) * r * jax.nn.softplus(-lam)
    a = jnp.exp(log_a)
    b = jnp.sqrt(1.0 - a * a) * (ig * xc)
    return a, b


def _rglru_kernel(x_ref, mod_ref, win_ref, cw_ref, cb_ref, wa_ref, ba_ref, wx_ref, bx_ref,
                  lam_ref, wout_ref, o_ref, cst_ref, hst_ref,
                  xbuf_ref, hcar_ref, a_ref, b_ref, hs_ref):
    i = pl.program_id(0)

    @pl.when(i % TILES_PER_SEQ == 0)
    def _():
        xbuf_ref[0:SUBLANES, :] = jnp.zeros((SUBLANES, D_RNN), F32)
        hcar_ref[...] = jnp.zeros_like(hcar_ref)

    x = x_ref[...]
    m = mod_ref[0]
    h = _rms_mod(x, m[1:2], m[0:1]).astype(BF16)
    gate = _dot(h, win_ref[:, 0:D_RNN])
    xr = _dot(h, win_ref[:, D_RNN:2 * D_RNN])
    xbuf_ref[SUBLANES:, :] = xr
    cw = cw_ref[...]
    xc = (cb_ref[...] + cw[3:4] * xr
          + cw[2:3] * xbuf_ref[SUBLANES - 1:SUBLANES - 1 + TM, :]
          + cw[1:2] * xbuf_ref[SUBLANES - 2:SUBLANES - 2 + TM, :]
          + cw[0:1] * xbuf_ref[SUBLANES - 3:SUBLANES - 3 + TM, :])
    last = xr[TM - SUBLANES:, :]
    xbuf_ref[0:SUBLANES, :] = last
    cst_ref[0] = last

    a, b = _rg_gates(xc, wa_ref, ba_ref[...], wx_ref, bx_ref[...], lam_ref[...])

    a3 = a.reshape(N_GROUPS, SUBLANES, D_RNN)
    b3 = b.reshape(N_GROUPS, SUBLANES, D_RNN)
    r_id = lax.broadcasted_iota(jnp.int32, (N_GROUPS, SUBLANES, D_RNN), 1)
    for s in (1, 2, 4):
        keep = r_id >= s
        a_sh = jnp.where(keep, pltpu.roll(a3, s, 1), 1.0)
        b_sh = jnp.where(keep, pltpu.roll(b3, s, 1), 0.0)
        b3 = a3 * b_sh + b3
        a3 = a3 * a_sh
    a_ref[...] = a3
    b_ref[...] = b3

    def step(g, hc):
        hs = a_ref[g] * hc + b_ref[g]
        hs_ref[g] = hs
        return hs[SUBLANES - 1:SUBLANES, :]

    hc = lax.fori_loop(0, N_GROUPS, step, hcar_ref[...])
    hcar_ref[...] = hc
    hst_ref[0] = hc

    y = (jax.nn.gelu(gate) * hs_ref[...].reshape(TM, D_RNN)).astype(BF16)
    o_ref[...] = x + m[2:3] * _dot(y, wout_ref[...])


def _rglru(x2d, mod, w_in, conv_w, conv_b, w_a, b_a, w_x, b_x, lam, w_out):
    n_tok = x2d.shape[0]
    return pl.pallas_call(
        _rglru_kernel,
        grid=(n_tok // TM,),
        in_specs=[
            pl.BlockSpec((TM, D_MODEL), lambda i: (i, 0)),
            pl.BlockSpec((1, 6, D_MODEL), lambda i: (i // TILES_PER_SEQ, 0, 0)),
            _const_spec((D_MODEL, 2 * D_RNN)),
            _const_spec((4, D_RNN)),
            _const_spec((1, D_RNN)),
            _const_spec((RG_HEADS, RG_BLOCK, RG_BLOCK)),
            _const_spec((1, D_RNN)),
            _const_spec((RG_HEADS, RG_BLOCK, RG_BLOCK)),
            _const_spec((1, D_RNN)),
            _const_spec((1, D_RNN)),
            _const_spec((D_RNN, D_MODEL)),
        ],
        out_specs=[
            pl.BlockSpec((TM, D_MODEL), lambda i: (i, 0)),
            pl.BlockSpec((1, SUBLANES, D_RNN), lambda i: (i // TILES_PER_SEQ, 0, 0)),
            pl.BlockSpec((1, 1, D_RNN), lambda i: (i // TILES_PER_SEQ, 0, 0)),
        ],
        out_shape=[
            jax.ShapeDtypeStruct((n_tok, D_MODEL), F32),
            jax.ShapeDtypeStruct((BATCH, SUBLANES, D_RNN), F32),
            jax.ShapeDtypeStruct((BATCH, 1, D_RNN), F32),
        ],
        scratch_shapes=[
            pltpu.VMEM((TM + SUBLANES, D_RNN), F32),
            pltpu.VMEM((1, D_RNN), F32),
            pltpu.VMEM((N_GROUPS, SUBLANES, D_RNN), F32),
            pltpu.VMEM((N_GROUPS, SUBLANES, D_RNN), F32),
            pltpu.VMEM((N_GROUPS, SUBLANES, D_RNN), F32),
        ],
        compiler_params=_cparams("arbitrary"),
        name="rglru",
    )(x2d, mod, w_in, conv_w, conv_b, w_a, b_a, w_x, b_x, lam, w_out)


def _s_even_in_kernel(x_ref, mod_ref, w_ref, lng_ref, lnb_ref, ws0_ref, bs0_ref,
                      va_ref, a_ref, q_ref, k_ref, v_ref):
    mod = mod_ref[...]
    h = _rms_mod(x_ref[...], mod[:, D_MODEL:2 * D_MODEL], mod[:, 0:D_MODEL]).astype(BF16)
    u = jax.nn.gelu(_dot(h, w_ref[:, 0:W_A]))
    va = _layernorm(jax.nn.gelu(_dot(h, w_ref[:, W_A:2 * W_A])), lng_ref[...], lnb_ref[...])
    va_ref[...] = va
    a_ref[...] = (u * (ws0_ref[...] * va + bs0_ref[...])).astype(BF16)
    base = 2 * W_A
    q_ref[...] = _dot(h, w_ref[:, base:base + W_B])
    k_ref[...] = _dot(h, w_ref[:, base + W_B:base + 2 * W_B])
    v_ref[...] = _dot(h, w_ref[:, base + 2 * W_B:base + 3 * W_B])


def _s_even_in(x, mod, w_in, ln_g, ln_b, ws0, bs0):
    n = x.shape[0]
    f = jax.ShapeDtypeStruct((n, W_B), F32)
    return pl.pallas_call(
        _s_even_in_kernel,
        out_shape=[f, jax.ShapeDtypeStruct((n, W_A), BF16), f, f, f],
        compiler_params=pltpu.CompilerParams(vmem_limit_bytes=VMEM_LIMIT),
        name="s_even_in",
    )(x, mod, w_in, ln_g, ln_b, ws0, bs0)


def _s_attn_kernel(q_ref, kn_ref, vn_ref, kt_ref, vt_ref, o_ref, kto_ref, vto_ref):
    n_hd = N_HEADS * HD
    qrow = q_ref[0] * (HD ** -0.5)
    knrow = kn_ref[0]
    vnrow = vn_ref[0]
    hrow = lax.broadcasted_iota(jnp.int32, (N_HEADS, n_hd), 0)
    hcol = lax.broadcasted_iota(jnp.int32, (N_HEADS, n_hd), 1) // HD
    own = hrow == hcol
    qbd = jnp.where(own, jnp.broadcast_to(qrow, (N_HEADS, n_hd)), 0.0)
    kt = kt_ref[0].reshape(n_hd, WIN)
    vt = vt_ref[0].reshape(n_hd, WIN)
    s = _dot(qbd.astype(BF16), kt.astype(BF16))
    s_new = jnp.sum(qbd * knrow, axis=-1, keepdims=True)
    t = lax.broadcasted_iota(jnp.int32, (N_HEADS, WIN), 1)
    dist = WIN - t
    vtb = vt.astype(BF16)
    num = jnp.zeros((N_HEADS, n_hd), F32)
    outs, lses = [], []
    for d in DILATIONS:
        valid = ((dist & (d - 1)) == 0) & (dist <= N_BACK * d)
        sm = jnp.where(valid, s, NEG_INF)
        mx = jnp.maximum(jnp.max(sm, axis=-1, keepdims=True), s_new)
        e = jnp.where(valid, jnp.exp(sm - mx), 0.0)
        e_new = jnp.exp(s_new - mx)
        den = jnp.sum(e, axis=-1, keepdims=True) + e_new
        o = lax.dot_general(e.astype(BF16), vtb, (((1,), (1,)), ((), ())),
                            preferred_element_type=F32)
        outs.append((o + e_new * vnrow) / den)
        lses.append(mx + jnp.log(den))
    mx = jnp.maximum(jnp.maximum(lses[0], lses[1]), lses[2])
    ws = [jnp.exp(l - mx) for l in lses]
    num = ws[0] * outs[0] + ws[1] * outs[1] + ws[2] * outs[2]
    mixed = num / (ws[0] + ws[1] + ws[2])
    o_ref[0] = jnp.sum(jnp.where(own, mixed, 0.0), axis=0, keepdims=True)

    r2 = lax.broadcasted_iota(jnp.int32, (n_hd, n_hd), 0)
    c2 = lax.broadcasted_iota(jnp.int32, (n_hd, n_hd), 1)
    diag = r2 == c2
    kcol = jnp.sum(jnp.where(diag, jnp.broadcast_to(knrow, (n_hd, n_hd)), 0.0), axis=-1, keepdims=True)
    vcol = jnp.sum(jnp.where(diag, jnp.broadcast_to(vnrow, (n_hd, n_hd)), 0.0), axis=-1, keepdims=True)
    tt = lax.broadcasted_iota(jnp.int32, (n_hd, WIN), 1)
    is_last = tt == WIN - 1
    kto_ref[0] = jnp.where(is_last, kcol, pltpu.roll(kt, WIN - 1, 1)).reshape(N_HEADS, HD, WIN)
    vto_ref[0] = jnp.where(is_last, vcol, pltpu.roll(vt, WIN - 1, 1)).reshape(N_HEADS, HD, WIN)


def _s_attn(q, k_new, v_new, kt, vt):
    n = q.shape[0]
    n_hd = N_HEADS * HD
    row_spec = pl.BlockSpec((1, 1, n_hd), lambda i: (i, 0, 0))
    cache_spec = pl.BlockSpec((1, N_HEADS, HD, WIN), lambda i: (i, 0, 0, 0))
    cache_shape = jax.ShapeDtypeStruct((n, N_HEADS, HD, WIN), F32)
    return pl.pallas_call(
        _s_attn_kernel,
        grid=(n,),
        in_specs=[row_spec, row_spec, row_spec, cache_spec, cache_spec],
        out_specs=[row_spec, cache_spec, cache_spec],
        out_shape=[jax.ShapeDtypeStruct((n, 1, n_hd), F32), cache_shape, cache_shape],
        compiler_params=_cparams("arbitrary"),
        name="s_attn",
    )(q.reshape(n, 1, n_hd), k_new.reshape(n, 1, n_hd), v_new.reshape(n, 1, n_hd), kt, vt)


def _s_even_out_kernel(x_ref, mod_ref, a_ref, b_ref, w_ref, o_ref):
    g1 = mod_ref[:, 2 * D_MODEL:3 * D_MODEL]
    mo = _dot(a_ref[...], w_ref[0:W_A, :]) + _dot(b_ref[...].astype(BF16), w_ref[W_A:W_A + W_B, :])
    o_ref[...] = x_ref[...] + g1 * mo


def _s_even_out(x, mod, a_out, b_out, w_out):
    return pl.pallas_call(
        _s_even_out_kernel,
        out_shape=jax.ShapeDtypeStruct(x.shape, F32),
        compiler_params=pltpu.CompilerParams(vmem_limit_bytes=VMEM_LIMIT),
        name="s_even_out",
    )(x, mod, a_out, b_out, w_out)


def _s_ffn_kernel(x_ref, mod_ref, p2_ref, p1_ref, wup_ref, cw_ref, cb_ref, wdn_ref, fg_ref,
                  o_ref, up_ref, *, final):
    x = x_ref[...]
    mod = mod_ref[...]
    h = _rms_mod(x, mod[:, 4 * D_MODEL:5 * D_MODEL], mod[:, 3 * D_MODEL:4 * D_MODEL]).astype(BF16)
    up = _dot(h, wup_ref[...])
    up_ref[...] = up
    cw = cw_ref[...]
    conv = cb_ref[...] + cw[0:1] * p2_ref[...] + cw[1:2] * p1_ref[...] + cw[2:3] * up
    ca, cg = conv[:, 0:D_FF], conv[:, D_FF:2 * D_FF]
    act = (ca * jax.nn.sigmoid(ca) * cg).astype(BF16)
    y = x + mod[:, 5 * D_MODEL:6 * D_MODEL] * _dot(act, wdn_ref[...])
    if final:
        y = _rmsnorm(y) * fg_ref[...]
    o_ref[...] = y


def _s_ffn(x, mod, p2, p1, w_up, conv_w, conv_b, w_down, final_g, final):
    n = x.shape[0]
    return pl.pallas_call(
        functools.partial(_s_ffn_kernel, final=final),
        out_shape=[jax.ShapeDtypeStruct(x.shape, F32), jax.ShapeDtypeStruct((n, 2 * D_FF), F32)],
        compiler_params=pltpu.CompilerParams(vmem_limit_bytes=VMEM_LIMIT),
        name="s_conv_ffn_final" if final else "s_conv_ffn",
    )(x, mod, p2, p1, w_up, conv_w, conv_b, w_down, final_g)


def _s_rglru_kernel(x_ref, mod_ref, c0_ref, c1_ref, c2_ref, h0_ref, win_ref, cw_ref, cb_ref,
                    wa_ref, ba_ref, wx_ref, bx_ref, lam_ref, wout_ref, o_ref, xr_ref, hn_ref):
    x = x_ref[...]
    mod = mod_ref[...]
    h = _rms_mod(x, mod[:, D_MODEL:2 * D_MODEL], mod[:, 0:D_MODEL]).astype(BF16)
    gate = _dot(h, win_ref[:, 0:D_RNN])
    xr = _dot(h, win_ref[:, D_RNN:2 * D_RNN])
    xr_ref[...] = xr
    cw = cw_ref[...]
    xc = (cb_ref[...] + cw[0:1] * c0_ref[...] + cw[1:2] * c1_ref[...] + cw[2:3] * c2_ref[...]
          + cw[3:4] * xr)
    a, b = _rg_gates(xc, wa_ref, ba_ref[...], wx_ref, bx_ref[...], lam_ref[...])
    hn = a * h0_ref[...] + b
    hn_ref[...] = hn
    y = (jax.nn.gelu(gate) * hn).astype(BF16)
    o_ref[...] = x + mod[:, 2 * D_MODEL:3 * D_MODEL] * _dot(y, wout_ref[...])


def _s_rglru(x, mod, c0, c1, c2, h0, w_in, conv_w, conv_b, w_a, b_a, w_x, b_x, lam, w_out):
    f = jax.ShapeDtypeStruct(x.shape, F32)
    return pl.pallas_call(
        _s_rglru_kernel,
        out_shape=[f, f, f],
        compiler_params=pltpu.CompilerParams(vmem_limit_bytes=VMEM_LIMIT),
        name="s_rglru",
    )(x, mod, c0, c1, c2, h0, w_in, conv_w, conv_b, w_a, b_a, w_x, b_x, lam, w_out)


def kernel(x_prompt, x_sample, cache_win_k, cache_win_v, state_rglru_conv, state_rglru_h, state_ffn_conv, c_prompt, c_sample, w_ada, b_ada, w_in_even, ln_v_g, ln_v_b, w_sgu, b_sgu, w_out_even, w_in_odd, rg_conv_w, rg_conv_b, rg_w_a, rg_b_a, rg_w_x, rg_b_x, rg_lambda, w_out_odd, ffn_w_up, ffn_conv_w, ffn_conv_b, ffn_w_down, final_g):
    w_in_even_b = w_in_even[0].astype(BF16)
    w_out_even_b = w_out_even[0].astype(BF16)
    w_in_odd_b = w_in_odd[0].astype(BF16)
    w_out_odd_b = w_out_odd[0].astype(BF16)
    rg_w_a_b = rg_w_a[0].astype(BF16)
    rg_w_x_b = rg_w_x[0].astype(BF16)
    w_up_b = [ffn_w_up[l].astype(BF16) for l in range(2)]
    w_down_b = [ffn_w_down[l].astype(BF16) for l in range(2)]
    final_g2 = final_g.reshape(1, D_MODEL)

    pad = jnp.zeros((ADA_ROWS - BATCH - DEC_BATCH, D_MODEL), F32)
    mod = _ada(jnp.concatenate([c_prompt, c_sample, pad], axis=0), w_ada, b_ada)
    mod_p = mod[:, :BATCH].reshape(2, BATCH, 6, D_MODEL)
    mod_s = mod[:, BATCH:BATCH + DEC_BATCH]

    x = x_prompt.reshape(BATCH * SEQ, D_MODEL)
    a_out, q, k, v, kt_p, vt_p = _even_in(x, mod_p[0], w_in_even_b, ln_v_g, ln_v_b, w_sgu[0],
                                          b_sgu[0].T)
    b_out = _attn(q, k, v)
    x, ffn_st0 = _ffn(x, mod_p[0], w_up_b[0], ffn_conv_w[0], ffn_conv_b[0:1], w_down_b[0],
                      final_g2, False, pre=(a_out, b_out, w_out_even_b))
    x, rg_cst, rg_hst = _rglru(x, mod_p[1], w_in_odd_b, rg_conv_w[0], rg_conv_b, rg_w_a_b, rg_b_a,
                               rg_w_x_b, rg_b_x, rg_lambda, w_out_odd_b)
    y_p, ffn_st1 = _ffn(x, mod_p[1], w_up_b[1], ffn_conv_w[1], ffn_conv_b[1:2], w_down_b[1],
                        final_g2, True)

    y_prompt = y_p.reshape(BATCH, SEQ, D_MODEL)
    to_win = lambda t: t.reshape(1, BATCH, N_HEADS, HD, WIN).transpose(0, 1, 4, 2, 3)
    win_k_prompt = to_win(kt_p)
    win_v_prompt = to_win(vt_p)
    rglru_conv_prompt = rg_cst[None, :, SUBLANES - 3:, :]
    rglru_h_prompt = rg_hst.reshape(1, BATCH, D_RNN)
    ffn_conv_prompt = jnp.stack([ffn_st0[:, SUBLANES - 2:, :], ffn_st1[:, SUBLANES - 2:, :]])

    xs = x_sample.reshape(DEC_BATCH, D_MODEL)
    ws0 = jnp.repeat(w_sgu[0, :, 0, 0], A_GROUP).reshape(1, W_A)
    bs0 = jnp.repeat(b_sgu[0, :, 0], A_GROUP).reshape(1, W_A)
    va_s, a_s, q_s, k_s, v_s = _s_even_in(xs, mod_s[0], w_in_even_b, ln_v_g, ln_v_b, ws0, bs0)
    kt_c = cache_win_k[0].transpose(0, 2, 3, 1)
    vt_c = cache_win_v[0].transpose(0, 2, 3, 1)
    b_s, kt_n, vt_n = _s_attn(q_s, k_s, v_s, kt_c, vt_c)
    xs = _s_even_out(xs, mod_s[0], a_s, b_s.reshape(DEC_BATCH, W_B), w_out_even_b)
    st0 = state_ffn_conv[0]
    xs, up0 = _s_ffn(xs, mod_s[0], st0[:, 0], st0[:, 1], w_up_b[0], ffn_conv_w[0], ffn_conv_b[0:1],
                     w_down_b[0], final_g2, False)
    cst = state_rglru_conv[0]
    xs, xr_s, hn_s = _s_rglru(xs, mod_s[1], cst[:, 0], cst[:, 1], cst[:, 2], state_rglru_h[0],
                              w_in_odd_b, rg_conv_w[0], rg_conv_b, rg_w_a_b, rg_b_a, rg_w_x_b,
                              rg_b_x, rg_lambda, w_out_odd_b)
    st1 = state_ffn_conv[1]
    ys, up1 = _s_ffn(xs, mod_s[1], st1[:, 0], st1[:, 1], w_up_b[1], ffn_conv_w[1], ffn_conv_b[1:2],
                     w_down_b[1], final_g2, True)

    y_sample = ys.reshape(DEC_BATCH, 1, D_MODEL)
    chunk_v_sample = va_s.reshape(1, DEC_BATCH, 1, W_A)
    win_k_sample = kt_n.transpose(0, 3, 1, 2)[None]
    win_v_sample = vt_n.transpose(0, 3, 1, 2)[None]
    rglru_conv_sample = jnp.stack([cst[:, 1], cst[:, 2], xr_s], axis=1)[None]
    rglru_h_sample = hn_s[None]
    ffn_conv_sample = jnp.stack([jnp.stack([st0[:, 1], up0], axis=1),
                                 jnp.stack([st1[:, 1], up1], axis=1)])

    return (y_prompt, y_sample, win_k_prompt, win_v_prompt, rglru_conv_prompt, rglru_h_prompt,
            ffn_conv_prompt, chunk_v_sample, win_k_sample, win_v_sample, rglru_conv_sample,
            rglru_h_sample, ffn_conv_sample)
```

```python
import functools

import jax
import jax.numpy as jnp
from jax import lax
from jax.experimental import pallas as pl
from jax.experimental.pallas import tpu as pltpu

F32 = jnp.float32
BF16 = jnp.bfloat16

D_MODEL = 1024
BATCH = 4
SEQ = 4096
DEC_BATCH = 32
W_A = 512
A_GROUP = 128
G_A = 4
CHUNK = 128
W_B = 512
HD = 64
N_HEADS = 8
DILATIONS = (1, 4, 16)
N_BACK = 128
WIN = 2048
N_IN_EVEN = 2 * W_A + 3 * W_B
D_RNN = 1024
RG_BLOCK = 128
RG_HEADS = 8
RG_C = 8.0
D_FF = 2816
EPS = 1e-6
NEG_INF = -1e30
LOG2_E = 1.4426950408889634
LN_2 = 0.6931471805599453

LANES = 128
SUBLANES = 8
TM = 512
TILES_PER_SEQ = SEQ // TM
N_TILES = BATCH * TILES_PER_SEQ
FF_CHUNK = 256
N_FF_CHUNKS = D_FF // FF_CHUNK
N_SLABS = W_B // LANES
VMEM_LIMIT = 56 * 1024 * 1024


def _cparams(*sem):
    return pltpu.CompilerParams(dimension_semantics=sem, vmem_limit_bytes=VMEM_LIMIT)


def _const_spec(shape):
    nd = len(shape)
    return pl.BlockSpec(shape, lambda *_: (0,) * nd, pipeline_mode=pl.Buffered(1))


def _layer_spec(shape, layer):
    nd = len(shape)
    return pl.BlockSpec((None,) + tuple(shape), lambda *_: (layer,) + (0,) * nd,
                        pipeline_mode=pl.Buffered(1))


def _shift_rows(cur, tail, k):
    rolled = pltpu.roll(cur, k, 0)
    r = lax.broadcasted_iota(jnp.int32, (SUBLANES, cur.shape[1]), 0)
    head = jnp.where(r < k, pltpu.roll(tail, k, 0), rolled[0:SUBLANES])
    return jnp.concatenate([head, rolled[SUBLANES:]], axis=0)


def _rms_mod(x, scale, shift):
    xn = x * lax.rsqrt(jnp.mean(x * x, axis=-1, keepdims=True) + EPS)
    return xn * (1.0 + scale) + shift


def _rmsnorm(x):
    return x * lax.rsqrt(jnp.mean(x * x, axis=-1, keepdims=True) + EPS)


def _layernorm(x, g, b):
    mu = jnp.mean(x, axis=-1, keepdims=True)
    xc = x - mu
    var = jnp.mean(xc * xc, axis=-1, keepdims=True)
    return xc * lax.rsqrt(var + EPS) * g + b


def _dot(a, b):
    return jnp.dot(a, b, preferred_element_type=F32)


ADA_ROWS = 40
ADA_TN = 1024


def _ada_kernel(c_ref, w_ref, b_ref, o_ref):
    c = c_ref[...]
    s = (c * jax.nn.sigmoid(c)).astype(BF16)
    o_ref[0] = _dot(s, w_ref[0].astype(BF16)) + b_ref[0]


def _ada(c_all, w_ada, b_ada):
    depth = w_ada.shape[0]
    n_out = w_ada.shape[2]
    return pl.pallas_call(
        _ada_kernel,
        grid=(depth, n_out // ADA_TN),
        in_specs=[
            pl.BlockSpec((ADA_ROWS, D_MODEL), lambda l, j: (0, 0)),
            pl.BlockSpec((1, D_MODEL, ADA_TN), lambda l, j: (l, 0, j)),
            pl.BlockSpec((1, 1, ADA_TN), lambda l, j: (l, 0, j)),
        ],
        out_specs=pl.BlockSpec((1, ADA_ROWS, ADA_TN), lambda l, j: (l, 0, j)),
        out_shape=jax.ShapeDtypeStruct((depth, ADA_ROWS, n_out), F32),
        compiler_params=_cparams("arbitrary", "arbitrary"),
        name="ada_mod",
    )(c_all, w_ada, b_ada.reshape(depth, 1, n_out))


def _even_in_kernel(x_ref, mod_ref, w_ref, lng_ref, lnb_ref, ws_ref, bst_ref,
                    a_ref, q_ref, k_ref, v_ref, kt_ref, vt_ref):
    i = pl.program_id(0)
    m = mod_ref[0]
    h = _rms_mod(x_ref[...], m[1:2], m[0:1]).astype(BF16)
    u = jax.nn.gelu(_dot(h, w_ref[:, 0:W_A]))
    va = _layernorm(jax.nn.gelu(_dot(h, w_ref[:, W_A:2 * W_A])), lng_ref[...], lnb_ref[...])
    vab = va.astype(BF16)
    row = lax.broadcasted_iota(jnp.int32, (CHUNK, CHUNK), 0)
    col = lax.broadcasted_iota(jnp.int32, (CHUNK, CHUNK), 1)
    causal = col <= row
    bst = bst_ref[...]
    for g in range(G_A):
        wg = jnp.where(causal, ws_ref[g], 0.0).astype(BF16)
        bias = bst[:, g:g + 1]
        lo, hi = g * A_GROUP, (g + 1) * A_GROUP
        for c in range(TM // CHUNK):
            r0, r1 = c * CHUNK, (c + 1) * CHUNK
            mix = _dot(wg, vab[r0:r1, lo:hi]) + bias
            a_ref[r0:r1, lo:hi] = (u[r0:r1, lo:hi] * mix).astype(BF16)
    base = 2 * W_A
    q = _dot(h, w_ref[:, base:base + W_B])
    k = _dot(h, w_ref[:, base + W_B:base + 2 * W_B])
    v = _dot(h, w_ref[:, base + 2 * W_B:base + 3 * W_B])
    for s in range(N_SLABS):
        q_ref[s] = q[:, s * LANES:(s + 1) * LANES]
        k_ref[s] = k[:, s * LANES:(s + 1) * LANES]
        v_ref[s] = v[:, s * LANES:(s + 1) * LANES]

    @pl.when(i % TILES_PER_SEQ >= TILES_PER_SEQ - WIN // TM)
    def _():
        kt_ref[0] = k.T
        vt_ref[0] = v.T


def _even_in(x2d, mod, w_in, ln_g, ln_b, w_s, b_s_t):
    n_tok = x2d.shape[0]
    first_win_tile = TILES_PER_SEQ - WIN // TM
    qkv_shape = jax.ShapeDtypeStruct((N_SLABS, n_tok, LANES), F32)
    qkv_spec = pl.BlockSpec((N_SLABS, TM, LANES), lambda i: (0, i, 0))
    win_shape = jax.ShapeDtypeStruct((BATCH, W_B, WIN), F32)
    win_spec = pl.BlockSpec(
        (1, W_B, TM),
        lambda i: (i // TILES_PER_SEQ, 0, jnp.maximum(i % TILES_PER_SEQ - first_win_tile, 0)))
    return pl.pallas_call(
        _even_in_kernel,
        grid=(N_TILES,),
        in_specs=[
            pl.BlockSpec((TM, D_MODEL), lambda i: (i, 0)),
            pl.BlockSpec((1, 6, D_MODEL), lambda i: (i // TILES_PER_SEQ, 0, 0)),
            _const_spec((D_MODEL, N_IN_EVEN)),
            _const_spec((1, W_A)),
            _const_spec((1, W_A)),
            _const_spec((G_A, CHUNK, CHUNK)),
            _const_spec((CHUNK, G_A)),
        ],
        out_specs=[
            pl.BlockSpec((TM, W_A), lambda i: (i, 0)),
            qkv_spec, qkv_spec, qkv_spec, win_spec, win_spec,
        ],
        out_shape=[
            jax.ShapeDtypeStruct((n_tok, W_A), BF16),
            qkv_shape, qkv_shape, qkv_shape, win_shape, win_shape,
        ],
        compiler_params=_cparams("arbitrary"),
        name="even_in",
    )(x2d, mod, w_in, ln_g, ln_b, w_s, b_s_t)


ATT_BLK = 128
ATT_UNROLL = 8


def _attn_kernel(q_ref, k_ref, v_ref, o_ref, acc_ref, max_ref, den_ref):
    q2, k2, v2 = q_ref.at[0], k_ref.at[0], v_ref.at[0]
    qi = lax.broadcasted_iota(jnp.int32, (ATT_BLK, 2 * ATT_BLK), 0)
    kj = lax.broadcasted_iota(jnp.int32, (ATT_BLK, 2 * ATT_BLK), 1)
    dist = qi + ATT_BLK - kj
    band = (dist >= 0) & (dist <= N_BACK)
    bias_full = jnp.where(band, 0.0, NEG_INF).astype(F32)
    bias_first = jnp.where(band & (kj >= ATT_BLK), 0.0, NEG_INF).astype(F32)
    lane = lax.broadcasted_iota(jnp.int32, (ATT_BLK, LANES), 1)
    head0 = lane < HD

    for p, d in enumerate(DILATIONS):
        n_blk = SEQ // (d * ATT_BLK)

        def unit(idx, carry, p=p, d=d, n_blk=n_blk):
            k_prev, v_prev = carry
            c = idx // n_blk
            b = idx % n_blk
            own = c + d * ATT_BLK * b
            if d == 1:
                rows = pl.ds(pl.multiple_of(own, ATT_BLK), ATT_BLK)
            else:
                rows = pl.ds(own, ATT_BLK, stride=d)
            qb = q2[rows, :] * (HD ** -0.5 * LOG2_E)
            k_own = k2[rows, :].astype(BF16)
            v_own = v2[rows, :].astype(BF16)
            kb = jnp.concatenate([k_prev, k_own], axis=0)
            vb = jnp.concatenate([v_prev, v_own], axis=0)
            bias = jnp.where(b == 0, bias_first, bias_full)
            outs, mxs, dens = [], [], []
            for hh in range(2):
                sel = head0 if hh == 0 else jnp.logical_not(head0)
                qm = jnp.where(sel, qb, 0.0).astype(BF16)
                s = lax.dot_general(qm, kb, (((1,), (1,)), ((), ())),
                                    preferred_element_type=F32) + bias
                mx = jnp.max(s, axis=-1, keepdims=True)
                e = jnp.exp2(s - mx)
                dens.append(jnp.sum(e, axis=-1, keepdims=True))
                mxs.append(mx)
                outs.append(_dot(e.astype(BF16), vb))
            acc_ref[p, rows, :] = jnp.where(head0, outs[0], outs[1])
            max_ref[p, rows, :] = jnp.where(head0, mxs[0], mxs[1])
            den_ref[p, rows, :] = jnp.where(head0, dens[0], dens[1])
            return k_own, v_own

        zeros = jnp.zeros((ATT_BLK, LANES), BF16)
        lax.fori_loop(0, SEQ // ATT_BLK, unit, (zeros, zeros), unroll=ATT_UNROLL)

    def mix(t, carry):
        rows = pl.ds(pl.multiple_of(t * TM, TM), TM)
        m0, m1, m2 = max_ref[0, rows, :], max_ref[1, rows, :], max_ref[2, rows, :]
        mx = jnp.maximum(jnp.maximum(m0, m1), m2)
        e0, e1, e2 = jnp.exp2(m0 - mx), jnp.exp2(m1 - mx), jnp.exp2(m2 - mx)
        num = e0 * acc_ref[0, rows, :] + e1 * acc_ref[1, rows, :] + e2 * acc_ref[2, rows, :]
        den = e0 * den_ref[0, rows, :] + e1 * den_ref[1, rows, :] + e2 * den_ref[2, rows, :]
        o_ref[rows, :] = (num / den).astype(BF16)
        return carry

    lax.fori_loop(0, SEQ // TM, mix, 0)


def _attn(q, k, v):
    n_tok = q.shape[1]
    spec = pl.BlockSpec((1, SEQ, LANES), lambda n, s: (s, n, 0))
    return pl.pallas_call(
        _attn_kernel,
        grid=(BATCH, N_SLABS),
        in_specs=[spec, spec, spec],
        out_specs=pl.BlockSpec((SEQ, LANES), lambda n, s: (n, s)),
        out_shape=jax.ShapeDtypeStruct((n_tok, W_B), BF16),
        scratch_shapes=[pltpu.VMEM((len(DILATIONS), SEQ, LANES), F32)] * 3,
        compiler_params=_cparams("arbitrary", "arbitrary"),
        name="dil_attn",
    )(q, k, v)


def _ffn_kernel(*refs, pre, final):
    if pre:
        (x_ref, mod_ref, a_ref, b_ref, wo_ref, wup_ref, cw_ref, cb_ref, wdn_ref, fg_ref,
         o_ref, st_ref, carry_ref, act_ref) = refs
    else:
        (x_ref, mod_ref, wup_ref, cw_ref, cb_ref, wdn_ref, fg_ref,
         o_ref, st_ref, carry_ref, act_ref) = refs
    i = pl.program_id(0)

    @pl.when(i % TILES_PER_SEQ == 0)
    def _():
        carry_ref[...] = jnp.zeros_like(carry_ref)

    m = mod_ref[0]
    cw = cw_ref[...]
    cb = cb_ref[...]
    x = x_ref[...]
    if pre:
        x = x + m[2:3] * (_dot(a_ref[...], wo_ref[0:W_A, :]) + _dot(b_ref[...], wo_ref[W_A:W_A + W_B, :]))
    h = _rms_mod(x, m[4:5], m[3:4]).astype(BF16)
    for j in range(N_FF_CHUNKS):
        conv = []
        for half in range(2):
            c0 = half * D_FF + j * FF_CHUNK
            cols = slice(c0, c0 + FF_CHUNK)
            up = _dot(h, wup_ref[:, cols])
            tail = carry_ref[:, cols]
            carry_ref[:, cols] = up[TM - SUBLANES:, :]
            conv.append(cb[:, cols] + cw[0:1, cols] * _shift_rows(up, tail, 2)
                        + cw[1:2, cols] * _shift_rows(up, tail, 1) + cw[2:3, cols] * up)
        act = conv[0] * jax.nn.sigmoid(conv[0]) * conv[1]
        act_ref[:, j * FF_CHUNK:(j + 1) * FF_CHUNK] = act.astype(BF16)
    st_ref[0] = carry_ref[...]
    y = x + m[5:6] * _dot(act_ref[...], wdn_ref[...])
    if final:
        y = _rmsnorm(y) * fg_ref[...]
    o_ref[...] = y


def _ffn(x2d, mod, layer, w_up, conv_w, conv_b, w_down, final_g, final, pre=None):
    n_tok = x2d.shape[0]
    f2 = 2 * D_FF
    row_spec = pl.BlockSpec((TM, D_MODEL), lambda i: (i, 0))
    mod_spec = pl.BlockSpec((1, 6, D_MODEL), lambda i: (i // TILES_PER_SEQ, 0, 0))
    args, specs = [x2d, mod], [row_spec, mod_spec]
    if pre is not None:
        args += list(pre)
        specs += [pl.BlockSpec((TM, W_A), lambda i: (i, 0)),
                  pl.BlockSpec((TM, W_B), lambda i: (i, 0)),
                  _const_spec((W_A + W_B, D_MODEL))]
    args += [w_up, conv_w, conv_b, w_down, final_g]
    specs += [_layer_spec((D_MODEL, f2), layer), _layer_spec((3, f2), layer),
              _layer_spec((1, f2), layer), _layer_spec((D_FF, D_MODEL), layer),
              _const_spec((1, D_MODEL))]
    return pl.pallas_call(
        functools.partial(_ffn_kernel, pre=pre is not None, final=final),
        grid=(n_tok // TM,),
        in_specs=specs,
        out_specs=[
            row_spec,
            pl.BlockSpec((1, SUBLANES, f2), lambda i: (i // TILES_PER_SEQ, 0, 0)),
        ],
        out_shape=[
            jax.ShapeDtypeStruct((n_tok, D_MODEL), F32),
            jax.ShapeDtypeStruct((BATCH, SUBLANES, f2), F32),
        ],
        scratch_shapes=[
            pltpu.VMEM((SUBLANES, f2), F32),
            pltpu.VMEM((TM, D_FF), BF16),
        ],
        compiler_params=_cparams("arbitrary"),
        name="conv_ffn_final" if final else "conv_ffn",
    )(*args)


N_GROUPS = TM // SUBLANES


def _rg_gates(xc, wa_ref, ba, wx_ref, bx, lam):
    xcb = xc.astype(BF16)
    r_parts, i_parts = [], []
    for g in range(RG_HEADS):
        blk = xcb[:, g * RG_BLOCK:(g + 1) * RG_BLOCK]
        r_parts.append(_dot(blk, wa_ref[g]))
        i_parts.append(_dot(blk, wx_ref[g]))
    r = jax.nn.sigmoid(jnp.concatenate(r_parts, axis=-1) + ba)
    ig = jax.nn.sigmoid(jnp.concatenate(i_parts, axis=-1) + bx)
    log_a = r * ((-RG_C) * jax.nn.softplus(-lam))
    a = jnp.exp(log_a)
    b = jnp.sqrt(1.0 - a * a) * (ig * xc)
    return a, b


def _rglru_kernel(x_ref, mod_ref, win_ref, cw_ref, cb_ref, wa_ref, ba_ref, wx_ref, bx_ref,
                  lam_ref, wout_ref, o_ref, cst_ref, hst_ref,
                  xtail_ref, hcar_ref, a_ref, b_ref, hs_ref):
    i = pl.program_id(0)

    @pl.when(i % TILES_PER_SEQ == 0)
    def _():
        xtail_ref[...] = jnp.zeros_like(xtail_ref)
        hcar_ref[...] = jnp.zeros_like(hcar_ref)

    x = x_ref[...]
    m = mod_ref[0]
    h = _rms_mod(x, m[1:2], m[0:1]).astype(BF16)
    gate = _dot(h, win_ref[:, 0:D_RNN])
    xr = _dot(h, win_ref[:, D_RNN:2 * D_RNN])
    cw = cw_ref[...]
    tail = xtail_ref[...]
    xc = (cb_ref[...] + cw[3:4] * xr + cw[2:3] * _shift_rows(xr, tail, 1)
          + cw[1:2] * _shift_rows(xr, tail, 2) + cw[0:1] * _shift_rows(xr, tail, 3))
    last = xr[TM - SUBLANES:, :]
    xtail_ref[...] = last
    cst_ref[0] = last

    a, b = _rg_gates(xc, wa_ref, ba_ref[...], wx_ref, bx_ref[...], lam_ref[...])

    a3 = a.reshape(N_GROUPS, SUBLANES, D_RNN)
    b3 = b.reshape(N_GROUPS, SUBLANES, D_RNN)
    r_id = lax.broadcasted_iota(jnp.int32, (N_GROUPS, SUBLANES, D_RNN), 1)
    for s in (1, 2, 4):
        keep = r_id >= s
        a_sh = jnp.where(keep, pltpu.roll(a3, s, 1), 1.0)
        b_sh = jnp.where(keep, pltpu.roll(b3, s, 1), 0.0)
        b3 = a3 * b_sh + b3
        a3 = a3 * a_sh
    a_ref[...] = a3
    b_ref[...] = b3

    def step(g, hc):
        hs = a_ref[g] * hc + b_ref[g]
        hs_ref[g] = hs
        return hs[SUBLANES - 1:SUBLANES, :]

    hc = lax.fori_loop(0, N_GROUPS, step, hcar_ref[...])
    hcar_ref[...] = hc
    hst_ref[0] = hc

    y = (jax.nn.gelu(gate) * hs_ref[...].reshape(TM, D_RNN)).astype(BF16)
    o_ref[...] = x + m[2:3] * _dot(y, wout_ref[...])


def _rglru(x2d, mod, w_in, conv_w, conv_b, w_a, b_a, w_x, b_x, lam, w_out):
    n_tok = x2d.shape[0]
    return pl.pallas_call(
        _rglru_kernel,
        grid=(n_tok // TM,),
        in_specs=[
            pl.BlockSpec((TM, D_MODEL), lambda i: (i, 0)),
            pl.BlockSpec((1, 6, D_MODEL), lambda i: (i // TILES_PER_SEQ, 0, 0)),
            _const_spec((D_MODEL, 2 * D_RNN)),
            _const_spec((4, D_RNN)),
            _const_spec((1, D_RNN)),
            _const_spec((RG_HEADS, RG_BLOCK, RG_BLOCK)),
            _const_spec((1, D_RNN)),
            _const_spec((RG_HEADS, RG_BLOCK, RG_BLOCK)),
            _const_spec((1, D_RNN)),
            _const_spec((1, D_RNN)),
            _const_spec((D_RNN, D_MODEL)),
        ],
        out_specs=[
            pl.BlockSpec((TM, D_MODEL), lambda i: (i, 0)),
            pl.BlockSpec((1, SUBLANES, D_RNN), lambda i: (i // TILES_PER_SEQ, 0, 0)),
            pl.BlockSpec((1, 1, D_RNN), lambda i: (i // TILES_PER_SEQ, 0, 0)),
        ],
        out_shape=[
            jax.ShapeDtypeStruct((n_tok, D_MODEL), F32),
            jax.ShapeDtypeStruct((BATCH, SUBLANES, D_RNN), F32),
            jax.ShapeDtypeStruct((BATCH, 1, D_RNN), F32),
        ],
        scratch_shapes=[
            pltpu.VMEM((SUBLANES, D_RNN), F32),
            pltpu.VMEM((1, D_RNN), F32),
            pltpu.VMEM((N_GROUPS, SUBLANES, D_RNN), F32),
            pltpu.VMEM((N_GROUPS, SUBLANES, D_RNN), F32),
            pltpu.VMEM((N_GROUPS, SUBLANES, D_RNN), F32),
        ],
        compiler_params=_cparams("arbitrary"),
        name="rglru",
    )(x2d, mod, w_in, conv_w, conv_b, w_a, b_a, w_x, b_x, lam, w_out)


def _s_even_in_kernel(x_ref, mod_ref, w_ref, lng_ref, lnb_ref, ws0_ref, bs0_ref,
                      va_ref, a_ref, q_ref, k_ref, v_ref):
    mod = mod_ref[...]
    h = _rms_mod(x_ref[...], mod[:, D_MODEL:2 * D_MODEL], mod[:, 0:D_MODEL]).astype(BF16)
    u = jax.nn.gelu(_dot(h, w_ref[:, 0:W_A]))
    va = _layernorm(jax.nn.gelu(_dot(h, w_ref[:, W_A:2 * W_A])), lng_ref[...], lnb_ref[...])
    va_ref[...] = va
    a_ref[...] = (u * (ws0_ref[...] * va + bs0_ref[...])).astype(BF16)
    base = 2 * W_A
    q_ref[...] = _dot(h, w_ref[:, base:base + W_B])
    k_ref[...] = _dot(h, w_ref[:, base + W_B:base + 2 * W_B])
    v_ref[...] = _dot(h, w_ref[:, base + 2 * W_B:base + 3 * W_B])


def _s_even_in(x, mod, w_in, ln_g, ln_b, ws0, bs0):
    n = x.shape[0]
    f = jax.ShapeDtypeStruct((n, W_B), F32)
    return pl.pallas_call(
        _s_even_in_kernel,
        out_shape=[f, jax.ShapeDtypeStruct((n, W_A), BF16), f, f, f],
        compiler_params=pltpu.CompilerParams(vmem_limit_bytes=VMEM_LIMIT),
        name="s_even_in",
    )(x, mod, w_in, ln_g, ln_b, ws0, bs0)


def _s_attn_kernel(q_ref, kn_ref, vn_ref, kt_ref, vt_ref, o_ref, kto_ref, vto_ref):
    n_hd = N_HEADS * HD
    qrow = q_ref[0] * (HD ** -0.5)
    knrow = kn_ref[0]
    vnrow = vn_ref[0]
    hrow = lax.broadcasted_iota(jnp.int32, (N_HEADS, n_hd), 0)
    hcol = lax.broadcasted_iota(jnp.int32, (N_HEADS, n_hd), 1) // HD
    own = hrow == hcol
    qbd = jnp.where(own, jnp.broadcast_to(qrow, (N_HEADS, n_hd)), 0.0)
    kt = kt_ref[0].reshape(n_hd, WIN)
    vt = vt_ref[0].reshape(n_hd, WIN)
    s = _dot(qbd.astype(BF16), kt.astype(BF16))
    s_new = jnp.sum(qbd * knrow, axis=-1, keepdims=True)
    t = lax.broadcasted_iota(jnp.int32, (N_HEADS, WIN), 1)
    dist = WIN - t
    vtb = vt.astype(BF16)
    num = jnp.zeros((N_HEADS, n_hd), F32)
    outs, lses = [], []
    for d in DILATIONS:
        valid = ((dist & (d - 1)) == 0) & (dist <= N_BACK * d)
        sm = jnp.where(valid, s, NEG_INF)
        mx = jnp.maximum(jnp.max(sm, axis=-1, keepdims=True), s_new)
        e = jnp.where(valid, jnp.exp(sm - mx), 0.0)
        e_new = jnp.exp(s_new - mx)
        den = jnp.sum(e, axis=-1, keepdims=True) + e_new
        o = lax.dot_general(e.astype(BF16), vtb, (((1,), (1,)), ((), ())),
                            preferred_element_type=F32)
        outs.append((o + e_new * vnrow) / den)
        lses.append(mx + jnp.log(den))
    mx = jnp.maximum(jnp.maximum(lses[0], lses[1]), lses[2])
    ws = [jnp.exp(l - mx) for l in lses]
    num = ws[0] * outs[0] + ws[1] * outs[1] + ws[2] * outs[2]
    mixed = num / (ws[0] + ws[1] + ws[2])
    o_ref[0] = jnp.sum(jnp.where(own, mixed, 0.0), axis=0, keepdims=True)

    r2 = lax.broadcasted_iota(jnp.int32, (n_hd, n_hd), 0)
    c2 = lax.broadcasted_iota(jnp.int32, (n_hd, n_hd), 1)
    diag = r2 == c2
    kcol = jnp.sum(jnp.where(diag, jnp.broadcast_to(knrow, (n_hd, n_hd)), 0.0), axis=-1, keepdims=True)
    vcol = jnp.sum(jnp.where(diag, jnp.broadcast_to(vnrow, (n_hd, n_hd)), 0.0), axis=-1, keepdims=True)
    tt = lax.broadcasted_iota(jnp.int32, (n_hd, WIN), 1)
    is_last = tt == WIN - 1
    kto_ref[0] = jnp.where(is_last, kcol, pltpu.roll(kt, WIN - 1, 1)).reshape(N_HEADS, HD, WIN)
    vto_ref[0] = jnp.where(is_last, vcol, pltpu.roll(vt, WIN - 1, 1)).reshape(N_HEADS, HD, WIN)


def _s_attn(q, k_new, v_new, kt, vt):
    n = q.shape[0]
    n_hd = N_HEADS * HD
    row_spec = pl.BlockSpec((1, 1, n_hd), lambda i: (i, 0, 0))
    cache_spec = pl.BlockSpec((1, N_HEADS, HD, WIN), lambda i: (i, 0, 0, 0))
    cache_shape = jax.ShapeDtypeStruct((n, N_HEADS, HD, WIN), F32)
    return pl.pallas_call(
        _s_attn_kernel,
        grid=(n,),
        in_specs=[row_spec, row_spec, row_spec, cache_spec, cache_spec],
        out_specs=[row_spec, cache_spec, cache_spec],
        out_shape=[jax.ShapeDtypeStruct((n, 1, n_hd), F32), cache_shape, cache_shape],
        compiler_params=_cparams("arbitrary"),
        name="s_attn",
    )(q.reshape(n, 1, n_hd), k_new.reshape(n, 1, n_hd), v_new.reshape(n, 1, n_hd), kt, vt)


def _s_even_out_kernel(x_ref, mod_ref, a_ref, b_ref, w_ref, o_ref):
    g1 = mod_ref[:, 2 * D_MODEL:3 * D_MODEL]
    mo = _dot(a_ref[...], w_ref[0:W_A, :]) + _dot(b_ref[...].astype(BF16), w_ref[W_A:W_A + W_B, :])
    o_ref[...] = x_ref[...] + g1 * mo


def _s_even_out(x, mod, a_out, b_out, w_out):
    return pl.pallas_call(
        _s_even_out_kernel,
        out_shape=jax.ShapeDtypeStruct(x.shape, F32),
        compiler_params=pltpu.CompilerParams(vmem_limit_bytes=VMEM_LIMIT),
        name="s_even_out",
    )(x, mod, a_out, b_out, w_out)


def _s_ffn_kernel(x_ref, mod_ref, p2_ref, p1_ref, wup_ref, cw_ref, cb_ref, wdn_ref, fg_ref,
                  o_ref, up_ref, *, final):
    x = x_ref[...]
    mod = mod_ref[...]
    h = _rms_mod(x, mod[:, 4 * D_MODEL:5 * D_MODEL], mod[:, 3 * D_MODEL:4 * D_MODEL]).astype(BF16)
    up = _dot(h, wup_ref[...])
    up_ref[...] = up
    cw = cw_ref[...]
    conv = cb_ref[...] + cw[0:1] * p2_ref[...] + cw[1:2] * p1_ref[...] + cw[2:3] * up
    ca, cg = conv[:, 0:D_FF], conv[:, D_FF:2 * D_FF]
    act = (ca * jax.nn.sigmoid(ca) * cg).astype(BF16)
    y = x + mod[:, 5 * D_MODEL:6 * D_MODEL] * _dot(act, wdn_ref[...])
    if final:
        y = _rmsnorm(y) * fg_ref[...]
    o_ref[...] = y


def _s_ffn(x, mod, p2, p1, layer, w_up, conv_w, conv_b, w_down, final_g, final):
    n = x.shape[0]
    f2 = 2 * D_FF
    return pl.pallas_call(
        functools.partial(_s_ffn_kernel, final=final),
        grid=(1,),
        in_specs=[_const_spec(x.shape), _const_spec(mod.shape), _const_spec(p2.shape),
                  _const_spec(p1.shape), _layer_spec((D_MODEL, f2), layer),
                  _layer_spec((3, f2), layer), _layer_spec((1, f2), layer),
                  _layer_spec((D_FF, D_MODEL), layer), _const_spec((1, D_MODEL))],
        out_specs=[pl.BlockSpec(x.shape, lambda i: (0, 0)), pl.BlockSpec((n, f2), lambda i: (0, 0))],
        out_shape=[jax.ShapeDtypeStruct(x.shape, F32), jax.ShapeDtypeStruct((n, f2), F32)],
        compiler_params=_cparams("arbitrary"),
        name="s_conv_ffn_final" if final else "s_conv_ffn",
    )(x, mod, p2, p1, w_up, conv_w, conv_b, w_down, final_g)


def _s_rglru_kernel(x_ref, mod_ref, c0_ref, c1_ref, c2_ref, h0_ref, win_ref, cw_ref, cb_ref,
                    wa_ref, ba_ref, wx_ref, bx_ref, lam_ref, wout_ref, o_ref, xr_ref, hn_ref):
    x = x_ref[...]
    mod = mod_ref[...]
    h = _rms_mod(x, mod[:, D_MODEL:2 * D_MODEL], mod[:, 0:D_MODEL]).astype(BF16)
    gate = _dot(h, win_ref[:, 0:D_RNN])
    xr = _dot(h, win_ref[:, D_RNN:2 * D_RNN])
    xr_ref[...] = xr
    cw = cw_ref[...]
    xc = (cb_ref[...] + cw[0:1] * c0_ref[...] + cw[1:2] * c1_ref[...] + cw[2:3] * c2_ref[...]
          + cw[3:4] * xr)
    a, b = _rg_gates(xc, wa_ref, ba_ref[...], wx_ref, bx_ref[...], lam_ref[...])
    hn = a * h0_ref[...] + b
    hn_ref[...] = hn
    y = (jax.nn.gelu(gate) * hn).astype(BF16)
    o_ref[...] = x + mod[:, 2 * D_MODEL:3 * D_MODEL] * _dot(y, wout_ref[...])


def _s_rglru(x, mod, c0, c1, c2, h0, w_in, conv_w, conv_b, w_a, b_a, w_x, b_x, lam, w_out):
    f = jax.ShapeDtypeStruct(x.shape, F32)
    return pl.pallas_call(
        _s_rglru_kernel,
        out_shape=[f, f, f],
        compiler_params=pltpu.CompilerParams(vmem_limit_bytes=VMEM_LIMIT),
        name="s_rglru",
    )(x, mod, c0, c1, c2, h0, w_in, conv_w, conv_b, w_a, b_a, w_x, b_x, lam, w_out)


def kernel(x_prompt, x_sample, cache_win_k, cache_win_v, state_rglru_conv, state_rglru_h, state_ffn_conv, c_prompt, c_sample, w_ada, b_ada, w_in_even, ln_v_g, ln_v_b, w_sgu, b_sgu, w_out_even, w_in_odd, rg_conv_w, rg_conv_b, rg_w_a, rg_b_a, rg_w_x, rg_b_x, rg_lambda, w_out_odd, ffn_w_up, ffn_conv_w, ffn_conv_b, ffn_w_down, final_g):
    w_in_even_b = w_in_even[0].astype(BF16)
    w_out_even_b = w_out_even[0].astype(BF16)
    w_in_odd_b = w_in_odd[0].astype(BF16)
    w_out_odd_b = w_out_odd[0].astype(BF16)
    rg_w_a_b = rg_w_a[0].astype(BF16)
    rg_w_x_b = rg_w_x[0].astype(BF16)
    w_up_b = ffn_w_up.astype(BF16)
    w_down_b = ffn_w_down.astype(BF16)
    conv_b3 = ffn_conv_b.reshape(2, 1, 2 * D_FF)
    final_g2 = final_g.reshape(1, D_MODEL)

    pad = jnp.zeros((ADA_ROWS - BATCH - DEC_BATCH, D_MODEL), F32)
    mod = _ada(jnp.concatenate([c_prompt, c_sample, pad], axis=0), w_ada, b_ada)
    mod_p = mod[:, :BATCH].reshape(2, BATCH, 6, D_MODEL)
    mod_s = mod[:, BATCH:BATCH + DEC_BATCH]

    x = x_prompt.reshape(BATCH * SEQ, D_MODEL)
    a_out, q, k, v, kt_p, vt_p = _even_in(x, mod_p[0], w_in_even_b, ln_v_g, ln_v_b, w_sgu[0],
                                          b_sgu[0].T)
    b_out = _attn(q, k, v)
    x, ffn_st0 = _ffn(x, mod_p[0], 0, w_up_b, ffn_conv_w, conv_b3, w_down_b, final_g2, False,
                      pre=(a_out, b_out, w_out_even_b))
    x, rg_cst, rg_hst = _rglru(x, mod_p[1], w_in_odd_b, rg_conv_w[0], rg_conv_b, rg_w_a_b, rg_b_a,
                               rg_w_x_b, rg_b_x, rg_lambda, w_out_odd_b)
    y_p, ffn_st1 = _ffn(x, mod_p[1], 1, w_up_b, ffn_conv_w, conv_b3, w_down_b, final_g2, True)

    y_prompt = y_p.reshape(BATCH, SEQ, D_MODEL)
    to_win = lambda t: t.reshape(1, BATCH, N_HEADS, HD, WIN).transpose(0, 1, 4, 2, 3)
    win_k_prompt = to_win(kt_p)
    win_v_prompt = to_win(vt_p)
    rglru_conv_prompt = rg_cst[None, :, SUBLANES - 3:, :]
    rglru_h_prompt = rg_hst.reshape(1, BATCH, D_RNN)
    ffn_conv_prompt = jnp.stack([ffn_st0[:, SUBLANES - 2:, :], ffn_st1[:, SUBLANES - 2:, :]])

    xs = x_sample.reshape(DEC_BATCH, D_MODEL)
    ws0 = jnp.repeat(w_sgu[0, :, 0, 0], A_GROUP).reshape(1, W_A)
    bs0 = jnp.repeat(b_sgu[0, :, 0], A_GROUP).reshape(1, W_A)
    va_s, a_s, q_s, k_s, v_s = _s_even_in(xs, mod_s[0], w_in_even_b, ln_v_g, ln_v_b, ws0, bs0)
    kt_c = cache_win_k[0].transpose(0, 2, 3, 1)
    vt_c = cache_win_v[0].transpose(0, 2, 3, 1)
    b_s, kt_n, vt_n = _s_attn(q_s, k_s, v_s, kt_c, vt_c)
    xs = _s_even_out(xs, mod_s[0], a_s, b_s.reshape(DEC_BATCH, W_B), w_out_even_b)
    st0 = state_ffn_conv[0]
    xs, up0 = _s_ffn(xs, mod_s[0], st0[:, 0], st0[:, 1], 0, w_up_b, ffn_conv_w, conv_b3, w_down_b,
                     final_g2, False)
    cst = state_rglru_conv[0]
    xs, xr_s, hn_s = _s_rglru(xs, mod_s[1], cst[:, 0], cst[:, 1], cst[:, 2], state_rglru_h[0],
                              w_in_odd_b, rg_conv_w[0], rg_conv_b, rg_w_a_b, rg_b_a, rg_w_x_b,
                              rg_b_x, rg_lambda, w_out_odd_b)
    st1 = state_ffn_conv[1]
    ys, up1 = _s_ffn(xs, mod_s[1], st1[:, 0], st1[:, 1], 1, w_up_b, ffn_conv_w, conv_b3, w_down_b,
                     final_g2, True)

    y_sample = ys.reshape(DEC_BATCH, 1, D_MODEL)
    chunk_v_sample = va_s.reshape(1, DEC_BATCH, 1, W_A)
    win_k_sample = kt_n.transpose(0, 3, 1, 2)[None]
    win_v_sample = vt_n.transpose(0, 3, 1, 2)[None]
    rglru_conv_sample = jnp.stack([cst[:, 1], cst[:, 2], xr_s], axis=1)[None]
    rglru_h_sample = hn_s[None]
    ffn_conv_sample = jnp.stack([jnp.stack([st0[:, 1], up0], axis=1),
                                 jnp.stack([st1[:, 1], up1], axis=1)])

    return (y_prompt, y_sample, win_k_prompt, win_v_prompt, rglru_conv_prompt, rglru_h_prompt,
            ffn_conv_prompt, chunk_v_sample, win_k_sample, win_v_sample, rglru_conv_sample,
            rglru_h_sample, ffn_conv_sample)
```

```python
import functools

import jax
import jax.numpy as jnp
from jax import lax
from jax.experimental import pallas as pl
from jax.experimental.pallas import tpu as pltpu

F32 = jnp.float32
BF16 = jnp.bfloat16

D_MODEL = 1024
BATCH = 4
SEQ = 4096
DEC_BATCH = 32
W_A = 512
A_GROUP = 128
G_A = 4
CHUNK = 128
W_B = 512
HD = 64
N_HEADS = 8
DILATIONS = (1, 4, 16)
N_BACK = 128
WIN = 2048
N_IN_EVEN = 2 * W_A + 3 * W_B
D_RNN = 1024
RG_BLOCK = 128
RG_HEADS = 8
RG_C = 8.0
D_FF = 2816
EPS = 1e-6
NEG_INF = -1e30
LOG2_E = 1.4426950408889634
LN_2 = 0.6931471805599453

LANES = 128
SUBLANES = 8
TM = 512
TILES_PER_SEQ = SEQ // TM
N_TILES = BATCH * TILES_PER_SEQ
FF_CHUNK = 256
N_FF_CHUNKS = D_FF // FF_CHUNK
N_SLABS = W_B // LANES
VMEM_LIMIT = 56 * 1024 * 1024


def _cparams(*sem):
    return pltpu.CompilerParams(dimension_semantics=sem, vmem_limit_bytes=VMEM_LIMIT)


def _const_spec(shape):
    nd = len(shape)
    return pl.BlockSpec(shape, lambda *_: (0,) * nd, pipeline_mode=pl.Buffered(1))


def _layer_spec(shape, layer):
    nd = len(shape)
    return pl.BlockSpec((None,) + tuple(shape), lambda *_: (layer,) + (0,) * nd,
                        pipeline_mode=pl.Buffered(1))


def _shift_rows(cur, tail, k):
    rolled = pltpu.roll(cur, k, 0)
    r = lax.broadcasted_iota(jnp.int32, (SUBLANES, cur.shape[1]), 0)
    head = jnp.where(r < k, pltpu.roll(tail, k, 0), rolled[0:SUBLANES])
    return jnp.concatenate([head, rolled[SUBLANES:]], axis=0)


def _rms_mod(x, scale, shift):
    xn = x * lax.rsqrt(jnp.mean(x * x, axis=-1, keepdims=True) + EPS)
    return xn * (1.0 + scale) + shift


def _rmsnorm(x):
    return x * lax.rsqrt(jnp.mean(x * x, axis=-1, keepdims=True) + EPS)


def _layernorm(x, g, b):
    mu = jnp.mean(x, axis=-1, keepdims=True)
    xc = x - mu
    var = jnp.mean(xc * xc, axis=-1, keepdims=True)
    return xc * lax.rsqrt(var + EPS) * g + b


def _dot(a, b):
    return jnp.dot(a, b, preferred_element_type=F32)


ADA_ROWS = 40
ADA_TN = 1024


def _ada_kernel(c_ref, w_ref, b_ref, o_ref):
    c = c_ref[...]
    s = (c * jax.nn.sigmoid(c)).astype(BF16)
    o_ref[0] = _dot(s, w_ref[0].astype(BF16)) + b_ref[0]


def _ada(c_all, w_ada, b_ada):
    depth = w_ada.shape[0]
    n_out = w_ada.shape[2]
    return pl.pallas_call(
        _ada_kernel,
        grid=(depth, n_out // ADA_TN),
        in_specs=[
            pl.BlockSpec((ADA_ROWS, D_MODEL), lambda l, j: (0, 0)),
            pl.BlockSpec((1, D_MODEL, ADA_TN), lambda l, j: (l, 0, j)),
            pl.BlockSpec((1, 1, ADA_TN), lambda l, j: (l, 0, j)),
        ],
        out_specs=pl.BlockSpec((1, ADA_ROWS, ADA_TN), lambda l, j: (l, 0, j)),
        out_shape=jax.ShapeDtypeStruct((depth, ADA_ROWS, n_out), F32),
        compiler_params=_cparams("arbitrary", "arbitrary"),
        name="ada_mod",
    )(c_all, w_ada, b_ada.reshape(depth, 1, n_out))


def _even_in_kernel(x_ref, mod_ref, w_ref, lng_ref, lnb_ref, ws_ref, bst_ref,
                    a_ref, q_ref, k_ref, v_ref, kt_ref, vt_ref):
    i = pl.program_id(0)
    m = mod_ref[0]
    h = _rms_mod(x_ref[...], m[1:2], m[0:1]).astype(BF16)
    u = jax.nn.gelu(_dot(h, w_ref[:, 0:W_A]))
    va = _layernorm(jax.nn.gelu(_dot(h, w_ref[:, W_A:2 * W_A])), lng_ref[...], lnb_ref[...])
    vab = va.astype(BF16)
    row = lax.broadcasted_iota(jnp.int32, (CHUNK, CHUNK), 0)
    col = lax.broadcasted_iota(jnp.int32, (CHUNK, CHUNK), 1)
    causal = col <= row
    bst = bst_ref[...]
    for g in range(G_A):
        wg = jnp.where(causal, ws_ref[g], 0.0).astype(BF16)
        bias = bst[:, g:g + 1]
        lo, hi = g * A_GROUP, (g + 1) * A_GROUP
        for c in range(TM // CHUNK):
            r0, r1 = c * CHUNK, (c + 1) * CHUNK
            mix = _dot(wg, vab[r0:r1, lo:hi]) + bias
            a_ref[r0:r1, lo:hi] = (u[r0:r1, lo:hi] * mix).astype(BF16)
    base = 2 * W_A
    q = _dot(h, w_ref[:, base:base + W_B])
    k = _dot(h, w_ref[:, base + W_B:base + 2 * W_B])
    v = _dot(h, w_ref[:, base + 2 * W_B:base + 3 * W_B])
    for s in range(N_SLABS):
        q_ref[s] = q[:, s * LANES:(s + 1) * LANES]
        k_ref[s] = k[:, s * LANES:(s + 1) * LANES]
        v_ref[s] = v[:, s * LANES:(s + 1) * LANES]

    @pl.when(i % TILES_PER_SEQ >= TILES_PER_SEQ - WIN // TM)
    def _():
        kt_ref[0] = k.T
        vt_ref[0] = v.T


def _even_in(x2d, mod, w_in, ln_g, ln_b, w_s, b_s_t):
    n_tok = x2d.shape[0]
    first_win_tile = TILES_PER_SEQ - WIN // TM
    qkv_shape = jax.ShapeDtypeStruct((N_SLABS, n_tok, LANES), F32)
    qkv_spec = pl.BlockSpec((N_SLABS, TM, LANES), lambda i: (0, i, 0))
    win_shape = jax.ShapeDtypeStruct((BATCH, W_B, WIN), F32)
    win_spec = pl.BlockSpec(
        (1, W_B, TM),
        lambda i: (i // TILES_PER_SEQ, 0, jnp.maximum(i % TILES_PER_SEQ - first_win_tile, 0)))
    return pl.pallas_call(
        _even_in_kernel,
        grid=(N_TILES,),
        in_specs=[
            pl.BlockSpec((TM, D_MODEL), lambda i: (i, 0)),
            pl.BlockSpec((1, 6, D_MODEL), lambda i: (i // TILES_PER_SEQ, 0, 0)),
            _const_spec((D_MODEL, N_IN_EVEN)),
            _const_spec((1, W_A)),
            _const_spec((1, W_A)),
            _const_spec((G_A, CHUNK, CHUNK)),
            _const_spec((CHUNK, G_A)),
        ],
        out_specs=[
            pl.BlockSpec((TM, W_A), lambda i: (i, 0)),
            qkv_spec, qkv_spec, qkv_spec, win_spec, win_spec,
        ],
        out_shape=[
            jax.ShapeDtypeStruct((n_tok, W_A), BF16),
            qkv_shape, qkv_shape, qkv_shape, win_shape, win_shape,
        ],
        compiler_params=_cparams("arbitrary"),
        name="even_in",
    )(x2d, mod, w_in, ln_g, ln_b, w_s, b_s_t)


ATT_BLK = 128
ATT_UNROLL = 8


def _attn_kernel(q_ref, k_ref, v_ref, o_ref, acc_ref, max_ref, den_ref):
    q2, k2, v2 = q_ref.at[0], k_ref.at[0], v_ref.at[0]
    qi = lax.broadcasted_iota(jnp.int32, (ATT_BLK, 2 * ATT_BLK), 0)
    kj = lax.broadcasted_iota(jnp.int32, (ATT_BLK, 2 * ATT_BLK), 1)
    dist = qi + ATT_BLK - kj
    band = (dist >= 0) & (dist <= N_BACK)
    bias_full = jnp.where(band, 0.0, NEG_INF).astype(F32)
    bias_first = jnp.where(band & (kj >= ATT_BLK), 0.0, NEG_INF).astype(F32)
    lane = lax.broadcasted_iota(jnp.int32, (ATT_BLK, LANES), 1)
    head0 = lane < HD

    for p, d in enumerate(DILATIONS):
        n_blk = SEQ // (d * ATT_BLK)

        def unit(idx, carry, p=p, d=d, n_blk=n_blk):
            k_prev, v_prev = carry
            c = idx // n_blk
            b = idx % n_blk
            own = c + d * ATT_BLK * b
            if d == 1:
                rows = pl.ds(pl.multiple_of(own, ATT_BLK), ATT_BLK)
            else:
                rows = pl.ds(own, ATT_BLK, stride=d)
            qb = q2[rows, :] * (HD ** -0.5 * LOG2_E)
            k_own = k2[rows, :].astype(BF16)
            v_own = v2[rows, :].astype(BF16)
            kb = jnp.concatenate([k_prev, k_own], axis=0)
            vb = jnp.concatenate([v_prev, v_own], axis=0)
            bias = jnp.where(b == 0, bias_first, bias_full)
            outs, mxs, dens = [], [], []
            for hh in range(2):
                sel = head0 if hh == 0 else jnp.logical_not(head0)
                qm = jnp.where(sel, qb, 0.0).astype(BF16)
                s = lax.dot_general(qm, kb, (((1,), (1,)), ((), ())),
                                    preferred_element_type=F32) + bias
                mx = jnp.max(s, axis=-1, keepdims=True)
                e = jnp.exp2(s - mx)
                dens.append(jnp.sum(e, axis=-1, keepdims=True))
                mxs.append(mx)
                outs.append(_dot(e.astype(BF16), vb))
            acc_ref[p, rows, :] = jnp.where(head0, outs[0], outs[1])
            max_ref[p, rows, :] = jnp.where(head0, mxs[0], mxs[1])
            den_ref[p, rows, :] = jnp.where(head0, dens[0], dens[1])
            return k_own, v_own

        zeros = jnp.zeros((ATT_BLK, LANES), BF16)
        lax.fori_loop(0, SEQ // ATT_BLK, unit, (zeros, zeros), unroll=ATT_UNROLL)

    def mix(t, carry):
        rows = pl.ds(pl.multiple_of(t * TM, TM), TM)
        m0, m1, m2 = max_ref[0, rows, :], max_ref[1, rows, :], max_ref[2, rows, :]
        mx = jnp.maximum(jnp.maximum(m0, m1), m2)
        e0, e1, e2 = jnp.exp2(m0 - mx), jnp.exp2(m1 - mx), jnp.exp2(m2 - mx)
        num = e0 * acc_ref[0, rows, :] + e1 * acc_ref[1, rows, :] + e2 * acc_ref[2, rows, :]
        den = e0 * den_ref[0, rows, :] + e1 * den_ref[1, rows, :] + e2 * den_ref[2, rows, :]
        o_ref[rows, :] = (num / den).astype(BF16)
        return carry

    lax.fori_loop(0, SEQ // TM, mix, 0)


def _attn(q, k, v):
    n_tok = q.shape[1]
    spec = pl.BlockSpec((1, SEQ, LANES), lambda n, s: (s, n, 0))
    return pl.pallas_call(
        _attn_kernel,
        grid=(BATCH, N_SLABS),
        in_specs=[spec, spec, spec],
        out_specs=pl.BlockSpec((SEQ, LANES), lambda n, s: (n, s)),
        out_shape=jax.ShapeDtypeStruct((n_tok, W_B), BF16),
        scratch_shapes=[pltpu.VMEM((len(DILATIONS), SEQ, LANES), F32)] * 3,
        compiler_params=_cparams("arbitrary", "arbitrary"),
        name="dil_attn",
    )(q, k, v)


S_ATT_HEADS = 4
S_ATT_PHASES = N_HEADS // S_ATT_HEADS
FF_SPLIT = 6


def _ffn_kernel(*refs, pre, final, att):
    refs = list(refs)
    x_ref, mod_ref = refs[0:2]
    del refs[0:2]
    if pre:
        a_ref, b_ref, wo_ref = refs[0:3]
        del refs[0:3]
    wup_ref, cw_ref, cb_ref, wdn_ref, fg_ref = refs[0:5]
    del refs[0:5]
    if att:
        (qs_ref, kn_ref, vn_ref, kt_hbm, vt_hbm, o_ref, st_ref, bs_ref, kto_hbm, vto_hbm,
         carry_ref, act_ref, kin_ref, vin_ref, kout_ref, vout_ref, in_sem, out_sem) = refs
    else:
        o_ref, st_ref, carry_ref, act_ref = refs
    i = pl.program_id(0)
    last = pl.num_programs(0) - 1

    @pl.when(i % TILES_PER_SEQ == 0)
    def _():
        carry_ref[...] = jnp.zeros_like(carry_ref)

    def copies_in(seq, ph):
        hs = pl.ds(ph * S_ATT_HEADS, S_ATT_HEADS)
        return (pltpu.make_async_copy(kt_hbm.at[seq, hs], kin_ref.at[ph], in_sem.at[ph, 0]),
                pltpu.make_async_copy(vt_hbm.at[seq, hs], vin_ref.at[ph], in_sem.at[ph, 1]))

    def copies_out(seq, ph):
        hs = pl.ds(ph * S_ATT_HEADS, S_ATT_HEADS)
        return (pltpu.make_async_copy(kout_ref.at[ph], kto_hbm.at[seq, hs], out_sem.at[ph, 0]),
                pltpu.make_async_copy(vout_ref.at[ph], vto_hbm.at[seq, hs], out_sem.at[ph, 1]))

    def start(cps):
        for c in cps:
            c.start()

    def wait(cps):
        for c in cps:
            c.wait()

    def window_phase(ph):
        lanes = slice(ph * S_ATT_HEADS * HD, (ph + 1) * S_ATT_HEADS * HD)
        row = pl.ds(i, 1)
        o_row, kt_new, vt_new = _s_attn_heads(qs_ref[row, lanes], kn_ref[row, lanes],
                                              vn_ref[row, lanes], kin_ref[ph], vin_ref[ph])
        bs_ref[row, lanes] = o_row
        kout_ref[ph] = kt_new
        vout_ref[ph] = vt_new

    if att:
        @pl.when(i == 0)
        def _():
            kout_ref[1] = jnp.zeros(kout_ref.shape[1:], F32)
            vout_ref[1] = jnp.zeros(vout_ref.shape[1:], F32)
            start(copies_in(0, 0))
            start(copies_out(0, 1))

        wait(copies_in(i, 0))
        start(copies_in(i, 1))

        @pl.when(i > 0)
        def _():
            wait(copies_out(i - 1, 0))

        window_phase(0)

    m = mod_ref[0]
    cw = cw_ref[...]
    cb = cb_ref[...]
    x = x_ref[...]
    if pre:
        x = x + m[2:3] * (_dot(a_ref[...], wo_ref[0:W_A, :]) + _dot(b_ref[...], wo_ref[W_A:W_A + W_B, :]))
    h = _rms_mod(x, m[4:5], m[3:4]).astype(BF16)
    for j in range(N_FF_CHUNKS):
        if att and j == FF_SPLIT:
            start(copies_out(i, 0))
            wait(copies_in(i, 1))
            start(copies_in(jnp.minimum(i + 1, last), 0))
            wait(copies_out(jnp.maximum(i - 1, 0), 1))
            window_phase(1)
        conv = []
        for half in range(2):
            c0 = half * D_FF + j * FF_CHUNK
            cols = slice(c0, c0 + FF_CHUNK)
            up = _dot(h, wup_ref[:, cols])
            tail = carry_ref[:, cols]
            carry_ref[:, cols] = up[TM - SUBLANES:, :]
            conv.append(cb[:, cols] + cw[0:1, cols] * _shift_rows(up, tail, 2)
                        + cw[1:2, cols] * _shift_rows(up, tail, 1) + cw[2:3, cols] * up)
        act = conv[0] * jax.nn.sigmoid(conv[0]) * conv[1]
        act_ref[:, j * FF_CHUNK:(j + 1) * FF_CHUNK] = act.astype(BF16)
    st_ref[0] = carry_ref[...]
    y = x + m[5:6] * _dot(act_ref[...], wdn_ref[...])
    if final:
        y = _rmsnorm(y) * fg_ref[...]
    o_ref[...] = y

    if att:
        start(copies_out(i, 1))

        @pl.when(i == last)
        def _():
            wait(copies_out(i, 0))
            wait(copies_out(i, 1))
            wait(copies_in(i, 0))


def _ffn(x2d, mod, layer, w_up, conv_w, conv_b, w_down, final_g, final, pre=None, att=None):
    n_tok = x2d.shape[0]
    f2 = 2 * D_FF
    row_spec = pl.BlockSpec((TM, D_MODEL), lambda i: (i, 0))
    mod_spec = pl.BlockSpec((1, 6, D_MODEL), lambda i: (i // TILES_PER_SEQ, 0, 0))
    args, specs = [x2d, mod], [row_spec, mod_spec]
    if pre is not None:
        args += list(pre)
        specs += [pl.BlockSpec((TM, W_A), lambda i: (i, 0)),
                  pl.BlockSpec((TM, W_B), lambda i: (i, 0)),
                  _const_spec((W_A + W_B, D_MODEL))]
    args += [w_up, conv_w, conv_b, w_down, final_g]
    specs += [_layer_spec((D_MODEL, f2), layer), _layer_spec((3, f2), layer),
              _layer_spec((1, f2), layer), _layer_spec((D_FF, D_MODEL), layer),
              _const_spec((1, D_MODEL))]
    out_specs = [row_spec, pl.BlockSpec((1, SUBLANES, f2), lambda i: (i // TILES_PER_SEQ, 0, 0))]
    out_shape = [jax.ShapeDtypeStruct((n_tok, D_MODEL), F32),
                 jax.ShapeDtypeStruct((BATCH, SUBLANES, f2), F32)]
    scratch = [pltpu.VMEM((SUBLANES, f2), F32), pltpu.VMEM((TM, D_FF), BF16)]
    if att is not None:
        q_s, kt = att[0], att[3]
        assert q_s.shape[0] == n_tok // TM, "one sample sequence per grid step"
        any_spec = pl.BlockSpec(memory_space=pl.ANY)
        args += list(att)
        specs += [_const_spec(q_s.shape)] * 3 + [any_spec, any_spec]
        out_specs += [pl.BlockSpec(q_s.shape, lambda i: (0, 0)), any_spec, any_spec]
        out_shape += [jax.ShapeDtypeStruct(q_s.shape, F32),
                      jax.ShapeDtypeStruct(kt.shape, F32), jax.ShapeDtypeStruct(kt.shape, F32)]
        win_buf = pltpu.VMEM((S_ATT_PHASES, S_ATT_HEADS, HD, WIN), F32)
        scratch += [win_buf] * 4 + [pltpu.SemaphoreType.DMA((S_ATT_PHASES, 2))] * 2
    return pl.pallas_call(
        functools.partial(_ffn_kernel, pre=pre is not None, final=final, att=att is not None),
        grid=(n_tok // TM,),
        in_specs=specs,
        out_specs=out_specs,
        out_shape=out_shape,
        scratch_shapes=scratch,
        compiler_params=_cparams("arbitrary"),
        name="conv_ffn_final" if final else "conv_ffn",
    )(*args)


N_GROUPS = TM // SUBLANES


def _rg_gates(xc, wa_ref, ba, wx_ref, bx, lam):
    xcb = xc.astype(BF16)
    r_parts, i_parts = [], []
    for g in range(RG_HEADS):
        blk = xcb[:, g * RG_BLOCK:(g + 1) * RG_BLOCK]
        r_parts.append(_dot(blk, wa_ref[g]))
        i_parts.append(_dot(blk, wx_ref[g]))
    r = jax.nn.sigmoid(jnp.concatenate(r_parts, axis=-1) + ba)
    ig = jax.nn.sigmoid(jnp.concatenate(i_parts, axis=-1) + bx)
    log_a = r * ((-RG_C) * jax.nn.softplus(-lam))
    a = jnp.exp(log_a)
    b = jnp.sqrt(1.0 - a * a) * (ig * xc)
    return a, b


def _rglru_kernel(x_ref, mod_ref, win_ref, cw_ref, cb_ref, wa_ref, ba_ref, wx_ref, bx_ref,
                  lam_ref, wout_ref, o_ref, cst_ref, hst_ref,
                  xtail_ref, hcar_ref, a_ref, b_ref, hs_ref):
    i = pl.program_id(0)

    @pl.when(i % TILES_PER_SEQ == 0)
    def _():
        xtail_ref[...] = jnp.zeros_like(xtail_ref)
        hcar_ref[...] = jnp.zeros_like(hcar_ref)

    x = x_ref[...]
    m = mod_ref[0]
    h = _rms_mod(x, m[1:2], m[0:1]).astype(BF16)
    gate = _dot(h, win_ref[:, 0:D_RNN])
    xr = _dot(h, win_ref[:, D_RNN:2 * D_RNN])
    cw = cw_ref[...]
    tail = xtail_ref[...]
    xc = (cb_ref[...] + cw[3:4] * xr + cw[2:3] * _shift_rows(xr, tail, 1)
          + cw[1:2] * _shift_rows(xr, tail, 2) + cw[0:1] * _shift_rows(xr, tail, 3))
    last = xr[TM - SUBLANES:, :]
    xtail_ref[...] = last
    cst_ref[0] = last

    a, b = _rg_gates(xc, wa_ref, ba_ref[...], wx_ref, bx_ref[...], lam_ref[...])

    a3 = a.reshape(N_GROUPS, SUBLANES, D_RNN)
    b3 = b.reshape(N_GROUPS, SUBLANES, D_RNN)
    r_id = lax.broadcasted_iota(jnp.int32, (N_GROUPS, SUBLANES, D_RNN), 1)
    for s in (1, 2, 4):
        keep = r_id >= s
        a_sh = jnp.where(keep, pltpu.roll(a3, s, 1), 1.0)
        b_sh = jnp.where(keep, pltpu.roll(b3, s, 1), 0.0)
        b3 = a3 * b_sh + b3
        a3 = a3 * a_sh
    a_ref[...] = a3
    b_ref[...] = b3

    def step(g, hc):
        hs = a_ref[g] * hc + b_ref[g]
        hs_ref[g] = hs
        return hs[SUBLANES - 1:SUBLANES, :]

    hc = lax.fori_loop(0, N_GROUPS, step, hcar_ref[...])
    hcar_ref[...] = hc
    hst_ref[0] = hc

    y = (jax.nn.gelu(gate) * hs_ref[...].reshape(TM, D_RNN)).astype(BF16)
    o_ref[...] = x + m[2:3] * _dot(y, wout_ref[...])


def _rglru(x2d, mod, w_in, conv_w, conv_b, w_a, b_a, w_x, b_x, lam, w_out):
    n_tok = x2d.shape[0]
    return pl.pallas_call(
        _rglru_kernel,
        grid=(n_tok // TM,),
        in_specs=[
            pl.BlockSpec((TM, D_MODEL), lambda i: (i, 0)),
            pl.BlockSpec((1, 6, D_MODEL), lambda i: (i // TILES_PER_SEQ, 0, 0)),
            _const_spec((D_MODEL, 2 * D_RNN)),
            _const_spec((4, D_RNN)),
            _const_spec((1, D_RNN)),
            _const_spec((RG_HEADS, RG_BLOCK, RG_BLOCK)),
            _const_spec((1, D_RNN)),
            _const_spec((RG_HEADS, RG_BLOCK, RG_BLOCK)),
            _const_spec((1, D_RNN)),
            _const_spec((1, D_RNN)),
            _const_spec((D_RNN, D_MODEL)),
        ],
        out_specs=[
            pl.BlockSpec((TM, D_MODEL), lambda i: (i, 0)),
            pl.BlockSpec((1, SUBLANES, D_RNN), lambda i: (i // TILES_PER_SEQ, 0, 0)),
            pl.BlockSpec((1, 1, D_RNN), lambda i: (i // TILES_PER_SEQ, 0, 0)),
        ],
        out_shape=[
            jax.ShapeDtypeStruct((n_tok, D_MODEL), F32),
            jax.ShapeDtypeStruct((BATCH, SUBLANES, D_RNN), F32),
            jax.ShapeDtypeStruct((BATCH, 1, D_RNN), F32),
        ],
        scratch_shapes=[
            pltpu.VMEM((SUBLANES, D_RNN), F32),
            pltpu.VMEM((1, D_RNN), F32),
            pltpu.VMEM((N_GROUPS, SUBLANES, D_RNN), F32),
            pltpu.VMEM((N_GROUPS, SUBLANES, D_RNN), F32),
            pltpu.VMEM((N_GROUPS, SUBLANES, D_RNN), F32),
        ],
        compiler_params=_cparams("arbitrary"),
        name="rglru",
    )(x2d, mod, w_in, conv_w, conv_b, w_a, b_a, w_x, b_x, lam, w_out)


def _s_even_in_kernel(x_ref, mod_ref, w_ref, lng_ref, lnb_ref, ws0_ref, bs0_ref,
                      va_ref, a_ref, q_ref, k_ref, v_ref):
    mod = mod_ref[...]
    h = _rms_mod(x_ref[...], mod[:, D_MODEL:2 * D_MODEL], mod[:, 0:D_MODEL]).astype(BF16)
    u = jax.nn.gelu(_dot(h, w_ref[:, 0:W_A]))
    va = _layernorm(jax.nn.gelu(_dot(h, w_ref[:, W_A:2 * W_A])), lng_ref[...], lnb_ref[...])
    va_ref[...] = va
    a_ref[...] = (u * (ws0_ref[...] * va + bs0_ref[...])).astype(BF16)
    base = 2 * W_A
    q_ref[...] = _dot(h, w_ref[:, base:base + W_B])
    k_ref[...] = _dot(h, w_ref[:, base + W_B:base + 2 * W_B])
    v_ref[...] = _dot(h, w_ref[:, base + 2 * W_B:base + 3 * W_B])


def _s_even_in(x, mod, w_in, ln_g, ln_b, ws0, bs0):
    n = x.shape[0]
    f = jax.ShapeDtypeStruct((n, W_B), F32)
    return pl.pallas_call(
        _s_even_in_kernel,
        out_shape=[f, jax.ShapeDtypeStruct((n, W_A), BF16), f, f, f],
        compiler_params=pltpu.CompilerParams(vmem_limit_bytes=VMEM_LIMIT),
        name="s_even_in",
    )(x, mod, w_in, ln_g, ln_b, ws0, bs0)


def _s_attn_heads(qrow, knrow, vnrow, kt3, vt3):
    nh = kt3.shape[0]
    n_hd = nh * HD
    qrow = qrow * (HD ** -0.5)
    hrow = lax.broadcasted_iota(jnp.int32, (SUBLANES, n_hd), 0)
    hcol = lax.broadcasted_iota(jnp.int32, (SUBLANES, n_hd), 1) // HD
    own = hrow == hcol
    qbd = jnp.where(own, jnp.broadcast_to(qrow, (SUBLANES, n_hd)), 0.0)
    kt = kt3.reshape(n_hd, WIN)
    vt = vt3.reshape(n_hd, WIN)
    s = _dot(qbd.astype(BF16), kt.astype(BF16))
    s_new = jnp.sum(qbd * knrow, axis=-1, keepdims=True)
    t = lax.broadcasted_iota(jnp.int32, (SUBLANES, WIN), 1)
    dist = WIN - t
    vtb = vt.astype(BF16)
    outs, lses = [], []
    for d in DILATIONS:
        valid = ((dist & (d - 1)) == 0) & (dist <= N_BACK * d)
        sm = jnp.where(valid, s, NEG_INF)
        mx = jnp.maximum(jnp.max(sm, axis=-1, keepdims=True), s_new)
        e = jnp.where(valid, jnp.exp(sm - mx), 0.0)
        e_new = jnp.exp(s_new - mx)
        den = jnp.sum(e, axis=-1, keepdims=True) + e_new
        o = lax.dot_general(e.astype(BF16), vtb, (((1,), (1,)), ((), ())),
                            preferred_element_type=F32)
        outs.append((o + e_new * vnrow) / den)
        lses.append(mx + jnp.log(den))
    mx = jnp.maximum(jnp.maximum(lses[0], lses[1]), lses[2])
    ws = [jnp.exp(l - mx) for l in lses]
    num = ws[0] * outs[0] + ws[1] * outs[1] + ws[2] * outs[2]
    mixed = num / (ws[0] + ws[1] + ws[2])
    o_row = jnp.sum(jnp.where(own, mixed, 0.0), axis=0, keepdims=True)

    r2 = lax.broadcasted_iota(jnp.int32, (n_hd, n_hd), 0)
    c2 = lax.broadcasted_iota(jnp.int32, (n_hd, n_hd), 1)
    diag = r2 == c2
    kcol = jnp.sum(jnp.where(diag, jnp.broadcast_to(knrow, (n_hd, n_hd)), 0.0), axis=-1, keepdims=True)
    vcol = jnp.sum(jnp.where(diag, jnp.broadcast_to(vnrow, (n_hd, n_hd)), 0.0), axis=-1, keepdims=True)
    tt = lax.broadcasted_iota(jnp.int32, (n_hd, WIN), 1)
    is_last = tt == WIN - 1
    kt_new = jnp.where(is_last, kcol, pltpu.roll(kt, WIN - 1, 1)).reshape(nh, HD, WIN)
    vt_new = jnp.where(is_last, vcol, pltpu.roll(vt, WIN - 1, 1)).reshape(nh, HD, WIN)
    return o_row, kt_new, vt_new


def _s_even_out_kernel(x_ref, mod_ref, a_ref, b_ref, w_ref, o_ref):
    g1 = mod_ref[:, 2 * D_MODEL:3 * D_MODEL]
    mo = _dot(a_ref[...], w_ref[0:W_A, :]) + _dot(b_ref[...].astype(BF16), w_ref[W_A:W_A + W_B, :])
    o_ref[...] = x_ref[...] + g1 * mo


def _s_even_out(x, mod, a_out, b_out, w_out):
    return pl.pallas_call(
        _s_even_out_kernel,
        out_shape=jax.ShapeDtypeStruct(x.shape, F32),
        compiler_params=pltpu.CompilerParams(vmem_limit_bytes=VMEM_LIMIT),
        name="s_even_out",
    )(x, mod, a_out, b_out, w_out)


def _s_ffn_kernel(x_ref, mod_ref, p2_ref, p1_ref, wup_ref, cw_ref, cb_ref, wdn_ref, fg_ref,
                  o_ref, up_ref, *, final):
    x = x_ref[...]
    mod = mod_ref[...]
    h = _rms_mod(x, mod[:, 4 * D_MODEL:5 * D_MODEL], mod[:, 3 * D_MODEL:4 * D_MODEL]).astype(BF16)
    up = _dot(h, wup_ref[...])
    up_ref[...] = up
    cw = cw_ref[...]
    conv = cb_ref[...] + cw[0:1] * p2_ref[...] + cw[1:2] * p1_ref[...] + cw[2:3] * up
    ca, cg = conv[:, 0:D_FF], conv[:, D_FF:2 * D_FF]
    act = (ca * jax.nn.sigmoid(ca) * cg).astype(BF16)
    y = x + mod[:, 5 * D_MODEL:6 * D_MODEL] * _dot(act, wdn_ref[...])
    if final:
        y = _rmsnorm(y) * fg_ref[...]
    o_ref[...] = y


def _s_ffn(x, mod, p2, p1, layer, w_up, conv_w, conv_b, w_down, final_g, final):
    n = x.shape[0]
    f2 = 2 * D_FF
    return pl.pallas_call(
        functools.partial(_s_ffn_kernel, final=final),
        grid=(1,),
        in_specs=[_const_spec(x.shape), _const_spec(mod.shape), _const_spec(p2.shape),
                  _const_spec(p1.shape), _layer_spec((D_MODEL, f2), layer),
                  _layer_spec((3, f2), layer), _layer_spec((1, f2), layer),
                  _layer_spec((D_FF, D_MODEL), layer), _const_spec((1, D_MODEL))],
        out_specs=[pl.BlockSpec(x.shape, lambda i: (0, 0)), pl.BlockSpec((n, f2), lambda i: (0, 0))],
        out_shape=[jax.ShapeDtypeStruct(x.shape, F32), jax.ShapeDtypeStruct((n, f2), F32)],
        compiler_params=_cparams("arbitrary"),
        name="s_conv_ffn_final" if final else "s_conv_ffn",
    )(x, mod, p2, p1, w_up, conv_w, conv_b, w_down, final_g)


def _s_rglru_kernel(x_ref, mod_ref, c0_ref, c1_ref, c2_ref, h0_ref, win_ref, cw_ref, cb_ref,
                    wa_ref, ba_ref, wx_ref, bx_ref, lam_ref, wout_ref, o_ref, xr_ref, hn_ref):
    x = x_ref[...]
    mod = mod_ref[...]
    h = _rms_mod(x, mod[:, D_MODEL:2 * D_MODEL], mod[:, 0:D_MODEL]).astype(BF16)
    gate = _dot(h, win_ref[:, 0:D_RNN])
    xr = _dot(h, win_ref[:, D_RNN:2 * D_RNN])
    xr_ref[...] = xr
    cw = cw_ref[...]
    xc = (cb_ref[...] + cw[0:1] * c0_ref[...] + cw[1:2] * c1_ref[...] + cw[2:3] * c2_ref[...]
          + cw[3:4] * xr)
    a, b = _rg_gates(xc, wa_ref, ba_ref[...], wx_ref, bx_ref[...], lam_ref[...])
    hn = a * h0_ref[...] + b
    hn_ref[...] = hn
    y = (jax.nn.gelu(gate) * hn).astype(BF16)
    o_ref[...] = x + mod[:, 2 * D_MODEL:3 * D_MODEL] * _dot(y, wout_ref[...])


def _s_rglru(x, mod, c0, c1, c2, h0, w_in, conv_w, conv_b, w_a, b_a, w_x, b_x, lam, w_out):
    f = jax.ShapeDtypeStruct(x.shape, F32)
    return pl.pallas_call(
        _s_rglru_kernel,
        out_shape=[f, f, f],
        compiler_params=pltpu.CompilerParams(vmem_limit_bytes=VMEM_LIMIT),
        name="s_rglru",
    )(x, mod, c0, c1, c2, h0, w_in, conv_w, conv_b, w_a, b_a, w_x, b_x, lam, w_out)


def kernel(x_prompt, x_sample, cache_win_k, cache_win_v, state_rglru_conv, state_rglru_h, state_ffn_conv, c_prompt, c_sample, w_ada, b_ada, w_in_even, ln_v_g, ln_v_b, w_sgu, b_sgu, w_out_even, w_in_odd, rg_conv_w, rg_conv_b, rg_w_a, rg_b_a, rg_w_x, rg_b_x, rg_lambda, w_out_odd, ffn_w_up, ffn_conv_w, ffn_conv_b, ffn_w_down, final_g):
    w_in_even_b = w_in_even[0].astype(BF16)
    w_out_even_b = w_out_even[0].astype(BF16)
    w_in_odd_b = w_in_odd[0].astype(BF16)
    w_out_odd_b = w_out_odd[0].astype(BF16)
    rg_w_a_b = rg_w_a[0].astype(BF16)
    rg_w_x_b = rg_w_x[0].astype(BF16)
    w_up_b = ffn_w_up.astype(BF16)
    w_down_b = ffn_w_down.astype(BF16)
    conv_b3 = ffn_conv_b.reshape(2, 1, 2 * D_FF)
    final_g2 = final_g.reshape(1, D_MODEL)

    pad = jnp.zeros((ADA_ROWS - BATCH - DEC_BATCH, D_MODEL), F32)
    mod = _ada(jnp.concatenate([c_prompt, c_sample, pad], axis=0), w_ada, b_ada)
    mod_p = mod[:, :BATCH].reshape(2, BATCH, 6, D_MODEL)
    mod_s = mod[:, BATCH:BATCH + DEC_BATCH]

    xs = x_sample.reshape(DEC_BATCH, D_MODEL)
    ws0 = jnp.repeat(w_sgu[0, :, 0, 0], A_GROUP).reshape(1, W_A)
    bs0 = jnp.repeat(b_sgu[0, :, 0], A_GROUP).reshape(1, W_A)
    va_s, a_s, q_s, k_s, v_s = _s_even_in(xs, mod_s[0], w_in_even_b, ln_v_g, ln_v_b, ws0, bs0)
    kt_c = cache_win_k[0].transpose(0, 2, 3, 1)
    vt_c = cache_win_v[0].transpose(0, 2, 3, 1)

    x = x_prompt.reshape(BATCH * SEQ, D_MODEL)
    a_out, q, k, v, kt_p, vt_p = _even_in(x, mod_p[0], w_in_even_b, ln_v_g, ln_v_b, w_sgu[0],
                                          b_sgu[0].T)
    b_out = _attn(q, k, v)
    x, ffn_st0 = _ffn(x, mod_p[0], 0, w_up_b, ffn_conv_w, conv_b3, w_down_b, final_g2, False,
                      pre=(a_out, b_out, w_out_even_b))
    x, rg_cst, rg_hst = _rglru(x, mod_p[1], w_in_odd_b, rg_conv_w[0], rg_conv_b, rg_w_a_b, rg_b_a,
                               rg_w_x_b, rg_b_x, rg_lambda, w_out_odd_b)
    y_p, ffn_st1, b_s, kt_n, vt_n = _ffn(x, mod_p[1], 1, w_up_b, ffn_conv_w, conv_b3, w_down_b,
                                         final_g2, True, att=(q_s, k_s, v_s, kt_c, vt_c))

    y_prompt = y_p.reshape(BATCH, SEQ, D_MODEL)
    to_win = lambda t: t.reshape(1, BATCH, N_HEADS, HD, WIN).transpose(0, 1, 4, 2, 3)
    win_k_prompt = to_win(kt_p)
    win_v_prompt = to_win(vt_p)
    rglru_conv_prompt = rg_cst[None, :, SUBLANES - 3:, :]
    rglru_h_prompt = rg_hst.reshape(1, BATCH, D_RNN)
    ffn_conv_prompt = jnp.stack([ffn_st0[:, SUBLANES - 2:, :], ffn_st1[:, SUBLANES - 2:, :]])

    xs = _s_even_out(xs, mod_s[0], a_s, b_s, w_out_even_b)
    st0 = state_ffn_conv[0]
    xs, up0 = _s_ffn(xs, mod_s[0], st0[:, 0], st0[:, 1], 0, w_up_b, ffn_conv_w, conv_b3, w_down_b,
                     final_g2, False)
    cst = state_rglru_conv[0]
    xs, xr_s, hn_s = _s_rglru(xs, mod_s[1], cst[:, 0], cst[:, 1], cst[:, 2], state_rglru_h[0],
                              w_in_odd_b, rg_conv_w[0], rg_conv_b, rg_w_a_b, rg_b_a, rg_w_x_b,
                              rg_b_x, rg_lambda, w_out_odd_b)
    st1 = state_ffn_conv[1]
    ys, up1 = _s_ffn(xs, mod_s[1], st1[:, 0], st1[:, 1], 1, w_up_b, ffn_conv_w, conv_b3, w_down_b,
                     final_g2, True)

    y_sample = ys.reshape(DEC_BATCH, 1, D_MODEL)
    chunk_v_sample = va_s.reshape(1, DEC_BATCH, 1, W_A)
    win_k_sample = kt_n.transpose(0, 3, 1, 2)[None]
    win_v_sample = vt_n.transpose(0, 3, 1, 2)[None]
    rglru_conv_sample = jnp.stack([cst[:, 1], cst[:, 2], xr_s], axis=1)[None]
    rglru_h_sample = hn_s[None]
    ffn_conv_sample = jnp.stack([jnp.stack([st0[:, 1], up0], axis=1),
                                 jnp.stack([st1[:, 1], up1], axis=1)])

    return (y_prompt, y_sample, win_k_prompt, win_v_prompt, rglru_conv_prompt, rglru_h_prompt,
            ffn_conv_prompt, chunk_v_sample, win_k_sample, win_v_sample, rglru_conv_sample,
            rglru_h_sample, ffn_conv_sample)
```

```python
import functools

import jax
import jax.numpy as jnp
from jax import lax
from jax.experimental import pallas as pl
from jax.experimental.pallas import tpu as pltpu

F32 = jnp.float32
BF16 = jnp.bfloat16

D_MODEL = 1024
BATCH = 4
SEQ = 4096
DEC_BATCH = 32
W_A = 512
A_GROUP = 128
G_A = 4
CHUNK = 128
W_B = 512
HD = 64
N_HEADS = 8
DILATIONS = (1, 4, 16)
N_BACK = 128
WIN = 2048
N_IN_EVEN = 2 * W_A + 3 * W_B
D_RNN = 1024
RG_BLOCK = 128
RG_HEADS = 8
RG_C = 8.0
D_FF = 2816
EPS = 1e-6
NEG_INF = -1e30
LOG2_E = 1.4426950408889634
LN_2 = 0.6931471805599453

LANES = 128
SUBLANES = 8
TM = 512
TILES_PER_SEQ = SEQ // TM
N_TILES = BATCH * TILES_PER_SEQ
FF_CHUNK = 256
N_FF_CHUNKS = D_FF // FF_CHUNK
N_SLABS = W_B // LANES
VMEM_LIMIT = 56 * 1024 * 1024


def _cparams(*sem):
    return pltpu.CompilerParams(dimension_semantics=sem, vmem_limit_bytes=VMEM_LIMIT)


def _const_spec(shape):
    nd = len(shape)
    return pl.BlockSpec(shape, lambda *_: (0,) * nd, pipeline_mode=pl.Buffered(1))


def _layer_spec(shape, layer):
    nd = len(shape)
    return pl.BlockSpec((None,) + tuple(shape), lambda *_: (layer,) + (0,) * nd,
                        pipeline_mode=pl.Buffered(1))


def _shift_rows(cur, tail, k):
    rolled = pltpu.roll(cur, k, 0)
    r = lax.broadcasted_iota(jnp.int32, (SUBLANES, cur.shape[1]), 0)
    head = jnp.where(r < k, pltpu.roll(tail, k, 0), rolled[0:SUBLANES])
    return jnp.concatenate([head, rolled[SUBLANES:]], axis=0)


def _rms_mod(x, scale, shift):
    xn = x * lax.rsqrt(jnp.mean(x * x, axis=-1, keepdims=True) + EPS)
    return xn * (1.0 + scale) + shift


def _rmsnorm(x):
    return x * lax.rsqrt(jnp.mean(x * x, axis=-1, keepdims=True) + EPS)


def _layernorm(x, g, b):
    mu = jnp.mean(x, axis=-1, keepdims=True)
    xc = x - mu
    var = jnp.mean(xc * xc, axis=-1, keepdims=True)
    return xc * lax.rsqrt(var + EPS) * g + b


def _dot(a, b):
    return jnp.dot(a, b, preferred_element_type=F32)


ADA_ROWS = 40
ADA_TN = 1024


def _ada_kernel(c_ref, w_ref, b_ref, o_ref):
    c = c_ref[...]
    s = (c * jax.nn.sigmoid(c)).astype(BF16)
    o_ref[0] = _dot(s, w_ref[0].astype(BF16)) + b_ref[0]


def _ada(c_all, w_ada, b_ada):
    depth = w_ada.shape[0]
    n_out = w_ada.shape[2]
    return pl.pallas_call(
        _ada_kernel,
        grid=(depth, n_out // ADA_TN),
        in_specs=[
            pl.BlockSpec((ADA_ROWS, D_MODEL), lambda l, j: (0, 0)),
            pl.BlockSpec((1, D_MODEL, ADA_TN), lambda l, j: (l, 0, j)),
            pl.BlockSpec((1, 1, ADA_TN), lambda l, j: (l, 0, j)),
        ],
        out_specs=pl.BlockSpec((1, ADA_ROWS, ADA_TN), lambda l, j: (l, 0, j)),
        out_shape=jax.ShapeDtypeStruct((depth, ADA_ROWS, n_out), F32),
        compiler_params=_cparams("arbitrary", "arbitrary"),
        name="ada_mod",
    )(c_all, w_ada, b_ada.reshape(depth, 1, n_out))


def _even_in_kernel(x_ref, mod_ref, w_ref, lng_ref, lnb_ref, ws_ref, bst_ref,
                    a_ref, q_ref, k_ref, v_ref, kt_ref, vt_ref):
    i = pl.program_id(0)
    m = mod_ref[0]
    h = _rms_mod(x_ref[...], m[1:2], m[0:1]).astype(BF16)
    u = jax.nn.gelu(_dot(h, w_ref[:, 0:W_A]))
    va = _layernorm(jax.nn.gelu(_dot(h, w_ref[:, W_A:2 * W_A])), lng_ref[...], lnb_ref[...])
    vab = va.astype(BF16)
    row = lax.broadcasted_iota(jnp.int32, (CHUNK, CHUNK), 0)
    col = lax.broadcasted_iota(jnp.int32, (CHUNK, CHUNK), 1)
    causal = col <= row
    bst = bst_ref[...]
    for g in range(G_A):
        wg = jnp.where(causal, ws_ref[g], 0.0).astype(BF16)
        bias = bst[:, g:g + 1]
        lo, hi = g * A_GROUP, (g + 1) * A_GROUP
        for c in range(TM // CHUNK):
            r0, r1 = c * CHUNK, (c + 1) * CHUNK
            mix = _dot(wg, vab[r0:r1, lo:hi]) + bias
            a_ref[r0:r1, lo:hi] = (u[r0:r1, lo:hi] * mix).astype(BF16)
    base = 2 * W_A
    q = _dot(h, w_ref[:, base:base + W_B])
    k = _dot(h, w_ref[:, base + W_B:base + 2 * W_B])
    v = _dot(h, w_ref[:, base + 2 * W_B:base + 3 * W_B])
    for s in range(N_SLABS):
        q_ref[s] = q[:, s * LANES:(s + 1) * LANES]
        k_ref[s] = k[:, s * LANES:(s + 1) * LANES]
        v_ref[s] = v[:, s * LANES:(s + 1) * LANES]

    @pl.when(i % TILES_PER_SEQ >= TILES_PER_SEQ - WIN // TM)
    def _():
        kt_ref[0] = k.T
        vt_ref[0] = v.T


def _even_in(x2d, mod, w_in, ln_g, ln_b, w_s, b_s_t):
    n_tok = x2d.shape[0]
    first_win_tile = TILES_PER_SEQ - WIN // TM
    qkv_shape = jax.ShapeDtypeStruct((N_SLABS, n_tok, LANES), F32)
    qkv_spec = pl.BlockSpec((N_SLABS, TM, LANES), lambda i: (0, i, 0))
    win_shape = jax.ShapeDtypeStruct((BATCH, W_B, WIN), F32)
    win_spec = pl.BlockSpec(
        (1, W_B, TM),
        lambda i: (i // TILES_PER_SEQ, 0, jnp.maximum(i % TILES_PER_SEQ - first_win_tile, 0)))
    return pl.pallas_call(
        _even_in_kernel,
        grid=(N_TILES,),
        in_specs=[
            pl.BlockSpec((TM, D_MODEL), lambda i: (i, 0)),
            pl.BlockSpec((1, 6, D_MODEL), lambda i: (i // TILES_PER_SEQ, 0, 0)),
            _const_spec((D_MODEL, N_IN_EVEN)),
            _const_spec((1, W_A)),
            _const_spec((1, W_A)),
            _const_spec((G_A, CHUNK, CHUNK)),
            _const_spec((CHUNK, G_A)),
        ],
        out_specs=[
            pl.BlockSpec((TM, W_A), lambda i: (i, 0)),
            qkv_spec, qkv_spec, qkv_spec, win_spec, win_spec,
        ],
        out_shape=[
            jax.ShapeDtypeStruct((n_tok, W_A), BF16),
            qkv_shape, qkv_shape, qkv_shape, win_shape, win_shape,
        ],
        compiler_params=_cparams("arbitrary"),
        name="even_in",
    )(x2d, mod, w_in, ln_g, ln_b, w_s, b_s_t)


ATT_BLK = 128
ATT_UNROLL = 16


def _attn_kernel(q_ref, k_ref, v_ref, o_ref, acc_ref, max_ref, den_ref, q4_ref, k4_ref, v4_ref):
    q2, k2, v2 = q_ref.at[0], k_ref.at[0], v_ref.at[0]
    seq4 = SEQ // 4
    for src, dst in ((q2, q4_ref), (k2, k4_ref), (v2, v4_ref)):
        for c0 in range(4):
            for r in range(0, seq4, 2 * ATT_BLK):
                dst[c0 * seq4 + r:c0 * seq4 + r + 2 * ATT_BLK, :] = (
                    src[pl.ds(c0 + 4 * r, 2 * ATT_BLK, stride=4), :])
    qi = lax.broadcasted_iota(jnp.int32, (ATT_BLK, 2 * ATT_BLK), 0)
    kj = lax.broadcasted_iota(jnp.int32, (ATT_BLK, 2 * ATT_BLK), 1)
    dist = qi + ATT_BLK - kj
    band = (dist >= 0) & (dist <= N_BACK)
    bias_full = jnp.where(band, 0.0, NEG_INF).astype(F32)
    bias_first = jnp.where(band & (kj >= ATT_BLK), 0.0, NEG_INF).astype(F32)
    lane = lax.broadcasted_iota(jnp.int32, (ATT_BLK, LANES), 1)
    head0 = lane < HD

    for p, d in enumerate(DILATIONS):
        n_blk = SEQ // (d * ATT_BLK)

        def unit(idx, carry, p=p, d=d, n_blk=n_blk):
            k_prev, v_prev = carry
            c = idx // n_blk
            b = idx % n_blk
            if d == 1:
                qs, ks, vs = q2, k2, v2
                rows = pl.ds(pl.multiple_of(ATT_BLK * b, ATT_BLK), ATT_BLK)
            elif d == 4:
                qs, ks, vs = q4_ref, k4_ref, v4_ref
                rows = pl.ds(pl.multiple_of(c * seq4 + ATT_BLK * b, ATT_BLK), ATT_BLK)
            else:
                qs, ks, vs = q4_ref, k4_ref, v4_ref
                rows = pl.ds((c % 4) * seq4 + c // 4 + 4 * ATT_BLK * b, ATT_BLK, stride=4)
            qb = qs[rows, :] * (HD ** -0.5 * LOG2_E)
            k_own = ks[rows, :].astype(BF16)
            v_own = vs[rows, :].astype(BF16)
            kb = jnp.concatenate([k_prev, k_own], axis=0)
            vb = jnp.concatenate([v_prev, v_own], axis=0)
            bias = jnp.where(b == 0, bias_first, bias_full)
            outs, mxs, dens = [], [], []
            for hh in range(2):
                sel = head0 if hh == 0 else jnp.logical_not(head0)
                qm = jnp.where(sel, qb, 0.0).astype(BF16)
                s = lax.dot_general(qm, kb, (((1,), (1,)), ((), ())),
                                    preferred_element_type=F32) + bias
                mx = jnp.max(s, axis=-1, keepdims=True)
                e = jnp.exp2(s - mx)
                dens.append(jnp.sum(e, axis=-1, keepdims=True))
                mxs.append(mx)
                outs.append(_dot(e.astype(BF16), vb))
            acc_ref[p, rows, :] = jnp.where(head0, outs[0], outs[1])
            max_ref[p, rows, :] = jnp.where(head0, mxs[0], mxs[1])
            den_ref[p, rows, :] = jnp.where(head0, dens[0], dens[1])
            return k_own, v_own

        zeros = jnp.zeros((ATT_BLK, LANES), BF16)
        lax.fori_loop(0, SEQ // ATT_BLK, unit, (zeros, zeros), unroll=ATT_UNROLL)

    def mix(t, carry):
        c0 = t // (seq4 // TM)
        r = (t % (seq4 // TM)) * TM
        rows4 = pl.ds(pl.multiple_of(c0 * seq4 + r, TM), TM)
        rows = pl.ds(c0 + 4 * r, TM, stride=4)
        m0, m1, m2 = max_ref[0, rows, :], max_ref[1, rows4, :], max_ref[2, rows4, :]
        mx = jnp.maximum(jnp.maximum(m0, m1), m2)
        e0, e1, e2 = jnp.exp2(m0 - mx), jnp.exp2(m1 - mx), jnp.exp2(m2 - mx)
        num = e0 * acc_ref[0, rows, :] + e1 * acc_ref[1, rows4, :] + e2 * acc_ref[2, rows4, :]
        den = e0 * den_ref[0, rows, :] + e1 * den_ref[1, rows4, :] + e2 * den_ref[2, rows4, :]
        acc_ref[0, rows, :] = num / den
        return carry

    lax.fori_loop(0, SEQ // TM, mix, 0)

    def emit(t, carry):
        rows = pl.ds(pl.multiple_of(t * TM, TM), TM)
        o_ref[rows, :] = acc_ref[0, rows, :].astype(BF16)
        return carry

    lax.fori_loop(0, SEQ // TM, emit, 0)


def _attn(q, k, v):
    n_tok = q.shape[1]
    spec = pl.BlockSpec((1, SEQ, LANES), lambda n, s: (s, n, 0))
    return pl.pallas_call(
        _attn_kernel,
        grid=(BATCH, N_SLABS),
        in_specs=[spec, spec, spec],
        out_specs=pl.BlockSpec((SEQ, LANES), lambda n, s: (n, s)),
        out_shape=jax.ShapeDtypeStruct((n_tok, W_B), BF16),
        scratch_shapes=([pltpu.VMEM((len(DILATIONS), SEQ, LANES), F32)] * 3
                        + [pltpu.VMEM((SEQ, LANES), F32)] * 3),
        compiler_params=_cparams("arbitrary", "arbitrary"),
        name="dil_attn",
    )(q, k, v)


S_ATT_HEADS = 4
S_ATT_PHASES = N_HEADS // S_ATT_HEADS
FF_SPLIT = 6


def _ffn_kernel(*refs, pre, final, att):
    refs = list(refs)
    x_ref, mod_ref = refs[0:2]
    del refs[0:2]
    if pre:
        a_ref, b_ref, wo_ref = refs[0:3]
        del refs[0:3]
    wup_ref, cw_ref, cb_ref, wdn_ref, fg_ref = refs[0:5]
    del refs[0:5]
    if att:
        (qs_ref, kn_ref, vn_ref, kt_hbm, vt_hbm, o_ref, st_ref, bs_ref, kto_hbm, vto_hbm,
         carry_ref, act_ref, kin_ref, vin_ref, kout_ref, vout_ref, in_sem, out_sem) = refs
    else:
        o_ref, st_ref, carry_ref, act_ref = refs
    i = pl.program_id(0)
    last = pl.num_programs(0) - 1

    @pl.when(i % TILES_PER_SEQ == 0)
    def _():
        carry_ref[...] = jnp.zeros_like(carry_ref)

    def copies_in(seq, ph):
        hs = pl.ds(ph * S_ATT_HEADS, S_ATT_HEADS)
        return (pltpu.make_async_copy(kt_hbm.at[seq, hs], kin_ref.at[ph], in_sem.at[ph, 0]),
                pltpu.make_async_copy(vt_hbm.at[seq, hs], vin_ref.at[ph], in_sem.at[ph, 1]))

    def copies_out(seq, ph):
        hs = pl.ds(ph * S_ATT_HEADS, S_ATT_HEADS)
        return (pltpu.make_async_copy(kout_ref.at[ph], kto_hbm.at[seq, hs], out_sem.at[ph, 0]),
                pltpu.make_async_copy(vout_ref.at[ph], vto_hbm.at[seq, hs], out_sem.at[ph, 1]))

    def start(cps):
        for c in cps:
            c.start()

    def wait(cps):
        for c in cps:
            c.wait()

    def window_phase(ph):
        lanes = slice(ph * S_ATT_HEADS * HD, (ph + 1) * S_ATT_HEADS * HD)
        row = pl.ds(i, 1)
        o_row, kt_new, vt_new = _s_attn_heads(qs_ref[row, lanes], kn_ref[row, lanes],
                                              vn_ref[row, lanes], kin_ref[ph], vin_ref[ph])
        bs_ref[row, lanes] = o_row
        kout_ref[ph] = kt_new
        vout_ref[ph] = vt_new

    if att:
        @pl.when(i == 0)
        def _():
            kout_ref[1] = jnp.zeros(kout_ref.shape[1:], F32)
            vout_ref[1] = jnp.zeros(vout_ref.shape[1:], F32)
            start(copies_in(0, 0))
            start(copies_out(0, 1))

        wait(copies_in(i, 0))
        start(copies_in(i, 1))

        @pl.when(i > 0)
        def _():
            wait(copies_out(i - 1, 0))

        window_phase(0)

    m = mod_ref[0]
    cw = cw_ref[...]
    cb = cb_ref[...]
    x = x_ref[...]
    if pre:
        x = x + m[2:3] * (_dot(a_ref[...], wo_ref[0:W_A, :]) + _dot(b_ref[...], wo_ref[W_A:W_A + W_B, :]))
    h = _rms_mod(x, m[4:5], m[3:4]).astype(BF16)
    for j in range(N_FF_CHUNKS):
        if att and j == FF_SPLIT:
            start(copies_out(i, 0))
            wait(copies_in(i, 1))
            start(copies_in(jnp.minimum(i + 1, last), 0))
            wait(copies_out(jnp.maximum(i - 1, 0), 1))
            window_phase(1)
        conv = []
        for half in range(2):
            c0 = half * D_FF + j * FF_CHUNK
            cols = slice(c0, c0 + FF_CHUNK)
            up = _dot(h, wup_ref[:, cols])
            tail = carry_ref[:, cols]
            carry_ref[:, cols] = up[TM - SUBLANES:, :]
            conv.append(cb[:, cols] + cw[0:1, cols] * _shift_rows(up, tail, 2)
                        + cw[1:2, cols] * _shift_rows(up, tail, 1) + cw[2:3, cols] * up)
        act = conv[0] * jax.nn.sigmoid(conv[0]) * conv[1]
        act_ref[:, j * FF_CHUNK:(j + 1) * FF_CHUNK] = act.astype(BF16)
    st_ref[0] = carry_ref[...]
    y = x + m[5:6] * _dot(act_ref[...], wdn_ref[...])
    if final:
        y = _rmsnorm(y) * fg_ref[...]
    o_ref[...] = y

    if att:
        start(copies_out(i, 1))

        @pl.when(i == last)
        def _():
            wait(copies_out(i, 0))
            wait(copies_out(i, 1))
            wait(copies_in(i, 0))


def _ffn(x2d, mod, layer, w_up, conv_w, conv_b, w_down, final_g, final, pre=None, att=None):
    n_tok = x2d.shape[0]
    f2 = 2 * D_FF
    row_spec = pl.BlockSpec((TM, D_MODEL), lambda i: (i, 0))
    mod_spec = pl.BlockSpec((1, 6, D_MODEL), lambda i: (i // TILES_PER_SEQ, 0, 0))
    args, specs = [x2d, mod], [row_spec, mod_spec]
    if pre is not None:
        args += list(pre)
        specs += [pl.BlockSpec((TM, W_A), lambda i: (i, 0)),
                  pl.BlockSpec((TM, W_B), lambda i: (i, 0)),
                  _const_spec((W_A + W_B, D_MODEL))]
    args += [w_up, conv_w, conv_b, w_down, final_g]
    specs += [_layer_spec((D_MODEL, f2), layer), _layer_spec((3, f2), layer),
              _layer_spec((1, f2), layer), _layer_spec((D_FF, D_MODEL), layer),
              _const_spec((1, D_MODEL))]
    out_specs = [row_spec, pl.BlockSpec((1, SUBLANES, f2), lambda i: (i // TILES_PER_SEQ, 0, 0))]
    out_shape = [jax.ShapeDtypeStruct((n_tok, D_MODEL), F32),
                 jax.ShapeDtypeStruct((BATCH, SUBLANES, f2), F32)]
    scratch = [pltpu.VMEM((SUBLANES, f2), F32), pltpu.VMEM((TM, D_FF), BF16)]
    if att is not None:
        q_s, kt = att[0], att[3]
        assert q_s.shape[0] == n_tok // TM, "one sample sequence per grid step"
        any_spec = pl.BlockSpec(memory_space=pl.ANY)
        args += list(att)
        specs += [_const_spec(q_s.shape)] * 3 + [any_spec, any_spec]
        out_specs += [pl.BlockSpec(q_s.shape, lambda i: (0, 0)), any_spec, any_spec]
        out_shape += [jax.ShapeDtypeStruct(q_s.shape, F32),
                      jax.ShapeDtypeStruct(kt.shape, F32), jax.ShapeDtypeStruct(kt.shape, F32)]
        win_buf = pltpu.VMEM((S_ATT_PHASES, S_ATT_HEADS, HD, WIN), F32)
        scratch += [win_buf] * 4 + [pltpu.SemaphoreType.DMA((S_ATT_PHASES, 2))] * 2
    return pl.pallas_call(
        functools.partial(_ffn_kernel, pre=pre is not None, final=final, att=att is not None),
        grid=(n_tok // TM,),
        in_specs=specs,
        out_specs=out_specs,
        out_shape=out_shape,
        scratch_shapes=scratch,
        compiler_params=_cparams("arbitrary"),
        name="conv_ffn_final" if final else "conv_ffn",
    )(*args)


N_GROUPS = TM // SUBLANES


def _rg_gates(xc, wa_ref, ba, wx_ref, bx, lam):
    xcb = xc.astype(BF16)
    r_parts, i_parts = [], []
    for g in range(RG_HEADS):
        blk = xcb[:, g * RG_BLOCK:(g + 1) * RG_BLOCK]
        r_parts.append(_dot(blk, wa_ref[g]))
        i_parts.append(_dot(blk, wx_ref[g]))
    r = jax.nn.sigmoid(jnp.concatenate(r_parts, axis=-1) + ba)
    ig = jax.nn.sigmoid(jnp.concatenate(i_parts, axis=-1) + bx)
    log_a = r * ((-RG_C) * jax.nn.softplus(-lam))
    a = jnp.exp(log_a)
    b = jnp.sqrt(1.0 - a * a) * (ig * xc)
    return a, b


def _rglru_kernel(x_ref, mod_ref, win_ref, cw_ref, cb_ref, wa_ref, ba_ref, wx_ref, bx_ref,
                  lam_ref, wout_ref, o_ref, cst_ref, hst_ref,
                  xtail_ref, hcar_ref, a_ref, b_ref, hs_ref):
    i = pl.program_id(0)

    @pl.when(i % TILES_PER_SEQ == 0)
    def _():
        xtail_ref[...] = jnp.zeros_like(xtail_ref)
        hcar_ref[...] = jnp.zeros_like(hcar_ref)

    x = x_ref[...]
    m = mod_ref[0]
    h = _rms_mod(x, m[1:2], m[0:1]).astype(BF16)
    gate = _dot(h, win_ref[:, 0:D_RNN])
    xr = _dot(h, win_ref[:, D_RNN:2 * D_RNN])
    cw = cw_ref[...]
    tail = xtail_ref[...]
    xc = (cb_ref[...] + cw[3:4] * xr + cw[2:3] * _shift_rows(xr, tail, 1)
          + cw[1:2] * _shift_rows(xr, tail, 2) + cw[0:1] * _shift_rows(xr, tail, 3))
    last = xr[TM - SUBLANES:, :]
    xtail_ref[...] = last
    cst_ref[0] = last

    a, b = _rg_gates(xc, wa_ref, ba_ref[...], wx_ref, bx_ref[...], lam_ref[...])

    a3 = a.reshape(N_GROUPS, SUBLANES, D_RNN)
    b3 = b.reshape(N_GROUPS, SUBLANES, D_RNN)
    r_id = lax.broadcasted_iota(jnp.int32, (N_GROUPS, SUBLANES, D_RNN), 1)
    for s in (1, 2, 4):
        keep = r_id >= s
        a_sh = jnp.where(keep, pltpu.roll(a3, s, 1), 1.0)
        b_sh = jnp.where(keep, pltpu.roll(b3, s, 1), 0.0)
        b3 = a3 * b_sh + b3
        a3 = a3 * a_sh
    a_ref[...] = a3
    b_ref[...] = b3

    def step(g, hc):
        hs = a_ref[g] * hc + b_ref[g]
        hs_ref[g] = hs
        return hs[SUBLANES - 1:SUBLANES, :]

    hc = lax.fori_loop(0, N_GROUPS, step, hcar_ref[...])
    hcar_ref[...] = hc
    hst_ref[0] = hc

    y = (jax.nn.gelu(gate) * hs_ref[...].reshape(TM, D_RNN)).astype(BF16)
    o_ref[...] = x + m[2:3] * _dot(y, wout_ref[...])


def _rglru(x2d, mod, w_in, conv_w, conv_b, w_a, b_a, w_x, b_x, lam, w_out):
    n_tok = x2d.shape[0]
    return pl.pallas_call(
        _rglru_kernel,
        grid=(n_tok // TM,),
        in_specs=[
            pl.BlockSpec((TM, D_MODEL), lambda i: (i, 0)),
            pl.BlockSpec((1, 6, D_MODEL), lambda i: (i // TILES_PER_SEQ, 0, 0)),
            _const_spec((D_MODEL, 2 * D_RNN)),
            _const_spec((4, D_RNN)),
            _const_spec((1, D_RNN)),
            _const_spec((RG_HEADS, RG_BLOCK, RG_BLOCK)),
            _const_spec((1, D_RNN)),
            _const_spec((RG_HEADS, RG_BLOCK, RG_BLOCK)),
            _const_spec((1, D_RNN)),
            _const_spec((1, D_RNN)),
            _const_spec((D_RNN, D_MODEL)),
        ],
        out_specs=[
            pl.BlockSpec((TM, D_MODEL), lambda i: (i, 0)),
            pl.BlockSpec((1, SUBLANES, D_RNN), lambda i: (i // TILES_PER_SEQ, 0, 0)),
            pl.BlockSpec((1, 1, D_RNN), lambda i: (i // TILES_PER_SEQ, 0, 0)),
        ],
        out_shape=[
            jax.ShapeDtypeStruct((n_tok, D_MODEL), F32),
            jax.ShapeDtypeStruct((BATCH, SUBLANES, D_RNN), F32),
            jax.ShapeDtypeStruct((BATCH, 1, D_RNN), F32),
        ],
        scratch_shapes=[
            pltpu.VMEM((SUBLANES, D_RNN), F32),
            pltpu.VMEM((1, D_RNN), F32),
            pltpu.VMEM((N_GROUPS, SUBLANES, D_RNN), F32),
            pltpu.VMEM((N_GROUPS, SUBLANES, D_RNN), F32),
            pltpu.VMEM((N_GROUPS, SUBLANES, D_RNN), F32),
        ],
        compiler_params=_cparams("arbitrary"),
        name="rglru",
    )(x2d, mod, w_in, conv_w, conv_b, w_a, b_a, w_x, b_x, lam, w_out)


def _s_even_in_kernel(x_ref, mod_ref, w_ref, lng_ref, lnb_ref, ws0_ref, bs0_ref,
                      va_ref, a_ref, q_ref, k_ref, v_ref):
    mod = mod_ref[...]
    h = _rms_mod(x_ref[...], mod[:, D_MODEL:2 * D_MODEL], mod[:, 0:D_MODEL]).astype(BF16)
    u = jax.nn.gelu(_dot(h, w_ref[:, 0:W_A]))
    va = _layernorm(jax.nn.gelu(_dot(h, w_ref[:, W_A:2 * W_A])), lng_ref[...], lnb_ref[...])
    va_ref[...] = va
    a_ref[...] = (u * (ws0_ref[...] * va + bs0_ref[...])).astype(BF16)
    base = 2 * W_A
    q_ref[...] = _dot(h, w_ref[:, base:base + W_B])
    k_ref[...] = _dot(h, w_ref[:, base + W_B:base + 2 * W_B])
    v_ref[...] = _dot(h, w_ref[:, base + 2 * W_B:base + 3 * W_B])


def _s_even_in(x, mod, w_in, ln_g, ln_b, ws0, bs0):
    n = x.shape[0]
    f = jax.ShapeDtypeStruct((n, W_B), F32)
    return pl.pallas_call(
        _s_even_in_kernel,
        out_shape=[f, jax.ShapeDtypeStruct((n, W_A), BF16), f, f, f],
        compiler_params=pltpu.CompilerParams(vmem_limit_bytes=VMEM_LIMIT),
        name="s_even_in",
    )(x, mod, w_in, ln_g, ln_b, ws0, bs0)


def _s_attn_heads(qrow, knrow, vnrow, kt3, vt3):
    nh = kt3.shape[0]
    n_hd = nh * HD
    qrow = qrow * (HD ** -0.5)
    hrow = lax.broadcasted_iota(jnp.int32, (SUBLANES, n_hd), 0)
    hcol = lax.broadcasted_iota(jnp.int32, (SUBLANES, n_hd), 1) // HD
    own = hrow == hcol
    qbd = jnp.where(own, jnp.broadcast_to(qrow, (SUBLANES, n_hd)), 0.0)
    kt = kt3.reshape(n_hd, WIN)
    vt = vt3.reshape(n_hd, WIN)
    s = _dot(qbd.astype(BF16), kt.astype(BF16))
    s_new = jnp.sum(qbd * knrow, axis=-1, keepdims=True)
    t = lax.broadcasted_iota(jnp.int32, (SUBLANES, WIN), 1)
    dist = WIN - t
    vtb = vt.astype(BF16)
    outs, lses = [], []
    for d in DILATIONS:
        valid = ((dist & (d - 1)) == 0) & (dist <= N_BACK * d)
        sm = jnp.where(valid, s, NEG_INF)
        mx = jnp.maximum(jnp.max(sm, axis=-1, keepdims=True), s_new)
        e = jnp.where(valid, jnp.exp(sm - mx), 0.0)
        e_new = jnp.exp(s_new - mx)
        den = jnp.sum(e, axis=-1, keepdims=True) + e_new
        o = lax.dot_general(e.astype(BF16), vtb, (((1,), (1,)), ((), ())),
                            preferred_element_type=F32)
        outs.append((o + e_new * vnrow) / den)
        lses.append(mx + jnp.log(den))
    mx = jnp.maximum(jnp.maximum(lses[0], lses[1]), lses[2])
    ws = [jnp.exp(l - mx) for l in lses]
    num = ws[0] * outs[0] + ws[1] * outs[1] + ws[2] * outs[2]
    mixed = num / (ws[0] + ws[1] + ws[2])
    o_row = jnp.sum(jnp.where(own, mixed, 0.0), axis=0, keepdims=True)

    r2 = lax.broadcasted_iota(jnp.int32, (n_hd, n_hd), 0)
    c2 = lax.broadcasted_iota(jnp.int32, (n_hd, n_hd), 1)
    diag = r2 == c2
    kcol = jnp.sum(jnp.where(diag, jnp.broadcast_to(knrow, (n_hd, n_hd)), 0.0), axis=-1, keepdims=True)
    vcol = jnp.sum(jnp.where(diag, jnp.broadcast_to(vnrow, (n_hd, n_hd)), 0.0), axis=-1, keepdims=True)
    tt = lax.broadcasted_iota(jnp.int32, (n_hd, WIN), 1)
    is_last = tt == WIN - 1
    kt_new = jnp.where(is_last, kcol, pltpu.roll(kt, WIN - 1, 1)).reshape(nh, HD, WIN)
    vt_new = jnp.where(is_last, vcol, pltpu.roll(vt, WIN - 1, 1)).reshape(nh, HD, WIN)
    return o_row, kt_new, vt_new


def _s_even_out_kernel(x_ref, mod_ref, a_ref, b_ref, w_ref, o_ref):
    g1 = mod_ref[:, 2 * D_MODEL:3 * D_MODEL]
    mo = _dot(a_ref[...], w_ref[0:W_A, :]) + _dot(b_ref[...].astype(BF16), w_ref[W_A:W_A + W_B, :])
    o_ref[...] = x_ref[...] + g1 * mo


def _s_even_out(x, mod, a_out, b_out, w_out):
    return pl.pallas_call(
        _s_even_out_kernel,
        out_shape=jax.ShapeDtypeStruct(x.shape, F32),
        compiler_params=pltpu.CompilerParams(vmem_limit_bytes=VMEM_LIMIT),
        name="s_even_out",
    )(x, mod, a_out, b_out, w_out)


def _s_ffn_kernel(x_ref, mod_ref, p2_ref, p1_ref, wup_ref, cw_ref, cb_ref, wdn_ref, fg_ref,
                  o_ref, up_ref, *, final):
    x = x_ref[...]
    mod = mod_ref[...]
    h = _rms_mod(x, mod[:, 4 * D_MODEL:5 * D_MODEL], mod[:, 3 * D_MODEL:4 * D_MODEL]).astype(BF16)
    up = _dot(h, wup_ref[...])
    up_ref[...] = up
    cw = cw_ref[...]
    conv = cb_ref[...] + cw[0:1] * p2_ref[...] + cw[1:2] * p1_ref[...] + cw[2:3] * up
    ca, cg = conv[:, 0:D_FF], conv[:, D_FF:2 * D_FF]
    act = (ca * jax.nn.sigmoid(ca) * cg).astype(BF16)
    y = x + mod[:, 5 * D_MODEL:6 * D_MODEL] * _dot(act, wdn_ref[...])
    if final:
        y = _rmsnorm(y) * fg_ref[...]
    o_ref[...] = y


def _s_ffn(x, mod, p2, p1, layer, w_up, conv_w, conv_b, w_down, final_g, final):
    n = x.shape[0]
    f2 = 2 * D_FF
    return pl.pallas_call(
        functools.partial(_s_ffn_kernel, final=final),
        grid=(1,),
        in_specs=[_const_spec(x.shape), _const_spec(mod.shape), _const_spec(p2.shape),
                  _const_spec(p1.shape), _layer_spec((D_MODEL, f2), layer),
                  _layer_spec((3, f2), layer), _layer_spec((1, f2), layer),
                  _layer_spec((D_FF, D_MODEL), layer), _const_spec((1, D_MODEL))],
        out_specs=[pl.BlockSpec(x.shape, lambda i: (0, 0)), pl.BlockSpec((n, f2), lambda i: (0, 0))],
        out_shape=[jax.ShapeDtypeStruct(x.shape, F32), jax.ShapeDtypeStruct((n, f2), F32)],
        compiler_params=_cparams("arbitrary"),
        name="s_conv_ffn_final" if final else "s_conv_ffn",
    )(x, mod, p2, p1, w_up, conv_w, conv_b, w_down, final_g)


def _s_rglru_kernel(x_ref, mod_ref, c0_ref, c1_ref, c2_ref, h0_ref, win_ref, cw_ref, cb_ref,
                    wa_ref, ba_ref, wx_ref, bx_ref, lam_ref, wout_ref, o_ref, xr_ref, hn_ref):
    x = x_ref[...]
    mod = mod_ref[...]
    h = _rms_mod(x, mod[:, D_MODEL:2 * D_MODEL], mod[:, 0:D_MODEL]).astype(BF16)
    gate = _dot(h, win_ref[:, 0:D_RNN])
    xr = _dot(h, win_ref[:, D_RNN:2 * D_RNN])
    xr_ref[...] = xr
    cw = cw_ref[...]
    xc = (cb_ref[...] + cw[0:1] * c0_ref[...] + cw[1:2] * c1_ref[...] + cw[2:3] * c2_ref[...]
          + cw[3:4] * xr)
    a, b = _rg_gates(xc, wa_ref, ba_ref[...], wx_ref, bx_ref[...], lam_ref[...])
    hn = a * h0_ref[...] + b
    hn_ref[...] = hn
    y = (jax.nn.gelu(gate) * hn).astype(BF16)
    o_ref[...] = x + mod[:, 2 * D_MODEL:3 * D_MODEL] * _dot(y, wout_ref[...])


def _s_rglru(x, mod, c0, c1, c2, h0, w_in, conv_w, conv_b, w_a, b_a, w_x, b_x, lam, w_out):
    f = jax.ShapeDtypeStruct(x.shape, F32)
    return pl.pallas_call(
        _s_rglru_kernel,
        out_shape=[f, f, f],
        compiler_params=pltpu.CompilerParams(vmem_limit_bytes=VMEM_LIMIT),
        name="s_rglru",
    )(x, mod, c0, c1, c2, h0, w_in, conv_w, conv_b, w_a, b_a, w_x, b_x, lam, w_out)


def kernel(x_prompt, x_sample, cache_win_k, cache_win_v, state_rglru_conv, state_rglru_h, state_ffn_conv, c_prompt, c_sample, w_ada, b_ada, w_in_even, ln_v_g, ln_v_b, w_sgu, b_sgu, w_out_even, w_in_odd, rg_conv_w, rg_conv_b, rg_w_a, rg_b_a, rg_w_x, rg_b_x, rg_lambda, w_out_odd, ffn_w_up, ffn_conv_w, ffn_conv_b, ffn_w_down, final_g):
    w_in_even_b = w_in_even[0].astype(BF16)
    w_out_even_b = w_out_even[0].astype(BF16)
    w_in_odd_b = w_in_odd[0].astype(BF16)
    w_out_odd_b = w_out_odd[0].astype(BF16)
    rg_w_a_b = rg_w_a[0].astype(BF16)
    rg_w_x_b = rg_w_x[0].astype(BF16)
    w_up_b = ffn_w_up.astype(BF16)
    w_down_b = ffn_w_down.astype(BF16)
    conv_b3 = ffn_conv_b.reshape(2, 1, 2 * D_FF)
    final_g2 = final_g.reshape(1, D_MODEL)

    pad = jnp.zeros((ADA_ROWS - BATCH - DEC_BATCH, D_MODEL), F32)
    mod = _ada(jnp.concatenate([c_prompt, c_sample, pad], axis=0), w_ada, b_ada)
    mod_p = mod[:, :BATCH].reshape(2, BATCH, 6, D_MODEL)
    mod_s = mod[:, BATCH:BATCH + DEC_BATCH]

    xs = x_sample.reshape(DEC_BATCH, D_MODEL)
    ws0 = jnp.repeat(w_sgu[0, :, 0, 0], A_GROUP).reshape(1, W_A)
    bs0 = jnp.repeat(b_sgu[0, :, 0], A_GROUP).reshape(1, W_A)
    va_s, a_s, q_s, k_s, v_s = _s_even_in(xs, mod_s[0], w_in_even_b, ln_v_g, ln_v_b, ws0, bs0)
    kt_c = cache_win_k[0].transpose(0, 2, 3, 1)
    vt_c = cache_win_v[0].transpose(0, 2, 3, 1)

    x = x_prompt.reshape(BATCH * SEQ, D_MODEL)
    a_out, q, k, v, kt_p, vt_p = _even_in(x, mod_p[0], w_in_even_b, ln_v_g, ln_v_b, w_sgu[0],
                                          b_sgu[0].T)
    b_out = _attn(q, k, v)
    x, ffn_st0 = _ffn(x, mod_p[0], 0, w_up_b, ffn_conv_w, conv_b3, w_down_b, final_g2, False,
                      pre=(a_out, b_out, w_out_even_b))
    x, rg_cst, rg_hst = _rglru(x, mod_p[1], w_in_odd_b, rg_conv_w[0], rg_conv_b, rg_w_a_b, rg_b_a,
                               rg_w_x_b, rg_b_x, rg_lambda, w_out_odd_b)
    y_p, ffn_st1, b_s, kt_n, vt_n = _ffn(x, mod_p[1], 1, w_up_b, ffn_conv_w, conv_b3, w_down_b,
                                         final_g2, True, att=(q_s, k_s, v_s, kt_c, vt_c))

    y_prompt = y_p.reshape(BATCH, SEQ, D_MODEL)
    to_win = lambda t: t.reshape(1, BATCH, N_HEADS, HD, WIN).transpose(0, 1, 4, 2, 3)
    win_k_prompt = to_win(kt_p)
    win_v_prompt = to_win(vt_p)
    rglru_conv_prompt = rg_cst[None, :, SUBLANES - 3:, :]
    rglru_h_prompt = rg_hst.reshape(1, BATCH, D_RNN)
    ffn_conv_prompt = jnp.stack([ffn_st0[:, SUBLANES - 2:, :], ffn_st1[:, SUBLANES - 2:, :]])

    xs = _s_even_out(xs, mod_s[0], a_s, b_s, w_out_even_b)
    st0 = state_ffn_conv[0]
    xs, up0 = _s_ffn(xs, mod_s[0], st0[:, 0], st0[:, 1], 0, w_up_b, ffn_conv_w, conv_b3, w_down_b,
                     final_g2, False)
    cst = state_rglru_conv[0]
    xs, xr_s, hn_s = _s_rglru(xs, mod_s[1], cst[:, 0], cst[:, 1], cst[:, 2], state_rglru_h[0],
                              w_in_odd_b, rg_conv_w[0], rg_conv_b, rg_w_a_b, rg_b_a, rg_w_x_b,
                              rg_b_x, rg_lambda, w_out_odd_b)
    st1 = state_ffn_conv[1]
    ys, up1 = _s_ffn(xs, mod_s[1], st1[:, 0], st1[:, 1], 1, w_up_b, ffn_conv_w, conv_b3, w_down_b,
                     final_g2, True)

    y_sample = ys.reshape(DEC_BATCH, 1, D_MODEL)
    chunk_v_sample = va_s.reshape(1, DEC_BATCH, 1, W_A)
    win_k_sample = kt_n.transpose(0, 3, 1, 2)[None]
    win_v_sample = vt_n.transpose(0, 3, 1, 2)[None]
    rglru_conv_sample = jnp.stack([cst[:, 1], cst[:, 2], xr_s], axis=1)[None]
    rglru_h_sample = hn_s[None]
    ffn_conv_sample = jnp.stack([jnp.stack([st0[:, 1], up0], axis=1),
                                 jnp.stack([st1[:, 1], up1], axis=1)])

    return (y_prompt, y_sample, win_k_prompt, win_v_prompt, rglru_conv_prompt, rglru_h_prompt,
            ffn_conv_prompt, chunk_v_sample, win_k_sample, win_v_sample, rglru_conv_sample,
            rglru_h_sample, ffn_conv_sample)
```

```python
import functools

import jax
import jax.numpy as jnp
from jax import lax
from jax.experimental import pallas as pl
from jax.experimental.pallas import tpu as pltpu

F32 = jnp.float32
BF16 = jnp.bfloat16

D_MODEL = 1024
BATCH = 4
SEQ = 4096
DEC_BATCH = 32
W_A = 512
A_GROUP = 128
G_A = 4
CHUNK = 128
W_B = 512
HD = 64
N_HEADS = 8
DILATIONS = (1, 4, 16)
N_BACK = 128
WIN = 2048
N_IN_EVEN = 2 * W_A + 3 * W_B
D_RNN = 1024
RG_BLOCK = 128
RG_HEADS = 8
RG_C = 8.0
D_FF = 2816
EPS = 1e-6
NEG_INF = -1e30
LOG2_E = 1.4426950408889634
LN_2 = 0.6931471805599453

LANES = 128
SUBLANES = 8
TM = 512
TILES_PER_SEQ = SEQ // TM
N_TILES = BATCH * TILES_PER_SEQ
FF_CHUNK = 256
N_FF_CHUNKS = D_FF // FF_CHUNK
N_SLABS = W_B // LANES
VMEM_LIMIT = 56 * 1024 * 1024


def _cparams(*sem):
    return pltpu.CompilerParams(dimension_semantics=sem, vmem_limit_bytes=VMEM_LIMIT)


def _const_spec(shape):
    nd = len(shape)
    return pl.BlockSpec(shape, lambda *_: (0,) * nd, pipeline_mode=pl.Buffered(1))


def _layer_spec(shape, layer):
    nd = len(shape)
    return pl.BlockSpec((None,) + tuple(shape), lambda *_: (layer,) + (0,) * nd,
                        pipeline_mode=pl.Buffered(1))


def _shift_rows(cur, tail, k):
    rolled = pltpu.roll(cur, k, 0)
    r = lax.broadcasted_iota(jnp.int32, (SUBLANES, cur.shape[1]), 0)
    head = jnp.where(r < k, pltpu.roll(tail, k, 0), rolled[0:SUBLANES])
    return jnp.concatenate([head, rolled[SUBLANES:]], axis=0)


def _rms_mod(x, scale, shift):
    xn = x * lax.rsqrt(jnp.mean(x * x, axis=-1, keepdims=True) + EPS)
    return xn * (1.0 + scale) + shift


def _rmsnorm(x):
    return x * lax.rsqrt(jnp.mean(x * x, axis=-1, keepdims=True) + EPS)


def _layernorm(x, g, b):
    mu = jnp.mean(x, axis=-1, keepdims=True)
    xc = x - mu
    var = jnp.mean(xc * xc, axis=-1, keepdims=True)
    return xc * lax.rsqrt(var + EPS) * g + b


def _dot(a, b):
    return jnp.dot(a, b, preferred_element_type=F32)


ADA_ROWS = 40
ADA_TN = 1024


def _ada_kernel(c_ref, w_ref, b_ref, o_ref):
    c = c_ref[...]
    s = (c * jax.nn.sigmoid(c)).astype(BF16)
    o_ref[0] = _dot(s, w_ref[0].astype(BF16)) + b_ref[0]


def _ada(c_all, w_ada, b_ada):
    depth = w_ada.shape[0]
    n_out = w_ada.shape[2]
    return pl.pallas_call(
        _ada_kernel,
        grid=(depth, n_out // ADA_TN),
        in_specs=[
            pl.BlockSpec((ADA_ROWS, D_MODEL), lambda l, j: (0, 0)),
            pl.BlockSpec((1, D_MODEL, ADA_TN), lambda l, j: (l, 0, j)),
            pl.BlockSpec((1, 1, ADA_TN), lambda l, j: (l, 0, j)),
        ],
        out_specs=pl.BlockSpec((1, ADA_ROWS, ADA_TN), lambda l, j: (l, 0, j)),
        out_shape=jax.ShapeDtypeStruct((depth, ADA_ROWS, n_out), F32),
        compiler_params=_cparams("arbitrary", "arbitrary"),
        name="ada_mod",
    )(c_all, w_ada, b_ada.reshape(depth, 1, n_out))


def _even_in_kernel(x_ref, mod_ref, w_ref, lng_ref, lnb_ref, ws_ref, bst_ref,
                    a_ref, q_ref, k_ref, v_ref, kt_ref, vt_ref):
    i = pl.program_id(0)
    m = mod_ref[0]
    h = _rms_mod(x_ref[...], m[1:2], m[0:1]).astype(BF16)
    base = 2 * W_A
    va = _layernorm(jax.nn.gelu(_dot(h, w_ref[:, W_A:2 * W_A])), lng_ref[...], lnb_ref[...])
    u = jax.nn.gelu(_dot(h, w_ref[:, 0:W_A]))
    q = _dot(h, w_ref[:, base:base + W_B])
    k = _dot(h, w_ref[:, base + W_B:base + 2 * W_B])
    vab = va.astype(BF16)
    row = lax.broadcasted_iota(jnp.int32, (CHUNK, CHUNK), 0)
    col = lax.broadcasted_iota(jnp.int32, (CHUNK, CHUNK), 1)
    causal = col <= row
    bst = bst_ref[...]
    for g in range(G_A):
        wg = jnp.where(causal, ws_ref[g], 0.0).astype(BF16)
        bias = bst[:, g:g + 1]
        lo, hi = g * A_GROUP, (g + 1) * A_GROUP
        for c in range(TM // CHUNK):
            r0, r1 = c * CHUNK, (c + 1) * CHUNK
            mix = _dot(wg, vab[r0:r1, lo:hi]) + bias
            a_ref[r0:r1, lo:hi] = (u[r0:r1, lo:hi] * mix).astype(BF16)
    v = _dot(h, w_ref[:, base + 2 * W_B:base + 3 * W_B])
    for s in range(N_SLABS):
        q_ref[s] = q[:, s * LANES:(s + 1) * LANES]
        k_ref[s] = k[:, s * LANES:(s + 1) * LANES]
        v_ref[s] = v[:, s * LANES:(s + 1) * LANES]

    @pl.when(i % TILES_PER_SEQ >= TILES_PER_SEQ - WIN // TM)
    def _():
        kt_ref[0] = k.T
        vt_ref[0] = v.T


def _even_in(x2d, mod, w_in, ln_g, ln_b, w_s, b_s_t):
    n_tok = x2d.shape[0]
    first_win_tile = TILES_PER_SEQ - WIN // TM
    qkv_shape = jax.ShapeDtypeStruct((N_SLABS, n_tok, LANES), F32)
    qkv_spec = pl.BlockSpec((N_SLABS, TM, LANES), lambda i: (0, i, 0))
    win_shape = jax.ShapeDtypeStruct((BATCH, W_B, WIN), F32)
    win_spec = pl.BlockSpec(
        (1, W_B, TM),
        lambda i: (i // TILES_PER_SEQ, 0, jnp.maximum(i % TILES_PER_SEQ - first_win_tile, 0)))
    return pl.pallas_call(
        _even_in_kernel,
        grid=(N_TILES,),
        in_specs=[
            pl.BlockSpec((TM, D_MODEL), lambda i: (i, 0)),
            pl.BlockSpec((1, 6, D_MODEL), lambda i: (i // TILES_PER_SEQ, 0, 0)),
            _const_spec((D_MODEL, N_IN_EVEN)),
            _const_spec((1, W_A)),
            _const_spec((1, W_A)),
            _const_spec((G_A, CHUNK, CHUNK)),
            _const_spec((CHUNK, G_A)),
        ],
        out_specs=[
            pl.BlockSpec((TM, W_A), lambda i: (i, 0)),
            qkv_spec, qkv_spec, qkv_spec, win_spec, win_spec,
        ],
        out_shape=[
            jax.ShapeDtypeStruct((n_tok, W_A), BF16),
            qkv_shape, qkv_shape, qkv_shape, win_shape, win_shape,
        ],
        compiler_params=_cparams("arbitrary"),
        name="even_in",
    )(x2d, mod, w_in, ln_g, ln_b, w_s, b_s_t)


ATT_BLK = 128
ATT_UNROLL = 16


def _attn_kernel(q_ref, k_ref, v_ref, o_ref, acc_ref, max_ref, den_ref, q4_ref, k4_ref, v4_ref):
    q2, k2, v2 = q_ref.at[0], k_ref.at[0], v_ref.at[0]
    seq4 = SEQ // 4
    for src, dst in ((q2, q4_ref), (k2, k4_ref), (v2, v4_ref)):
        for c0 in range(4):
            for r in range(0, seq4, 2 * ATT_BLK):
                dst[c0 * seq4 + r:c0 * seq4 + r + 2 * ATT_BLK, :] = (
                    src[pl.ds(c0 + 4 * r, 2 * ATT_BLK, stride=4), :])
    qi = lax.broadcasted_iota(jnp.int32, (ATT_BLK, 2 * ATT_BLK), 0)
    kj = lax.broadcasted_iota(jnp.int32, (ATT_BLK, 2 * ATT_BLK), 1)
    dist = qi + ATT_BLK - kj
    band = (dist >= 0) & (dist <= N_BACK)
    bias_full = jnp.where(band, 0.0, NEG_INF).astype(F32)
    bias_first = jnp.where(band & (kj >= ATT_BLK), 0.0, NEG_INF).astype(F32)
    lane = lax.broadcasted_iota(jnp.int32, (ATT_BLK, LANES), 1)
    head0 = lane < HD

    for p, d in enumerate(DILATIONS):
        n_blk = SEQ // (d * ATT_BLK)

        def unit(idx, carry, p=p, d=d, n_blk=n_blk):
            k_prev, v_prev = carry
            c = idx // n_blk
            b = idx % n_blk
            if d == 1:
                qs, ks, vs = q2, k2, v2
                rows = pl.ds(pl.multiple_of(ATT_BLK * b, ATT_BLK), ATT_BLK)
            elif d == 4:
                qs, ks, vs = q4_ref, k4_ref, v4_ref
                rows = pl.ds(pl.multiple_of(c * seq4 + ATT_BLK * b, ATT_BLK), ATT_BLK)
            else:
                qs, ks, vs = q4_ref, k4_ref, v4_ref
                rows = pl.ds((c % 4) * seq4 + c // 4 + 4 * ATT_BLK * b, ATT_BLK, stride=4)
            qb = qs[rows, :] * (HD ** -0.5 * LOG2_E)
            k_own = ks[rows, :].astype(BF16)
            v_own = vs[rows, :].astype(BF16)
            kb = jnp.concatenate([k_prev, k_own], axis=0)
            vb = jnp.concatenate([v_prev, v_own], axis=0)
            bias = jnp.where(b == 0, bias_first, bias_full)
            outs, mxs, dens = [], [], []
            for hh in range(2):
                sel = head0 if hh == 0 else jnp.logical_not(head0)
                qm = jnp.where(sel, qb, 0.0).astype(BF16)
                s = lax.dot_general(qm, kb, (((1,), (1,)), ((), ())),
                                    preferred_element_type=F32) + bias
                mx = jnp.max(s, axis=-1, keepdims=True)
                e = jnp.exp2(s - mx)
                dens.append(jnp.sum(e, axis=-1, keepdims=True))
                mxs.append(mx)
                outs.append(_dot(e.astype(BF16), vb))
            acc_ref[p, rows, :] = jnp.where(head0, outs[0], outs[1])
            max_ref[p, rows, :] = jnp.where(head0, mxs[0], mxs[1])
            den_ref[p, rows, :] = jnp.where(head0, dens[0], dens[1])
            return k_own, v_own

        zeros = jnp.zeros((ATT_BLK, LANES), BF16)
        lax.fori_loop(0, SEQ // ATT_BLK, unit, (zeros, zeros), unroll=ATT_UNROLL)

    def mix(t, carry):
        c0 = t // (seq4 // TM)
        r = (t % (seq4 // TM)) * TM
        rows4 = pl.ds(pl.multiple_of(c0 * seq4 + r, TM), TM)
        rows = pl.ds(c0 + 4 * r, TM, stride=4)
        m0, m1, m2 = max_ref[0, rows, :], max_ref[1, rows4, :], max_ref[2, rows4, :]
        mx = jnp.maximum(jnp.maximum(m0, m1), m2)
        e0, e1, e2 = jnp.exp2(m0 - mx), jnp.exp2(m1 - mx), jnp.exp2(m2 - mx)
        num = e0 * acc_ref[0, rows, :] + e1 * acc_ref[1, rows4, :] + e2 * acc_ref[2, rows4, :]
        den = e0 * den_ref[0, rows, :] + e1 * den_ref[1, rows4, :] + e2 * den_ref[2, rows4, :]
        acc_ref[0, rows, :] = num / den
        return carry

    lax.fori_loop(0, SEQ // TM, mix, 0)

    def emit(t, carry):
        rows = pl.ds(pl.multiple_of(t * TM, TM), TM)
        o_ref[rows, :] = acc_ref[0, rows, :].astype(BF16)
        return carry

    lax.fori_loop(0, SEQ // TM, emit, 0)


def _attn(q, k, v):
    n_tok = q.shape[1]
    spec = pl.BlockSpec((1, SEQ, LANES), lambda n, s: (s, n, 0))
    return pl.pallas_call(
        _attn_kernel,
        grid=(BATCH, N_SLABS),
        in_specs=[spec, spec, spec],
        out_specs=pl.BlockSpec((SEQ, LANES), lambda n, s: (n, s)),
        out_shape=jax.ShapeDtypeStruct((n_tok, W_B), BF16),
        scratch_shapes=([pltpu.VMEM((len(DILATIONS), SEQ, LANES), F32)] * 3
                        + [pltpu.VMEM((SEQ, LANES), F32)] * 3),
        compiler_params=_cparams("arbitrary", "arbitrary"),
        name="dil_attn",
    )(q, k, v)


S_ATT_HEADS = 4
S_ATT_PHASES = N_HEADS // S_ATT_HEADS
FF_SPLIT = 6


def _ffn_kernel(*refs, pre, final, att):
    refs = list(refs)
    x_ref, mod_ref = refs[0:2]
    del refs[0:2]
    if pre:
        a_ref, b_ref, wo_ref = refs[0:3]
        del refs[0:3]
    wup_ref, cw_ref, cb_ref, wdn_ref, fg_ref = refs[0:5]
    del refs[0:5]
    if att:
        (qs_ref, kn_ref, vn_ref, kt_hbm, vt_hbm, o_ref, st_ref, bs_ref, kto_hbm, vto_hbm,
         carry_ref, act_ref, kin_ref, vin_ref, kout_ref, vout_ref, in_sem, out_sem) = refs
    else:
        o_ref, st_ref, carry_ref, act_ref = refs
    i = pl.program_id(0)
    last = pl.num_programs(0) - 1

    @pl.when(i % TILES_PER_SEQ == 0)
    def _():
        carry_ref[...] = jnp.zeros_like(carry_ref)

    def copies_in(seq, ph):
        hs = pl.ds(ph * S_ATT_HEADS, S_ATT_HEADS)
        return (pltpu.make_async_copy(kt_hbm.at[seq, hs], kin_ref.at[ph], in_sem.at[ph, 0]),
                pltpu.make_async_copy(vt_hbm.at[seq, hs], vin_ref.at[ph], in_sem.at[ph, 1]))

    def copies_out(seq, ph):
        hs = pl.ds(ph * S_ATT_HEADS, S_ATT_HEADS)
        return (pltpu.make_async_copy(kout_ref.at[ph], kto_hbm.at[seq, hs], out_sem.at[ph, 0]),
                pltpu.make_async_copy(vout_ref.at[ph], vto_hbm.at[seq, hs], out_sem.at[ph, 1]))

    def start(cps):
        for c in cps:
            c.start()

    def wait(cps):
        for c in cps:
            c.wait()

    def window_phase(ph):
        lanes = slice(ph * S_ATT_HEADS * HD, (ph + 1) * S_ATT_HEADS * HD)
        row = pl.ds(i, 1)
        o_row, kt_new, vt_new = _s_attn_heads(qs_ref[row, lanes], kn_ref[row, lanes],
                                              vn_ref[row, lanes], kin_ref[ph], vin_ref[ph])
        bs_ref[row, lanes] = o_row
        kout_ref[ph] = kt_new
        vout_ref[ph] = vt_new

    if att:
        @pl.when(i == 0)
        def _():
            kout_ref[1] = jnp.zeros(kout_ref.shape[1:], F32)
            vout_ref[1] = jnp.zeros(vout_ref.shape[1:], F32)
            start(copies_in(0, 0))
            start(copies_out(0, 1))

        wait(copies_in(i, 0))
        start(copies_in(i, 1))

        @pl.when(i > 0)
        def _():
            wait(copies_out(i - 1, 0))

        window_phase(0)

    m = mod_ref[0]
    cw = cw_ref[...]
    cb = cb_ref[...]
    x = x_ref[...]
    if pre:
        x = x + m[2:3] * (_dot(a_ref[...], wo_ref[0:W_A, :]) + _dot(b_ref[...], wo_ref[W_A:W_A + W_B, :]))
    h = _rms_mod(x, m[4:5], m[3:4]).astype(BF16)
    for j in range(N_FF_CHUNKS):
        if att and j == FF_SPLIT:
            start(copies_out(i, 0))
            wait(copies_in(i, 1))
            start(copies_in(jnp.minimum(i + 1, last), 0))
            wait(copies_out(jnp.maximum(i - 1, 0), 1))
            window_phase(1)
        conv = []
        for half in range(2):
            c0 = half * D_FF + j * FF_CHUNK
            cols = slice(c0, c0 + FF_CHUNK)
            up = _dot(h, wup_ref[:, cols])
            tail = carry_ref[:, cols]
            carry_ref[:, cols] = up[TM - SUBLANES:, :]
            conv.append(cb[:, cols] + cw[0:1, cols] * _shift_rows(up, tail, 2)
                        + cw[1:2, cols] * _shift_rows(up, tail, 1) + cw[2:3, cols] * up)
        act = conv[0] * jax.nn.sigmoid(conv[0]) * conv[1]
        act_ref[:, j * FF_CHUNK:(j + 1) * FF_CHUNK] = act.astype(BF16)
    st_ref[0] = carry_ref[...]
    y = x + m[5:6] * _dot(act_ref[...], wdn_ref[...])
    if final:
        y = _rmsnorm(y) * fg_ref[...]
    o_ref[...] = y

    if att:
        start(copies_out(i, 1))

        @pl.when(i == last)
        def _():
            wait(copies_out(i, 0))
            wait(copies_out(i, 1))
            wait(copies_in(i, 0))


def _ffn(x2d, mod, layer, w_up, conv_w, conv_b, w_down, final_g, final, pre=None, att=None):
    n_tok = x2d.shape[0]
    f2 = 2 * D_FF
    row_spec = pl.BlockSpec((TM, D_MODEL), lambda i: (i, 0))
    mod_spec = pl.BlockSpec((1, 6, D_MODEL), lambda i: (i // TILES_PER_SEQ, 0, 0))
    args, specs = [x2d, mod], [row_spec, mod_spec]
    if pre is not None:
        args += list(pre)
        specs += [pl.BlockSpec((TM, W_A), lambda i: (i, 0)),
                  pl.BlockSpec((TM, W_B), lambda i: (i, 0)),
                  _const_spec((W_A + W_B, D_MODEL))]
    args += [w_up, conv_w, conv_b, w_down, final_g]
    specs += [_layer_spec((D_MODEL, f2), layer), _layer_spec((3, f2), layer),
              _layer_spec((1, f2), layer), _layer_spec((D_FF, D_MODEL), layer),
              _const_spec((1, D_MODEL))]
    out_specs = [row_spec, pl.BlockSpec((1, SUBLANES, f2), lambda i: (i // TILES_PER_SEQ, 0, 0))]
    out_shape = [jax.ShapeDtypeStruct((n_tok, D_MODEL), F32),
                 jax.ShapeDtypeStruct((BATCH, SUBLANES, f2), F32)]
    scratch = [pltpu.VMEM((SUBLANES, f2), F32), pltpu.VMEM((TM, D_FF), BF16)]
    if att is not None:
        q_s, kt = att[0], att[3]
        assert q_s.shape[0] == n_tok // TM, "one sample sequence per grid step"
        any_spec = pl.BlockSpec(memory_space=pl.ANY)
        args += list(att)
        specs += [_const_spec(q_s.shape)] * 3 + [any_spec, any_spec]
        out_specs += [pl.BlockSpec(q_s.shape, lambda i: (0, 0)), any_spec, any_spec]
        out_shape += [jax.ShapeDtypeStruct(q_s.shape, F32),
                      jax.ShapeDtypeStruct(kt.shape, F32), jax.ShapeDtypeStruct(kt.shape, F32)]
        win_buf = pltpu.VMEM((S_ATT_PHASES, S_ATT_HEADS, HD, WIN), F32)
        scratch += [win_buf] * 4 + [pltpu.SemaphoreType.DMA((S_ATT_PHASES, 2))] * 2
    return pl.pallas_call(
        functools.partial(_ffn_kernel, pre=pre is not None, final=final, att=att is not None),
        grid=(n_tok // TM,),
        in_specs=specs,
        out_specs=out_specs,
        out_shape=out_shape,
        scratch_shapes=scratch,
        compiler_params=_cparams("arbitrary"),
        name="conv_ffn_final" if final else "conv_ffn",
    )(*args)


SEG_LEN = TM // SUBLANES
RG_PAIR = 2


def _rg_gates(xc, wa_ref, ba, wx_ref, bx, lam):
    xcb = xc.astype(BF16)
    r_parts, i_parts = [], []
    for g in range(RG_HEADS):
        blk = xcb[:, g * RG_BLOCK:(g + 1) * RG_BLOCK]
        r_parts.append(_dot(blk, wa_ref[g]))
        i_parts.append(_dot(blk, wx_ref[g]))
    r = jax.nn.sigmoid(jnp.concatenate(r_parts, axis=-1) + ba)
    ig = jax.nn.sigmoid(jnp.concatenate(i_parts, axis=-1) + bx)
    log_a = r * ((-RG_C) * jax.nn.softplus(-lam))
    a = jnp.exp(log_a)
    b = jnp.sqrt(1.0 - a * a) * (ig * xc)
    return a, b


def _rglru_kernel(x_ref, mod_ref, perm_ref, unperm_ref, win_ref, cw_ref, cb_ref, wa_ref, ba_ref,
                  wx_ref, bx_ref, lam_ref, wout_ref, o_ref, cst_ref, hst_ref,
                  xtail_ref, hcar_ref, a_ref, b_ref, hs_ref, ac_ref):
    t = pl.program_id(0) % TILES_PER_SEQ

    @pl.when(t == 0)
    def _():
        xtail_ref[...] = jnp.zeros_like(xtail_ref)
        hcar_ref[...] = jnp.zeros_like(hcar_ref)

    sub = lax.broadcasted_iota(jnp.int32, (SUBLANES, D_RNN), 0)
    cw = cw_ref[...]

    def front(u):
        m = mod_ref[u]
        h = _rms_mod(x_ref[u], m[1:2], m[0:1]).astype(BF16)
        hp = _dot(perm_ref[...], h).astype(BF16)
        xr3 = _dot(hp, win_ref[:, D_RNN:2 * D_RNN]).reshape(SEG_LEN, SUBLANES, D_RNN)
        gate = _dot(hp, win_ref[:, 0:D_RNN])
        tail = xtail_ref[u]
        xc3 = cb_ref[...] + cw[3:4] * xr3
        for k in range(1, 4):
            wrap = [jnp.where(sub == 0, tail[SUBLANES - k + v:SUBLANES - k + v + 1, :],
                              pltpu.roll(xr3[SEG_LEN - k + v], 1, 0)) for v in range(k)]
            prev = jnp.concatenate([jnp.stack(wrap), xr3[0:SEG_LEN - k]], axis=0)
            xc3 = xc3 + cw[3 - k:4 - k] * prev
        last = jnp.zeros((SUBLANES, D_RNN), F32)
        for r in range(SUBLANES):
            last = jnp.where(sub == r, pltpu.roll(xr3[SEG_LEN - SUBLANES + r], r + 1, 0), last)
        xtail_ref[u] = last
        cst_ref[u] = last
        a, b = _rg_gates(xc3.reshape(TM, D_RNN), wa_ref, ba_ref[...], wx_ref, bx_ref[...],
                         lam_ref[...])
        a_ref[u] = a.reshape(SEG_LEN, SUBLANES, D_RNN)
        b_ref[u] = b.reshape(SEG_LEN, SUBLANES, D_RNN)
        return gate

    def scan(u):
        def step(v, carry):
            hl, ac = carry
            av = a_ref[u, v]
            hl = av * hl + b_ref[u, v]
            ac = av * ac
            hs_ref[u, v] = hl
            ac_ref[u, v] = ac
            return hl, ac

        zero = jnp.zeros((SUBLANES, D_RNN), F32)
        hl, ac = lax.fori_loop(0, SEG_LEN, step, (zero, zero + 1.0), unroll=True)
        h_in = jnp.where(sub == 0, hcar_ref[u], 0.0)
        for s in range(1, SUBLANES):
            h_in = jnp.where(sub == s, pltpu.roll(hl + ac * h_in, 1, 0), h_in)
        h_end = (hl + ac * h_in)[SUBLANES - 1:SUBLANES, :]
        hcar_ref[u] = h_end
        hst_ref[u] = h_end
        return hs_ref[u] + ac_ref[u] * h_in

    def back(u, gate, hs):
        y = (jax.nn.gelu(gate) * hs.reshape(TM, D_RNN)).astype(BF16)
        y = _dot(unperm_ref[...], y).astype(BF16)
        o_ref[u] = x_ref[u] + mod_ref[u][2:3] * _dot(y, wout_ref[...])

    gates = [front(u) for u in range(RG_PAIR)]
    for u in range(RG_PAIR):
        back(u, gates[u], scan(u))


def _rglru(x2d, mod, w_in, conv_w, conv_b, w_a, b_a, w_x, b_x, lam, w_out):
    n_tok = x2d.shape[0]
    x3 = x2d.reshape(BATCH, SEQ, D_MODEL)
    rho = jnp.arange(TM)
    perm = (jnp.arange(TM)[None, :] == ((rho % SUBLANES) * SEG_LEN + rho // SUBLANES)[:, None])
    perm = perm.astype(BF16)
    row_spec = pl.BlockSpec((RG_PAIR, TM, D_MODEL),
                            lambda i: (i // TILES_PER_SEQ, i % TILES_PER_SEQ, 0))

    def seq_spec(rows):
        return pl.BlockSpec((RG_PAIR, rows, D_RNN), lambda i: (i // TILES_PER_SEQ, 0, 0))

    out, cst, hst = pl.pallas_call(
        _rglru_kernel,
        grid=(n_tok // (RG_PAIR * TM),),
        in_specs=[
            row_spec,
            seq_spec(6),
            _const_spec((TM, TM)),
            _const_spec((TM, TM)),
            _const_spec((D_MODEL, 2 * D_RNN)),
            _const_spec((4, D_RNN)),
            _const_spec((1, D_RNN)),
            _const_spec((RG_HEADS, RG_BLOCK, RG_BLOCK)),
            _const_spec((1, D_RNN)),
            _const_spec((RG_HEADS, RG_BLOCK, RG_BLOCK)),
            _const_spec((1, D_RNN)),
            _const_spec((1, D_RNN)),
            _const_spec((D_RNN, D_MODEL)),
        ],
        out_specs=[row_spec, seq_spec(SUBLANES), seq_spec(1)],
        out_shape=[
            jax.ShapeDtypeStruct((BATCH, SEQ, D_MODEL), F32),
            jax.ShapeDtypeStruct((BATCH, SUBLANES, D_RNN), F32),
            jax.ShapeDtypeStruct((BATCH, 1, D_RNN), F32),
        ],
        scratch_shapes=[
            pltpu.VMEM((RG_PAIR, SUBLANES, D_RNN), F32),
            pltpu.VMEM((RG_PAIR, 1, D_RNN), F32),
        ] + [pltpu.VMEM((RG_PAIR, SEG_LEN, SUBLANES, D_RNN), F32)] * 4,
        compiler_params=_cparams("arbitrary"),
        name="rglru",
    )(x3, mod, perm, perm.T, w_in, conv_w, conv_b, w_a, b_a, w_x, b_x, lam, w_out)
    return out.reshape(n_tok, D_MODEL), cst, hst


def _s_even_in_kernel(x_ref, mod_ref, w_ref, lng_ref, lnb_ref, ws0_ref, bs0_ref,
                      va_ref, a_ref, q_ref, k_ref, v_ref):
    mod = mod_ref[...]
    h = _rms_mod(x_ref[...], mod[:, D_MODEL:2 * D_MODEL], mod[:, 0:D_MODEL]).astype(BF16)
    u = jax.nn.gelu(_dot(h, w_ref[:, 0:W_A]))
    va = _layernorm(jax.nn.gelu(_dot(h, w_ref[:, W_A:2 * W_A])), lng_ref[...], lnb_ref[...])
    va_ref[...] = va
    a_ref[...] = (u * (ws0_ref[...] * va + bs0_ref[...])).astype(BF16)
    base = 2 * W_A
    q_ref[...] = _dot(h, w_ref[:, base:base + W_B])
    k_ref[...] = _dot(h, w_ref[:, base + W_B:base + 2 * W_B])
    v_ref[...] = _dot(h, w_ref[:, base + 2 * W_B:base + 3 * W_B])


def _s_even_in(x, mod, w_in, ln_g, ln_b, ws0, bs0):
    n = x.shape[0]
    f = jax.ShapeDtypeStruct((n, W_B), F32)
    return pl.pallas_call(
        _s_even_in_kernel,
        out_shape=[f, jax.ShapeDtypeStruct((n, W_A), BF16), f, f, f],
        compiler_params=pltpu.CompilerParams(vmem_limit_bytes=VMEM_LIMIT),
        name="s_even_in",
    )(x, mod, w_in, ln_g, ln_b, ws0, bs0)


def _s_attn_heads(qrow, knrow, vnrow, kt3, vt3):
    nh = kt3.shape[0]
    n_hd = nh * HD
    qrow = qrow * (HD ** -0.5)
    hrow = lax.broadcasted_iota(jnp.int32, (SUBLANES, n_hd), 0)
    hcol = lax.broadcasted_iota(jnp.int32, (SUBLANES, n_hd), 1) // HD
    own = hrow == hcol
    qbd = jnp.where(own, jnp.broadcast_to(qrow, (SUBLANES, n_hd)), 0.0)
    kt = kt3.reshape(n_hd, WIN)
    vt = vt3.reshape(n_hd, WIN)
    s = _dot(qbd.astype(BF16), kt.astype(BF16))
    s_new = jnp.sum(qbd * knrow, axis=-1, keepdims=True)
    t = lax.broadcasted_iota(jnp.int32, (SUBLANES, WIN), 1)
    dist = WIN - t
    vtb = vt.astype(BF16)
    outs, lses = [], []
    for d in DILATIONS:
        valid = ((dist & (d - 1)) == 0) & (dist <= N_BACK * d)
        sm = jnp.where(valid, s, NEG_INF)
        mx = jnp.maximum(jnp.max(sm, axis=-1, keepdims=True), s_new)
        e = jnp.where(valid, jnp.exp(sm - mx), 0.0)
        e_new = jnp.exp(s_new - mx)
        den = jnp.sum(e, axis=-1, keepdims=True) + e_new
        o = lax.dot_general(e.astype(BF16), vtb, (((1,), (1,)), ((), ())),
                            preferred_element_type=F32)
        outs.append((o + e_new * vnrow) / den)
        lses.append(mx + jnp.log(den))
    mx = jnp.maximum(jnp.maximum(lses[0], lses[1]), lses[2])
    ws = [jnp.exp(l - mx) for l in lses]
    num = ws[0] * outs[0] + ws[1] * outs[1] + ws[2] * outs[2]
    mixed = num / (ws[0] + ws[1] + ws[2])
    o_row = jnp.sum(jnp.where(own, mixed, 0.0), axis=0, keepdims=True)

    r2 = lax.broadcasted_iota(jnp.int32, (n_hd, n_hd), 0)
    c2 = lax.broadcasted_iota(jnp.int32, (n_hd, n_hd), 1)
    diag = r2 == c2
    kcol = jnp.sum(jnp.where(diag, jnp.broadcast_to(knrow, (n_hd, n_hd)), 0.0), axis=-1, keepdims=True)
    vcol = jnp.sum(jnp.where(diag, jnp.broadcast_to(vnrow, (n_hd, n_hd)), 0.0), axis=-1, keepdims=True)
    tt = lax.broadcasted_iota(jnp.int32, (n_hd, WIN), 1)
    is_last = tt == WIN - 1
    kt_new = jnp.where(is_last, kcol, pltpu.roll(kt, WIN - 1, 1)).reshape(nh, HD, WIN)
    vt_new = jnp.where(is_last, vcol, pltpu.roll(vt, WIN - 1, 1)).reshape(nh, HD, WIN)
    return o_row, kt_new, vt_new


def _s_even_out_kernel(x_ref, mod_ref, a_ref, b_ref, w_ref, o_ref):
    g1 = mod_ref[:, 2 * D_MODEL:3 * D_MODEL]
    mo = _dot(a_ref[...], w_ref[0:W_A, :]) + _dot(b_ref[...].astype(BF16), w_ref[W_A:W_A + W_B, :])
    o_ref[...] = x_ref[...] + g1 * mo


def _s_even_out(x, mod, a_out, b_out, w_out):
    return pl.pallas_call(
        _s_even_out_kernel,
        out_shape=jax.ShapeDtypeStruct(x.shape, F32),
        compiler_params=pltpu.CompilerParams(vmem_limit_bytes=VMEM_LIMIT),
        name="s_even_out",
    )(x, mod, a_out, b_out, w_out)


def _s_ffn_kernel(x_ref, mod_ref, p2_ref, p1_ref, wup_ref, cw_ref, cb_ref, wdn_ref, fg_ref,
                  o_ref, up_ref, *, final):
    x = x_ref[...]
    mod = mod_ref[...]
    h = _rms_mod(x, mod[:, 4 * D_MODEL:5 * D_MODEL], mod[:, 3 * D_MODEL:4 * D_MODEL]).astype(BF16)
    up = _dot(h, wup_ref[...])
    up_ref[...] = up
    cw = cw_ref[...]
    conv = cb_ref[...] + cw[0:1] * p2_ref[...] + cw[1:2] * p1_ref[...] + cw[2:3] * up
    ca, cg = conv[:, 0:D_FF], conv[:, D_FF:2 * D_FF]
    act = (ca * jax.nn.sigmoid(ca) * cg).astype(BF16)
    y = x + mod[:, 5 * D_MODEL:6 * D_MODEL] * _dot(act, wdn_ref[...])
    if final:
        y = _rmsnorm(y) * fg_ref[...]
    o_ref[...] = y


def _s_ffn(x, mod, p2, p1, layer, w_up, conv_w, conv_b, w_down, final_g, final):
    n = x.shape[0]
    f2 = 2 * D_FF
    return pl.pallas_call(
        functools.partial(_s_ffn_kernel, final=final),
        grid=(1,),
        in_specs=[_const_spec(x.shape), _const_spec(mod.shape), _const_spec(p2.shape),
                  _const_spec(p1.shape), _layer_spec((D_MODEL, f2), layer),
                  _layer_spec((3, f2), layer), _layer_spec((1, f2), layer),
                  _layer_spec((D_FF, D_MODEL), layer), _const_spec((1, D_MODEL))],
        out_specs=[pl.BlockSpec(x.shape, lambda i: (0, 0)), pl.BlockSpec((n, f2), lambda i: (0, 0))],
        out_shape=[jax.ShapeDtypeStruct(x.shape, F32), jax.ShapeDtypeStruct((n, f2), F32)],
        compiler_params=_cparams("arbitrary"),
        name="s_conv_ffn_final" if final else "s_conv_ffn",
    )(x, mod, p2, p1, w_up, conv_w, conv_b, w_down, final_g)


def _s_rglru_kernel(x_ref, mod_ref, c0_ref, c1_ref, c2_ref, h0_ref, win_ref, cw_ref, cb_ref,
                    wa_ref, ba_ref, wx_ref, bx_ref, lam_ref, wout_ref, o_ref, xr_ref, hn_ref):
    x = x_ref[...]
    mod = mod_ref[...]
    h = _rms_mod(x, mod[:, D_MODEL:2 * D_MODEL], mod[:, 0:D_MODEL]).astype(BF16)
    gate = _dot(h, win_ref[:, 0:D_RNN])
    xr = _dot(h, win_ref[:, D_RNN:2 * D_RNN])
    xr_ref[...] = xr
    cw = cw_ref[...]
    xc = (cb_ref[...] + cw[0:1] * c0_ref[...] + cw[1:2] * c1_ref[...] + cw[2:3] * c2_ref[...]
          + cw[3:4] * xr)
    a, b = _rg_gates(xc, wa_ref, ba_ref[...], wx_ref, bx_ref[...], lam_ref[...])
    hn = a * h0_ref[...] + b
    hn_ref[...] = hn
    y = (jax.nn.gelu(gate) * hn).astype(BF16)
    o_ref[...] = x + mod[:, 2 * D_MODEL:3 * D_MODEL] * _dot(y, wout_ref[...])


def _s_rglru(x, mod, c0, c1, c2, h0, w_in, conv_w, conv_b, w_a, b_a, w_x, b_x, lam, w_out):
    f = jax.ShapeDtypeStruct(x.shape, F32)
    return pl.pallas_call(
        _s_rglru_kernel,
        out_shape=[f, f, f],
        compiler_params=pltpu.CompilerParams(vmem_limit_bytes=VMEM_LIMIT),
        name="s_rglru",
    )(x, mod, c0, c1, c2, h0, w_in, conv_w, conv_b, w_a, b_a, w_x, b_x, lam, w_out)


def kernel(x_prompt, x_sample, cache_win_k, cache_win_v, state_rglru_conv, state_rglru_h, state_ffn_conv, c_prompt, c_sample, w_ada, b_ada, w_in_even, ln_v_g, ln_v_b, w_sgu, b_sgu, w_out_even, w_in_odd, rg_conv_w, rg_conv_b, rg_w_a, rg_b_a, rg_w_x, rg_b_x, rg_lambda, w_out_odd, ffn_w_up, ffn_conv_w, ffn_conv_b, ffn_w_down, final_g):
    w_in_even_b = w_in_even[0].astype(BF16)
    w_out_even_b = w_out_even[0].astype(BF16)
    w_in_odd_b = w_in_odd[0].astype(BF16)
    w_out_odd_b = w_out_odd[0].astype(BF16)
    rg_w_a_b = rg_w_a[0].astype(BF16)
    rg_w_x_b = rg_w_x[0].astype(BF16)
    w_up_b = ffn_w_up.astype(BF16)
    w_down_b = ffn_w_down.astype(BF16)
    conv_b3 = ffn_conv_b.reshape(2, 1, 2 * D_FF)
    final_g2 = final_g.reshape(1, D_MODEL)

    pad = jnp.zeros((ADA_ROWS - BATCH - DEC_BATCH, D_MODEL), F32)
    mod = _ada(jnp.concatenate([c_prompt, c_sample, pad], axis=0), w_ada, b_ada)
    mod_p = mod[:, :BATCH].reshape(2, BATCH, 6, D_MODEL)
    mod_s = mod[:, BATCH:BATCH + DEC_BATCH]

    xs = x_sample.reshape(DEC_BATCH, D_MODEL)
    ws0 = jnp.repeat(w_sgu[0, :, 0, 0], A_GROUP).reshape(1, W_A)
    bs0 = jnp.repeat(b_sgu[0, :, 0], A_GROUP).reshape(1, W_A)
    va_s, a_s, q_s, k_s, v_s = _s_even_in(xs, mod_s[0], w_in_even_b, ln_v_g, ln_v_b, ws0, bs0)
    kt_c = cache_win_k[0].transpose(0, 2, 3, 1)
    vt_c = cache_win_v[0].transpose(0, 2, 3, 1)

    x = x_prompt.reshape(BATCH * SEQ, D_MODEL)
    a_out, q, k, v, kt_p, vt_p = _even_in(x, mod_p[0], w_in_even_b, ln_v_g, ln_v_b, w_sgu[0],
                                          b_sgu[0].T)
    b_out = _attn(q, k, v)
    x, ffn_st0 = _ffn(x, mod_p[0], 0, w_up_b, ffn_conv_w, conv_b3, w_down_b, final_g2, False,
                      pre=(a_out, b_out, w_out_even_b))
    x, rg_cst, rg_hst = _rglru(x, mod_p[1], w_in_odd_b, rg_conv_w[0], rg_conv_b, rg_w_a_b, rg_b_a,
                               rg_w_x_b, rg_b_x, rg_lambda, w_out_odd_b)
    y_p, ffn_st1, b_s, kt_n, vt_n = _ffn(x, mod_p[1], 1, w_up_b, ffn_conv_w, conv_b3, w_down_b,
                                         final_g2, True, att=(q_s, k_s, v_s, kt_c, vt_c))

    y_prompt = y_p.reshape(BATCH, SEQ, D_MODEL)
    to_win = lambda t: t.reshape(1, BATCH, N_HEADS, HD, WIN).transpose(0, 1, 4, 2, 3)
    win_k_prompt = to_win(kt_p)
    win_v_prompt = to_win(vt_p)
    rglru_conv_prompt = rg_cst[None, :, SUBLANES - 3:, :]
    rglru_h_prompt = rg_hst.reshape(1, BATCH, D_RNN)
    ffn_conv_prompt = jnp.stack([ffn_st0[:, SUBLANES - 2:, :], ffn_st1[:, SUBLANES - 2:, :]])

    xs = _s_even_out(xs, mod_s[0], a_s, b_s, w_out_even_b)
    st0 = state_ffn_conv[0]
    xs, up0 = _s_ffn(xs, mod_s[0], st0[:, 0], st0[:, 1], 0, w_up_b, ffn_conv_w, conv_b3, w_down_b,
                     final_g2, False)
    cst = state_rglru_conv[0]
    xs, xr_s, hn_s = _s_rglru(xs, mod_s[1], cst[:, 0], cst[:, 1], cst[:, 2], state_rglru_h[0],
                              w_in_odd_b, rg_conv_w[0], rg_conv_b, rg_w_a_b, rg_b_a, rg_w_x_b,
                              rg_b_x, rg_lambda, w_out_odd_b)
    st1 = state_ffn_conv[1]
    ys, up1 = _s_ffn(xs, mod_s[1], st1[:, 0], st1[:, 1], 1, w_up_b, ffn_conv_w, conv_b3, w_down_b,
                     final_g2, True)

    y_sample = ys.reshape(DEC_BATCH, 1, D_MODEL)
    chunk_v_sample = va_s.reshape(1, DEC_BATCH, 1, W_A)
    win_k_sample = kt_n.transpose(0, 3, 1, 2)[None]
    win_v_sample = vt_n.transpose(0, 3, 1, 2)[None]
    rglru_conv_sample = jnp.stack([cst[:, 1], cst[:, 2], xr_s], axis=1)[None]
    rglru_h_sample = hn_s[None]
    ffn_conv_sample = jnp.stack([jnp.stack([st0[:, 1], up0], axis=1),
                                 jnp.stack([st1[:, 1], up1], axis=1)])

    return (y_prompt, y_sample, win_k_prompt, win_v_prompt, rglru_conv_prompt, rglru_h_prompt,
            ffn_conv_prompt, chunk_v_sample, win_k_sample, win_v_sample, rglru_conv_sample,
            rglru_h_sample, ffn_conv_sample)
```

```python
import functools

import jax
import jax.numpy as jnp
from jax import lax
from jax.experimental import pallas as pl
from jax.experimental.pallas import tpu as pltpu

F32 = jnp.float32
BF16 = jnp.bfloat16

D_MODEL = 1024
BATCH = 4
SEQ = 4096
DEC_BATCH = 32
W_A = 512
A_GROUP = 128
G_A = 4
CHUNK = 128
W_B = 512
HD = 64
N_HEADS = 8
DILATIONS = (1, 4, 16)
N_BACK = 128
WIN = 2048
N_IN_EVEN = 2 * W_A + 3 * W_B
D_RNN = 1024
RG_BLOCK = 128
RG_HEADS = 8
RG_C = 8.0
D_FF = 2816
EPS = 1e-6
NEG_INF = -1e30
LOG2_E = 1.4426950408889634
LN_2 = 0.6931471805599453

LANES = 128
SUBLANES = 8
TM = 512
TILES_PER_SEQ = SEQ // TM
N_TILES = BATCH * TILES_PER_SEQ
FF_CHUNK = 256
N_FF_CHUNKS = D_FF // FF_CHUNK
N_SLABS = W_B // LANES
SEQ_PAIR = 2
VMEM_LIMIT = 56 * 1024 * 1024


def _cparams(*sem):
    return pltpu.CompilerParams(dimension_semantics=sem, vmem_limit_bytes=VMEM_LIMIT)


def _const_spec(shape):
    nd = len(shape)
    return pl.BlockSpec(shape, lambda *_: (0,) * nd, pipeline_mode=pl.Buffered(1))


def _layer_spec(shape, layer):
    nd = len(shape)
    return pl.BlockSpec((None,) + tuple(shape), lambda *_: (layer,) + (0,) * nd,
                        pipeline_mode=pl.Buffered(1))


def _shift_rows(cur, tail, k):
    rolled = pltpu.roll(cur, k, 0)
    r = lax.broadcasted_iota(jnp.int32, (SUBLANES, cur.shape[1]), 0)
    head = jnp.where(r < k, pltpu.roll(tail, k, 0), rolled[0:SUBLANES])
    return jnp.concatenate([head, rolled[SUBLANES:]], axis=0)


def _rms_mod(x, scale, shift):
    xn = x * lax.rsqrt(jnp.mean(x * x, axis=-1, keepdims=True) + EPS)
    return xn * (1.0 + scale) + shift


def _rmsnorm(x):
    return x * lax.rsqrt(jnp.mean(x * x, axis=-1, keepdims=True) + EPS)


def _layernorm(x, g, b):
    mu = jnp.mean(x, axis=-1, keepdims=True)
    xc = x - mu
    var = jnp.mean(xc * xc, axis=-1, keepdims=True)
    return xc * lax.rsqrt(var + EPS) * g + b


def _dot(a, b):
    return jnp.dot(a, b, preferred_element_type=F32)


ADA_ROWS = 40
ADA_TN = 1024


def _ada_kernel(c_ref, w_ref, b_ref, o_ref):
    c = c_ref[...]
    s = (c * jax.nn.sigmoid(c)).astype(BF16)
    o_ref[0] = _dot(s, w_ref[0].astype(BF16)) + b_ref[0]


def _ada(c_all, w_ada, b_ada):
    depth = w_ada.shape[0]
    n_out = w_ada.shape[2]
    return pl.pallas_call(
        _ada_kernel,
        grid=(depth, n_out // ADA_TN),
        in_specs=[
            pl.BlockSpec((ADA_ROWS, D_MODEL), lambda l, j: (0, 0)),
            pl.BlockSpec((1, D_MODEL, ADA_TN), lambda l, j: (l, 0, j)),
            pl.BlockSpec((1, 1, ADA_TN), lambda l, j: (l, 0, j)),
        ],
        out_specs=pl.BlockSpec((1, ADA_ROWS, ADA_TN), lambda l, j: (l, 0, j)),
        out_shape=jax.ShapeDtypeStruct((depth, ADA_ROWS, n_out), F32),
        compiler_params=_cparams("arbitrary", "arbitrary"),
        name="ada_mod",
    )(c_all, w_ada, b_ada.reshape(depth, 1, n_out))


def _even_in_kernel(x_ref, mod_ref, w_ref, lng_ref, lnb_ref, ws_ref, bst_ref,
                    a_ref, q_ref, k_ref, v_ref, kt_ref, vt_ref):
    base = 2 * W_A
    row = lax.broadcasted_iota(jnp.int32, (CHUNK, CHUNK), 0)
    col = lax.broadcasted_iota(jnp.int32, (CHUNK, CHUNK), 1)
    causal = col <= row
    bst = bst_ref[...]

    def front(u):
        m = mod_ref[u]
        h = _rms_mod(x_ref[u], m[1:2], m[0:1]).astype(BF16)
        va = _layernorm(jax.nn.gelu(_dot(h, w_ref[:, W_A:2 * W_A])), lng_ref[...], lnb_ref[...])
        ua = jax.nn.gelu(_dot(h, w_ref[:, 0:W_A]))
        q = _dot(h, w_ref[:, base:base + W_B])
        k = _dot(h, w_ref[:, base + W_B:base + 2 * W_B])
        for s in range(N_SLABS):
            q_ref[s, u] = q[:, s * LANES:(s + 1) * LANES]
            k_ref[s, u] = k[:, s * LANES:(s + 1) * LANES]
        return h, va.astype(BF16), ua, k

    def back(u, h, vab, ua):
        for g in range(G_A):
            wg = jnp.where(causal, ws_ref[g], 0.0).astype(BF16)
            bias = bst[:, g:g + 1]
            lo, hi = g * A_GROUP, (g + 1) * A_GROUP
            for c in range(TM // CHUNK):
                r0, r1 = c * CHUNK, (c + 1) * CHUNK
                mix = _dot(wg, vab[r0:r1, lo:hi]) + bias
                a_ref[u, r0:r1, lo:hi] = (ua[r0:r1, lo:hi] * mix).astype(BF16)
        v = _dot(h, w_ref[:, base + 2 * W_B:base + 3 * W_B])
        for s in range(N_SLABS):
            v_ref[s, u] = v[:, s * LANES:(s + 1) * LANES]
        return v

    fronts = [front(u) for u in range(SEQ_PAIR)]
    vs = [back(u, fronts[u][0], fronts[u][1], fronts[u][2]) for u in range(SEQ_PAIR)]
    for u in range(SEQ_PAIR):
        kt_ref[u] = fronts[u][3].T
        vt_ref[u] = vs[u].T


def _even_in(x2d, mod, w_in, ln_g, ln_b, w_s, b_s_t):
    n_tok = x2d.shape[0]
    first_win_tile = TILES_PER_SEQ - WIN // TM
    pair_tile = lambda i: (i // TILES_PER_SEQ, i % TILES_PER_SEQ, 0)
    qkv_shape = jax.ShapeDtypeStruct((N_SLABS, BATCH, SEQ, LANES), F32)
    qkv_spec = pl.BlockSpec((N_SLABS, SEQ_PAIR, TM, LANES),
                            lambda i: (0, i // TILES_PER_SEQ, i % TILES_PER_SEQ, 0))
    win_shape = jax.ShapeDtypeStruct((BATCH, W_B, WIN), F32)
    win_spec = pl.BlockSpec(
        (SEQ_PAIR, W_B, TM),
        lambda i: (i // TILES_PER_SEQ, 0, jnp.maximum(i % TILES_PER_SEQ - first_win_tile, 0)))
    a_out, q, k, v, kt, vt = pl.pallas_call(
        _even_in_kernel,
        grid=(N_TILES // SEQ_PAIR,),
        in_specs=[
            pl.BlockSpec((SEQ_PAIR, TM, D_MODEL), pair_tile),
            pl.BlockSpec((SEQ_PAIR, 6, D_MODEL), lambda i: (i // TILES_PER_SEQ, 0, 0)),
            _const_spec((D_MODEL, N_IN_EVEN)),
            _const_spec((1, W_A)),
            _const_spec((1, W_A)),
            _const_spec((G_A, CHUNK, CHUNK)),
            _const_spec((CHUNK, G_A)),
        ],
        out_specs=[
            pl.BlockSpec((SEQ_PAIR, TM, W_A), pair_tile),
            qkv_spec, qkv_spec, qkv_spec, win_spec, win_spec,
        ],
        out_shape=[
            jax.ShapeDtypeStruct((BATCH, SEQ, W_A), BF16),
            qkv_shape, qkv_shape, qkv_shape, win_shape, win_shape,
        ],
        compiler_params=_cparams("arbitrary"),
        name="even_in",
    )(x2d.reshape(BATCH, SEQ, D_MODEL), mod, w_in, ln_g, ln_b, w_s, b_s_t)
    flat = lambda a: a.reshape(N_SLABS, n_tok, LANES)
    return a_out.reshape(n_tok, W_A), flat(q), flat(k), flat(v), kt, vt


ATT_BLK = 128
ATT_UNROLL = 16


def _attn_kernel(q_ref, k_ref, v_ref, o_ref, acc_ref, max_ref, den_ref, q4_ref, k4_ref, v4_ref):
    q2, k2, v2 = q_ref.at[0], k_ref.at[0], v_ref.at[0]
    seq4 = SEQ // 4
    for src, dst in ((q2, q4_ref), (k2, k4_ref), (v2, v4_ref)):
        for c0 in range(4):
            for r in range(0, seq4, 2 * ATT_BLK):
                dst[c0 * seq4 + r:c0 * seq4 + r + 2 * ATT_BLK, :] = (
                    src[pl.ds(c0 + 4 * r, 2 * ATT_BLK, stride=4), :])
    qi = lax.broadcasted_iota(jnp.int32, (ATT_BLK, 2 * ATT_BLK), 0)
    kj = lax.broadcasted_iota(jnp.int32, (ATT_BLK, 2 * ATT_BLK), 1)
    dist = qi + ATT_BLK - kj
    band = (dist >= 0) & (dist <= N_BACK)
    bias_full = jnp.where(band, 0.0, NEG_INF).astype(F32)
    bias_first = jnp.where(band & (kj >= ATT_BLK), 0.0, NEG_INF).astype(F32)
    lane = lax.broadcasted_iota(jnp.int32, (ATT_BLK, LANES), 1)
    head0 = lane < HD

    for p, d in enumerate(DILATIONS):
        n_blk = SEQ // (d * ATT_BLK)

        def unit(idx, carry, p=p, d=d, n_blk=n_blk):
            k_prev, v_prev = carry
            c = idx // n_blk
            b = idx % n_blk
            if d == 1:
                qs, ks, vs = q2, k2, v2
                rows = pl.ds(pl.multiple_of(ATT_BLK * b, ATT_BLK), ATT_BLK)
            elif d == 4:
                qs, ks, vs = q4_ref, k4_ref, v4_ref
                rows = pl.ds(pl.multiple_of(c * seq4 + ATT_BLK * b, ATT_BLK), ATT_BLK)
            else:
                qs, ks, vs = q4_ref, k4_ref, v4_ref
                rows = pl.ds((c % 4) * seq4 + c // 4 + 4 * ATT_BLK * b, ATT_BLK, stride=4)
            qb = qs[rows, :] * (HD ** -0.5 * LOG2_E)
            k_own = ks[rows, :].astype(BF16)
            v_own = vs[rows, :].astype(BF16)
            kb = jnp.concatenate([k_prev, k_own], axis=0)
            vb = jnp.concatenate([v_prev, v_own], axis=0)
            bias = jnp.where(b == 0, bias_first, bias_full)
            outs, mxs, dens = [], [], []
            for hh in range(2):
                sel = head0 if hh == 0 else jnp.logical_not(head0)
                qm = jnp.where(sel, qb, 0.0).astype(BF16)
                s = lax.dot_general(qm, kb, (((1,), (1,)), ((), ())),
                                    preferred_element_type=F32) + bias
                mx = jnp.max(s, axis=-1, keepdims=True)
                e = jnp.exp2(s - mx)
                dens.append(jnp.sum(e, axis=-1, keepdims=True))
                mxs.append(mx)
                outs.append(_dot(e.astype(BF16), vb))
            acc_ref[p, rows, :] = jnp.where(head0, outs[0], outs[1])
            max_ref[p, rows, :] = jnp.where(head0, mxs[0], mxs[1])
            den_ref[p, rows, :] = jnp.where(head0, dens[0], dens[1])
            return k_own, v_own

        zeros = jnp.zeros((ATT_BLK, LANES), BF16)
        lax.fori_loop(0, SEQ // ATT_BLK, unit, (zeros, zeros), unroll=ATT_UNROLL)

    def mix(t, carry):
        c0 = t // (seq4 // TM)
        r = (t % (seq4 // TM)) * TM
        rows4 = pl.ds(pl.multiple_of(c0 * seq4 + r, TM), TM)
        rows = pl.ds(c0 + 4 * r, TM, stride=4)
        m0, m1, m2 = max_ref[0, rows, :], max_ref[1, rows4, :], max_ref[2, rows4, :]
        mx = jnp.maximum(jnp.maximum(m0, m1), m2)
        e0, e1, e2 = jnp.exp2(m0 - mx), jnp.exp2(m1 - mx), jnp.exp2(m2 - mx)
        num = e0 * acc_ref[0, rows, :] + e1 * acc_ref[1, rows4, :] + e2 * acc_ref[2, rows4, :]
        den = e0 * den_ref[0, rows, :] + e1 * den_ref[1, rows4, :] + e2 * den_ref[2, rows4, :]
        acc_ref[0, rows, :] = num / den
        return carry

    lax.fori_loop(0, SEQ // TM, mix, 0)

    def emit(t, carry):
        rows = pl.ds(pl.multiple_of(t * TM, TM), TM)
        o_ref[rows, :] = acc_ref[0, rows, :].astype(BF16)
        return carry

    lax.fori_loop(0, SEQ // TM, emit, 0)


def _attn(q, k, v):
    n_tok = q.shape[1]
    spec = pl.BlockSpec((1, SEQ, LANES), lambda n, s: (s, n, 0))
    return pl.pallas_call(
        _attn_kernel,
        grid=(BATCH, N_SLABS),
        in_specs=[spec, spec, spec],
        out_specs=pl.BlockSpec((SEQ, LANES), lambda n, s: (n, s)),
        out_shape=jax.ShapeDtypeStruct((n_tok, W_B), BF16),
        scratch_shapes=([pltpu.VMEM((len(DILATIONS), SEQ, LANES), F32)] * 3
                        + [pltpu.VMEM((SEQ, LANES), F32)] * 3),
        compiler_params=_cparams("arbitrary", "arbitrary"),
        name="dil_attn",
    )(q, k, v)


S_ATT_HEADS = 4
S_ATT_PHASES = N_HEADS // S_ATT_HEADS
FF_SPLIT = 6


def _ffn_kernel(*refs, pre, final, att):
    refs = list(refs)
    x_ref, mod_ref = refs[0:2]
    del refs[0:2]
    if pre:
        a_ref, b_ref, wo_ref = refs[0:3]
        del refs[0:3]
    wup_ref, cw_ref, cb_ref, wdn_ref, fg_ref = refs[0:5]
    del refs[0:5]
    if att:
        (qs_ref, kn_ref, vn_ref, kt_hbm, vt_hbm, o_ref, st_ref, bs_ref, kto_hbm, vto_hbm,
         carry_ref, act_ref, kin_ref, vin_ref, kout_ref, vout_ref, in_sem, out_sem) = refs
    else:
        o_ref, st_ref, carry_ref, act_ref = refs
    i = pl.program_id(0)
    last = pl.num_programs(0) - 1

    @pl.when(i % TILES_PER_SEQ == 0)
    def _():
        carry_ref[...] = jnp.zeros_like(carry_ref)

    def copies_in(seq, ph):
        hs = pl.ds(ph * S_ATT_HEADS, S_ATT_HEADS)
        return (pltpu.make_async_copy(kt_hbm.at[seq, hs], kin_ref.at[ph], in_sem.at[ph, 0]),
                pltpu.make_async_copy(vt_hbm.at[seq, hs], vin_ref.at[ph], in_sem.at[ph, 1]))

    def copies_out(seq, ph):
        hs = pl.ds(ph * S_ATT_HEADS, S_ATT_HEADS)
        return (pltpu.make_async_copy(kout_ref.at[ph], kto_hbm.at[seq, hs], out_sem.at[ph, 0]),
                pltpu.make_async_copy(vout_ref.at[ph], vto_hbm.at[seq, hs], out_sem.at[ph, 1]))

    def start(cps):
        for c in cps:
            c.start()

    def wait(cps):
        for c in cps:
            c.wait()

    def window_phase(ph):
        lanes = slice(ph * S_ATT_HEADS * HD, (ph + 1) * S_ATT_HEADS * HD)
        row = pl.ds(i, 1)
        o_row, kt_new, vt_new = _s_attn_heads(qs_ref[row, lanes], kn_ref[row, lanes],
                                              vn_ref[row, lanes], kin_ref[ph], vin_ref[ph])
        bs_ref[row, lanes] = o_row
        kout_ref[ph] = kt_new
        vout_ref[ph] = vt_new

    if att:
        @pl.when(i == 0)
        def _():
            kout_ref[1] = jnp.zeros(kout_ref.shape[1:], F32)
            vout_ref[1] = jnp.zeros(vout_ref.shape[1:], F32)
            start(copies_in(0, 0))
            start(copies_out(0, 1))

        wait(copies_in(i, 0))
        start(copies_in(i, 1))

        @pl.when(i > 0)
        def _():
            wait(copies_out(i - 1, 0))

        window_phase(0)

    m = mod_ref[0]
    cw = cw_ref[...]
    cb = cb_ref[...]
    x = x_ref[...]
    if pre:
        x = x + m[2:3] * (_dot(a_ref[...], wo_ref[0:W_A, :]) + _dot(b_ref[...], wo_ref[W_A:W_A + W_B, :]))
    h = _rms_mod(x, m[4:5], m[3:4]).astype(BF16)
    for j in range(N_FF_CHUNKS):
        if att and j == FF_SPLIT:
            start(copies_out(i, 0))
            wait(copies_in(i, 1))
            start(copies_in(jnp.minimum(i + 1, last), 0))
            wait(copies_out(jnp.maximum(i - 1, 0), 1))
            window_phase(1)
        conv = []
        for half in range(2):
            c0 = half * D_FF + j * FF_CHUNK
            cols = slice(c0, c0 + FF_CHUNK)
            up = _dot(h, wup_ref[:, cols])
            tail = carry_ref[:, cols]
            carry_ref[:, cols] = up[TM - SUBLANES:, :]
            conv.append(cb[:, cols] + cw[0:1, cols] * _shift_rows(up, tail, 2)
                        + cw[1:2, cols] * _shift_rows(up, tail, 1) + cw[2:3, cols] * up)
        act = conv[0] * jax.nn.sigmoid(conv[0]) * conv[1]
        act_ref[:, j * FF_CHUNK:(j + 1) * FF_CHUNK] = act.astype(BF16)
    st_ref[0] = carry_ref[...]
    y = x + m[5:6] * _dot(act_ref[...], wdn_ref[...])
    if final:
        y = _rmsnorm(y) * fg_ref[...]
    o_ref[...] = y

    if att:
        start(copies_out(i, 1))

        @pl.when(i == last)
        def _():
            wait(copies_out(i, 0))
            wait(copies_out(i, 1))
            wait(copies_in(i, 0))


def _ffn(x2d, mod, layer, w_up, conv_w, conv_b, w_down, final_g, final, pre=None, att=None):
    n_tok = x2d.shape[0]
    f2 = 2 * D_FF
    row_spec = pl.BlockSpec((TM, D_MODEL), lambda i: (i, 0))
    mod_spec = pl.BlockSpec((1, 6, D_MODEL), lambda i: (i // TILES_PER_SEQ, 0, 0))
    args, specs = [x2d, mod], [row_spec, mod_spec]
    if pre is not None:
        args += list(pre)
        specs += [pl.BlockSpec((TM, W_A), lambda i: (i, 0)),
                  pl.BlockSpec((TM, W_B), lambda i: (i, 0)),
                  _const_spec((W_A + W_B, D_MODEL))]
    args += [w_up, conv_w, conv_b, w_down, final_g]
    specs += [_layer_spec((D_MODEL, f2), layer), _layer_spec((3, f2), layer),
              _layer_spec((1, f2), layer), _layer_spec((D_FF, D_MODEL), layer),
              _const_spec((1, D_MODEL))]
    out_specs = [row_spec, pl.BlockSpec((1, SUBLANES, f2), lambda i: (i // TILES_PER_SEQ, 0, 0))]
    out_shape = [jax.ShapeDtypeStruct((n_tok, D_MODEL), F32),
                 jax.ShapeDtypeStruct((BATCH, SUBLANES, f2), F32)]
    scratch = [pltpu.VMEM((SUBLANES, f2), F32), pltpu.VMEM((TM, D_FF), BF16)]
    if att is not None:
        q_s, kt = att[0], att[3]
        assert q_s.shape[0] == n_tok // TM, "one sample sequence per grid step"
        any_spec = pl.BlockSpec(memory_space=pl.ANY)
        args += list(att)
        specs += [_const_spec(q_s.shape)] * 3 + [any_spec, any_spec]
        out_specs += [pl.BlockSpec(q_s.shape, lambda i: (0, 0)), any_spec, any_spec]
        out_shape += [jax.ShapeDtypeStruct(q_s.shape, F32),
                      jax.ShapeDtypeStruct(kt.shape, F32), jax.ShapeDtypeStruct(kt.shape, F32)]
        win_buf = pltpu.VMEM((S_ATT_PHASES, S_ATT_HEADS, HD, WIN), F32)
        scratch += [win_buf] * 4 + [pltpu.SemaphoreType.DMA((S_ATT_PHASES, 2))] * 2
    return pl.pallas_call(
        functools.partial(_ffn_kernel, pre=pre is not None, final=final, att=att is not None),
        grid=(n_tok // TM,),
        in_specs=specs,
        out_specs=out_specs,
        out_shape=out_shape,
        scratch_shapes=scratch,
        compiler_params=_cparams("arbitrary"),
        name="conv_ffn_final" if final else "conv_ffn",
    )(*args)


SEG_LEN = TM // SUBLANES
RG_PAIR = 2


def _rg_gates(xc, wa_ref, ba, wx_ref, bx, lam):
    xcb = xc.astype(BF16)
    r_parts, i_parts = [], []
    for g in range(RG_HEADS):
        blk = xcb[:, g * RG_BLOCK:(g + 1) * RG_BLOCK]
        r_parts.append(_dot(blk, wa_ref[g]))
        i_parts.append(_dot(blk, wx_ref[g]))
    r = jax.nn.sigmoid(jnp.concatenate(r_parts, axis=-1) + ba)
    ig = jax.nn.sigmoid(jnp.concatenate(i_parts, axis=-1) + bx)
    log_a = r * ((-RG_C) * jax.nn.softplus(-lam))
    a = jnp.exp(log_a)
    b = jnp.sqrt(1.0 - a * a) * (ig * xc)
    return a, b


def _rglru_kernel(x_ref, mod_ref, perm_ref, unperm_ref, win_ref, cw_ref, cb_ref, wa_ref, ba_ref,
                  wx_ref, bx_ref, lam_ref, wout_ref, o_ref, cst_ref, hst_ref,
                  xtail_ref, hcar_ref, a_ref, b_ref, hs_ref, ac_ref):
    t = pl.program_id(0) % TILES_PER_SEQ

    @pl.when(t == 0)
    def _():
        xtail_ref[...] = jnp.zeros_like(xtail_ref)
        hcar_ref[...] = jnp.zeros_like(hcar_ref)

    sub = lax.broadcasted_iota(jnp.int32, (SUBLANES, D_RNN), 0)
    cw = cw_ref[...]

    def front(u):
        m = mod_ref[u]
        h = _rms_mod(x_ref[u], m[1:2], m[0:1]).astype(BF16)
        hp = _dot(perm_ref[...], h).astype(BF16)
        xr3 = _dot(hp, win_ref[:, D_RNN:2 * D_RNN]).reshape(SEG_LEN, SUBLANES, D_RNN)
        gate = _dot(hp, win_ref[:, 0:D_RNN])
        tail = xtail_ref[u]
        xc3 = cb_ref[...] + cw[3:4] * xr3
        for k in range(1, 4):
            wrap = [jnp.where(sub == 0, tail[SUBLANES - k + v:SUBLANES - k + v + 1, :],
                              pltpu.roll(xr3[SEG_LEN - k + v], 1, 0)) for v in range(k)]
            prev = jnp.concatenate([jnp.stack(wrap), xr3[0:SEG_LEN - k]], axis=0)
            xc3 = xc3 + cw[3 - k:4 - k] * prev
        last = jnp.zeros((SUBLANES, D_RNN), F32)
        for r in range(SUBLANES):
            last = jnp.where(sub == r, pltpu.roll(xr3[SEG_LEN - SUBLANES + r], r + 1, 0), last)
        xtail_ref[u] = last
        cst_ref[u] = last
        a, b = _rg_gates(xc3.reshape(TM, D_RNN), wa_ref, ba_ref[...], wx_ref, bx_ref[...],
                         lam_ref[...])
        a_ref[u] = a.reshape(SEG_LEN, SUBLANES, D_RNN)
        b_ref[u] = b.reshape(SEG_LEN, SUBLANES, D_RNN)
        return gate

    def scan(u):
        def step(v, carry):
            hl, ac = carry
            av = a_ref[u, v]
            hl = av * hl + b_ref[u, v]
            ac = av * ac
            hs_ref[u, v] = hl
            ac_ref[u, v] = ac
            return hl, ac

        zero = jnp.zeros((SUBLANES, D_RNN), F32)
        hl, ac = lax.fori_loop(0, SEG_LEN, step, (zero, zero + 1.0), unroll=True)
        h_in = jnp.where(sub == 0, hcar_ref[u], 0.0)
        for s in range(1, SUBLANES):
            h_in = jnp.where(sub == s, pltpu.roll(hl + ac * h_in, 1, 0), h_in)
        h_end = (hl + ac * h_in)[SUBLANES - 1:SUBLANES, :]
        hcar_ref[u] = h_end
        hst_ref[u] = h_end
        return hs_ref[u] + ac_ref[u] * h_in

    def back(u, gate, hs):
        y = (jax.nn.gelu(gate) * hs.reshape(TM, D_RNN)).astype(BF16)
        y = _dot(unperm_ref[...], y).astype(BF16)
        o_ref[u] = x_ref[u] + mod_ref[u][2:3] * _dot(y, wout_ref[...])

    gates = [front(u) for u in range(RG_PAIR)]
    for u in range(RG_PAIR):
        back(u, gates[u], scan(u))


def _rglru(x2d, mod, w_in, conv_w, conv_b, w_a, b_a, w_x, b_x, lam, w_out):
    n_tok = x2d.shape[0]
    x3 = x2d.reshape(BATCH, SEQ, D_MODEL)
    rho = jnp.arange(TM)
    perm = (jnp.arange(TM)[None, :] == ((rho % SUBLANES) * SEG_LEN + rho // SUBLANES)[:, None])
    perm = perm.astype(BF16)
    row_spec = pl.BlockSpec((RG_PAIR, TM, D_MODEL),
                            lambda i: (i // TILES_PER_SEQ, i % TILES_PER_SEQ, 0))

    def seq_spec(rows):
        return pl.BlockSpec((RG_PAIR, rows, D_RNN), lambda i: (i // TILES_PER_SEQ, 0, 0))

    out, cst, hst = pl.pallas_call(
        _rglru_kernel,
        grid=(n_tok // (RG_PAIR * TM),),
        in_specs=[
            row_spec,
            seq_spec(6),
            _const_spec((TM, TM)),
            _const_spec((TM, TM)),
            _const_spec((D_MODEL, 2 * D_RNN)),
            _const_spec((4, D_RNN)),
            _const_spec((1, D_RNN)),
            _const_spec((RG_HEADS, RG_BLOCK, RG_BLOCK)),
            _const_spec((1, D_RNN)),
            _const_spec((RG_HEADS, RG_BLOCK, RG_BLOCK)),
            _const_spec((1, D_RNN)),
            _const_spec((1, D_RNN)),
            _const_spec((D_RNN, D_MODEL)),
        ],
        out_specs=[row_spec, seq_spec(SUBLANES), seq_spec(1)],
        out_shape=[
            jax.ShapeDtypeStruct((BATCH, SEQ, D_MODEL), F32),
            jax.ShapeDtypeStruct((BATCH, SUBLANES, D_RNN), F32),
            jax.ShapeDtypeStruct((BATCH, 1, D_RNN), F32),
        ],
        scratch_shapes=[
            pltpu.VMEM((RG_PAIR, SUBLANES, D_RNN), F32),
            pltpu.VMEM((RG_PAIR, 1, D_RNN), F32),
        ] + [pltpu.VMEM((RG_PAIR, SEG_LEN, SUBLANES, D_RNN), F32)] * 4,
        compiler_params=_cparams("arbitrary"),
        name="rglru",
    )(x3, mod, perm, perm.T, w_in, conv_w, conv_b, w_a, b_a, w_x, b_x, lam, w_out)
    return out.reshape(n_tok, D_MODEL), cst, hst


def _s_even_in_kernel(x_ref, mod_ref, w_ref, lng_ref, lnb_ref, ws0_ref, bs0_ref,
                      va_ref, a_ref, q_ref, k_ref, v_ref):
    mod = mod_ref[...]
    h = _rms_mod(x_ref[...], mod[:, D_MODEL:2 * D_MODEL], mod[:, 0:D_MODEL]).astype(BF16)
    u = jax.nn.gelu(_dot(h, w_ref[:, 0:W_A]))
    va = _layernorm(jax.nn.gelu(_dot(h, w_ref[:, W_A:2 * W_A])), lng_ref[...], lnb_ref[...])
    va_ref[...] = va
    a_ref[...] = (u * (ws0_ref[...] * va + bs0_ref[...])).astype(BF16)
    base = 2 * W_A
    q_ref[...] = _dot(h, w_ref[:, base:base + W_B])
    k_ref[...] = _dot(h, w_ref[:, base + W_B:base + 2 * W_B])
    v_ref[...] = _dot(h, w_ref[:, base + 2 * W_B:base + 3 * W_B])


def _s_even_in(x, mod, w_in, ln_g, ln_b, ws0, bs0):
    n = x.shape[0]
    f = jax.ShapeDtypeStruct((n, W_B), F32)
    return pl.pallas_call(
        _s_even_in_kernel,
        out_shape=[f, jax.ShapeDtypeStruct((n, W_A), BF16), f, f, f],
        compiler_params=pltpu.CompilerParams(vmem_limit_bytes=VMEM_LIMIT),
        name="s_even_in",
    )(x, mod, w_in, ln_g, ln_b, ws0, bs0)


def _s_attn_heads(qrow, knrow, vnrow, kt3, vt3):
    nh = kt3.shape[0]
    n_hd = nh * HD
    qrow = qrow * (HD ** -0.5)
    hrow = lax.broadcasted_iota(jnp.int32, (SUBLANES, n_hd), 0)
    hcol = lax.broadcasted_iota(jnp.int32, (SUBLANES, n_hd), 1) // HD
    own = hrow == hcol
    qbd = jnp.where(own, jnp.broadcast_to(qrow, (SUBLANES, n_hd)), 0.0)
    kt = kt3.reshape(n_hd, WIN)
    vt = vt3.reshape(n_hd, WIN)
    s = _dot(qbd.astype(BF16), kt.astype(BF16))
    s_new = jnp.sum(qbd * knrow, axis=-1, keepdims=True)
    t = lax.broadcasted_iota(jnp.int32, (SUBLANES, WIN), 1)
    dist = WIN - t
    vtb = vt.astype(BF16)
    outs, lses = [], []
    for d in DILATIONS:
        valid = ((dist & (d - 1)) == 0) & (dist <= N_BACK * d)
        sm = jnp.where(valid, s, NEG_INF)
        mx = jnp.maximum(jnp.max(sm, axis=-1, keepdims=True), s_new)
        e = jnp.where(valid, jnp.exp(sm - mx), 0.0)
        e_new = jnp.exp(s_new - mx)
        den = jnp.sum(e, axis=-1, keepdims=True) + e_new
        o = lax.dot_general(e.astype(BF16), vtb, (((1,), (1,)), ((), ())),
                            preferred_element_type=F32)
        outs.append((o + e_new * vnrow) / den)
        lses.append(mx + jnp.log(den))
    mx = jnp.maximum(jnp.maximum(lses[0], lses[1]), lses[2])
    ws = [jnp.exp(l - mx) for l in lses]
    num = ws[0] * outs[0] + ws[1] * outs[1] + ws[2] * outs[2]
    mixed = num / (ws[0] + ws[1] + ws[2])
    o_row = jnp.sum(jnp.where(own, mixed, 0.0), axis=0, keepdims=True)

    r2 = lax.broadcasted_iota(jnp.int32, (n_hd, n_hd), 0)
    c2 = lax.broadcasted_iota(jnp.int32, (n_hd, n_hd), 1)
    diag = r2 == c2
    kcol = jnp.sum(jnp.where(diag, jnp.broadcast_to(knrow, (n_hd, n_hd)), 0.0), axis=-1, keepdims=True)
    vcol = jnp.sum(jnp.where(diag, jnp.broadcast_to(vnrow, (n_hd, n_hd)), 0.0), axis=-1, keepdims=True)
    tt = lax.broadcasted_iota(jnp.int32, (n_hd, WIN), 1)
    is_last = tt == WIN - 1
    kt_new = jnp.where(is_last, kcol, pltpu.roll(kt, WIN - 1, 1)).reshape(nh, HD, WIN)
    vt_new = jnp.where(is_last, vcol, pltpu.roll(vt, WIN - 1, 1)).reshape(nh, HD, WIN)
    return o_row, kt_new, vt_new


def _s_even_out_kernel(x_ref, mod_ref, a_ref, b_ref, w_ref, o_ref):
    g1 = mod_ref[:, 2 * D_MODEL:3 * D_MODEL]
    mo = _dot(a_ref[...], w_ref[0:W_A, :]) + _dot(b_ref[...].astype(BF16), w_ref[W_A:W_A + W_B, :])
    o_ref[...] = x_ref[...] + g1 * mo


def _s_even_out(x, mod, a_out, b_out, w_out):
    return pl.pallas_call(
        _s_even_out_kernel,
        out_shape=jax.ShapeDtypeStruct(x.shape, F32),
        compiler_params=pltpu.CompilerParams(vmem_limit_bytes=VMEM_LIMIT),
        name="s_even_out",
    )(x, mod, a_out, b_out, w_out)


def _s_ffn_kernel(x_ref, mod_ref, p2_ref, p1_ref, wup_ref, cw_ref, cb_ref, wdn_ref, fg_ref,
                  o_ref, up_ref, *, final):
    x = x_ref[...]
    mod = mod_ref[...]
    h = _rms_mod(x, mod[:, 4 * D_MODEL:5 * D_MODEL], mod[:, 3 * D_MODEL:4 * D_MODEL]).astype(BF16)
    up = _dot(h, wup_ref[...])
    up_ref[...] = up
    cw = cw_ref[...]
    conv = cb_ref[...] + cw[0:1] * p2_ref[...] + cw[1:2] * p1_ref[...] + cw[2:3] * up
    ca, cg = conv[:, 0:D_FF], conv[:, D_FF:2 * D_FF]
    act = (ca * jax.nn.sigmoid(ca) * cg).astype(BF16)
    y = x + mod[:, 5 * D_MODEL:6 * D_MODEL] * _dot(act, wdn_ref[...])
    if final:
        y = _rmsnorm(y) * fg_ref[...]
    o_ref[...] = y


def _s_ffn(x, mod, p2, p1, layer, w_up, conv_w, conv_b, w_down, final_g, final):
    n = x.shape[0]
    f2 = 2 * D_FF
    return pl.pallas_call(
        functools.partial(_s_ffn_kernel, final=final),
        grid=(1,),
        in_specs=[_const_spec(x.shape), _const_spec(mod.shape), _const_spec(p2.shape),
                  _const_spec(p1.shape), _layer_spec((D_MODEL, f2), layer),
                  _layer_spec((3, f2), layer), _layer_spec((1, f2), layer),
                  _layer_spec((D_FF, D_MODEL), layer), _const_spec((1, D_MODEL))],
        out_specs=[pl.BlockSpec(x.shape, lambda i: (0, 0)), pl.BlockSpec((n, f2), lambda i: (0, 0))],
        out_shape=[jax.ShapeDtypeStruct(x.shape, F32), jax.ShapeDtypeStruct((n, f2), F32)],
        compiler_params=_cparams("arbitrary"),
        name="s_conv_ffn_final" if final else "s_conv_ffn",
    )(x, mod, p2, p1, w_up, conv_w, conv_b, w_down, final_g)


def _s_rglru_kernel(x_ref, mod_ref, c0_ref, c1_ref, c2_ref, h0_ref, win_ref, cw_ref, cb_ref,
                    wa_ref, ba_ref, wx_ref, bx_ref, lam_ref, wout_ref, o_ref, xr_ref, hn_ref):
    x = x_ref[...]
    mod = mod_ref[...]
    h = _rms_mod(x, mod[:, D_MODEL:2 * D_MODEL], mod[:, 0:D_MODEL]).astype(BF16)
    gate = _dot(h, win_ref[:, 0:D_RNN])
    xr = _dot(h, win_ref[:, D_RNN:2 * D_RNN])
    xr_ref[...] = xr
    cw = cw_ref[...]
    xc = (cb_ref[...] + cw[0:1] * c0_ref[...] + cw[1:2] * c1_ref[...] + cw[2:3] * c2_ref[...]
          + cw[3:4] * xr)
    a, b = _rg_gates(xc, wa_ref, ba_ref[...], wx_ref, bx_ref[...], lam_ref[...])
    hn = a * h0_ref[...] + b
    hn_ref[...] = hn
    y = (jax.nn.gelu(gate) * hn).astype(BF16)
    o_ref[...] = x + mod[:, 2 * D_MODEL:3 * D_MODEL] * _dot(y, wout_ref[...])


def _s_rglru(x, mod, c0, c1, c2, h0, w_in, conv_w, conv_b, w_a, b_a, w_x, b_x, lam, w_out):
    f = jax.ShapeDtypeStruct(x.shape, F32)
    return pl.pallas_call(
        _s_rglru_kernel,
        out_shape=[f, f, f],
        compiler_params=pltpu.CompilerParams(vmem_limit_bytes=VMEM_LIMIT),
        name="s_rglru",
    )(x, mod, c0, c1, c2, h0, w_in, conv_w, conv_b, w_a, b_a, w_x, b_x, lam, w_out)


def kernel(x_prompt, x_sample, cache_win_k, cache_win_v, state_rglru_conv, state_rglru_h, state_ffn_conv, c_prompt, c_sample, w_ada, b_ada, w_in_even, ln_v_g, ln_v_b, w_sgu, b_sgu, w_out_even, w_in_odd, rg_conv_w, rg_conv_b, rg_w_a, rg_b_a, rg_w_x, rg_b_x, rg_lambda, w_out_odd, ffn_w_up, ffn_conv_w, ffn_conv_b, ffn_w_down, final_g):
    w_in_even_b = w_in_even[0].astype(BF16)
    w_out_even_b = w_out_even[0].astype(BF16)
    w_in_odd_b = w_in_odd[0].astype(BF16)
    w_out_odd_b = w_out_odd[0].astype(BF16)
    rg_w_a_b = rg_w_a[0].astype(BF16)
    rg_w_x_b = rg_w_x[0].astype(BF16)
    w_up_b = ffn_w_up.astype(BF16)
    w_down_b = ffn_w_down.astype(BF16)
    conv_b3 = ffn_conv_b.reshape(2, 1, 2 * D_FF)
    final_g2 = final_g.reshape(1, D_MODEL)

    pad = jnp.zeros((ADA_ROWS - BATCH - DEC_BATCH, D_MODEL), F32)
    mod = _ada(jnp.concatenate([c_prompt, c_sample, pad], axis=0), w_ada, b_ada)
    mod_p = mod[:, :BATCH].reshape(2, BATCH, 6, D_MODEL)
    mod_s = mod[:, BATCH:BATCH + DEC_BATCH]

    xs = x_sample.reshape(DEC_BATCH, D_MODEL)
    ws0 = jnp.repeat(w_sgu[0, :, 0, 0], A_GROUP).reshape(1, W_A)
    bs0 = jnp.repeat(b_sgu[0, :, 0], A_GROUP).reshape(1, W_A)
    va_s, a_s, q_s, k_s, v_s = _s_even_in(xs, mod_s[0], w_in_even_b, ln_v_g, ln_v_b, ws0, bs0)
    kt_c = cache_win_k[0].transpose(0, 2, 3, 1)
    vt_c = cache_win_v[0].transpose(0, 2, 3, 1)

    x = x_prompt.reshape(BATCH * SEQ, D_MODEL)
    a_out, q, k, v, kt_p, vt_p = _even_in(x, mod_p[0], w_in_even_b, ln_v_g, ln_v_b, w_sgu[0],
                                          b_sgu[0].T)
    b_out = _attn(q, k, v)
    x, ffn_st0 = _ffn(x, mod_p[0], 0, w_up_b, ffn_conv_w, conv_b3, w_down_b, final_g2, False,
                      pre=(a_out, b_out, w_out_even_b))
    x, rg_cst, rg_hst = _rglru(x, mod_p[1], w_in_odd_b, rg_conv_w[0], rg_conv_b, rg_w_a_b, rg_b_a,
                               rg_w_x_b, rg_b_x, rg_lambda, w_out_odd_b)
    y_p, ffn_st1, b_s, kt_n, vt_n = _ffn(x, mod_p[1], 1, w_up_b, ffn_conv_w, conv_b3, w_down_b,
                                         final_g2, True, att=(q_s, k_s, v_s, kt_c, vt_c))

    y_prompt = y_p.reshape(BATCH, SEQ, D_MODEL)
    to_win = lambda t: t.reshape(1, BATCH, N_HEADS, HD, WIN).transpose(0, 1, 4, 2, 3)
    win_k_prompt = to_win(kt_p)
    win_v_prompt = to_win(vt_p)
    rglru_conv_prompt = rg_cst[None, :, SUBLANES - 3:, :]
    rglru_h_prompt = rg_hst.reshape(1, BATCH, D_RNN)
    ffn_conv_prompt = jnp.stack([ffn_st0[:, SUBLANES - 2:, :], ffn_st1[:, SUBLANES - 2:, :]])

    xs = _s_even_out(xs, mod_s[0], a_s, b_s, w_out_even_b)
    st0 = state_ffn_conv[0]
    xs, up0 = _s_ffn(xs, mod_s[0], st0[:, 0], st0[:, 1], 0, w_up_b, ffn_conv_w, conv_b3, w_down_b,
                     final_g2, False)
    cst = state_rglru_conv[0]
    xs, xr_s, hn_s = _s_rglru(xs, mod_s[1], cst[:, 0], cst[:, 1], cst[:, 2], state_rglru_h[0],
                              w_in_odd_b, rg_conv_w[0], rg_conv_b, rg_w_a_b, rg_b_a, rg_w_x_b,
                              rg_b_x, rg_lambda, w_out_odd_b)
    st1 = state_ffn_conv[1]
    ys, up1 = _s_ffn(xs, mod_s[1], st1[:, 0], st1[:, 1], 1, w_up_b, ffn_conv_w, conv_b3, w_down_b,
                     final_g2, True)

    y_sample = ys.reshape(DEC_BATCH, 1, D_MODEL)
    chunk_v_sample = va_s.reshape(1, DEC_BATCH, 1, W_A)
    win_k_sample = kt_n.transpose(0, 3, 1, 2)[None]
    win_v_sample = vt_n.transpose(0, 3, 1, 2)[None]
    rglru_conv_sample = jnp.stack([cst[:, 1], cst[:, 2], xr_s], axis=1)[None]
    rglru_h_sample = hn_s[None]
    ffn_conv_sample = jnp.stack([jnp.stack([st0[:, 1], up0], axis=1),
                                 jnp.stack([st1[:, 1], up1], axis=1)])

    return (y_prompt, y_sample, win_k_prompt, win_v_prompt, rglru_conv_prompt, rglru_h_prompt,
            ffn_conv_prompt, chunk_v_sample, win_k_sample, win_v_sample, rglru_conv_sample,
            rglru_h_sample, ffn_conv_sample)
```

```python
import functools

import jax
import jax.numpy as jnp
from jax import lax
from jax.experimental import pallas as pl
from jax.experimental.pallas import tpu as pltpu

F32 = jnp.float32
BF16 = jnp.bfloat16

D_MODEL = 1024
BATCH = 4
SEQ = 4096
DEC_BATCH = 32
W_A = 512
A_GROUP = 128
G_A = 4
CHUNK = 128
W_B = 512
HD = 64
N_HEADS = 8
DILATIONS = (1, 4, 16)
N_BACK = 128
WIN = 2048
N_IN_EVEN = 2 * W_A + 3 * W_B
D_RNN = 1024
RG_BLOCK = 128
RG_HEADS = 8
RG_C = 8.0
D_FF = 2816
EPS = 1e-6
NEG_INF = -1e30
LOG2_E = 1.4426950408889634
LN_2 = 0.6931471805599453

LANES = 128
SUBLANES = 8
TM = 512
TILES_PER_SEQ = SEQ // TM
N_TILES = BATCH * TILES_PER_SEQ
FF_CHUNK = 256
N_FF_CHUNKS = D_FF // FF_CHUNK
N_SLABS = W_B // LANES
SEQ_PAIR = 2
VMEM_LIMIT = 56 * 1024 * 1024


def _cparams(*sem):
    return pltpu.CompilerParams(dimension_semantics=sem, vmem_limit_bytes=VMEM_LIMIT)


def _const_spec(shape):
    nd = len(shape)
    return pl.BlockSpec(shape, lambda *_: (0,) * nd, pipeline_mode=pl.Buffered(1))


def _layer_spec(shape, layer):
    nd = len(shape)
    return pl.BlockSpec((None,) + tuple(shape), lambda *_: (layer,) + (0,) * nd,
                        pipeline_mode=pl.Buffered(1))


def _shift_rows(cur, tail, k):
    rolled = pltpu.roll(cur, k, 0)
    r = lax.broadcasted_iota(jnp.int32, (SUBLANES, cur.shape[1]), 0)
    head = jnp.where(r < k, pltpu.roll(tail, k, 0), rolled[0:SUBLANES])
    return jnp.concatenate([head, rolled[SUBLANES:]], axis=0)


def _rms_mod(x, scale, shift):
    xn = x * lax.rsqrt(jnp.mean(x * x, axis=-1, keepdims=True) + EPS)
    return xn * (1.0 + scale) + shift


def _rmsnorm(x):
    return x * lax.rsqrt(jnp.mean(x * x, axis=-1, keepdims=True) + EPS)


def _layernorm(x, g, b):
    mu = jnp.mean(x, axis=-1, keepdims=True)
    xc = x - mu
    var = jnp.mean(xc * xc, axis=-1, keepdims=True)
    return xc * lax.rsqrt(var + EPS) * g + b


def _dot(a, b):
    return jnp.dot(a, b, preferred_element_type=F32)


ADA_ROWS = 40
ADA_TN = 1024


def _ada_kernel(c_ref, w_ref, b_ref, o_ref):
    c = c_ref[...]
    s = (c * jax.nn.sigmoid(c)).astype(BF16)
    o_ref[0] = _dot(s, w_ref[0].astype(BF16)) + b_ref[0]


def _ada(c_all, w_ada, b_ada):
    depth = w_ada.shape[0]
    n_out = w_ada.shape[2]
    return pl.pallas_call(
        _ada_kernel,
        grid=(depth, n_out // ADA_TN),
        in_specs=[
            pl.BlockSpec((ADA_ROWS, D_MODEL), lambda l, j: (0, 0)),
            pl.BlockSpec((1, D_MODEL, ADA_TN), lambda l, j: (l, 0, j)),
            pl.BlockSpec((1, 1, ADA_TN), lambda l, j: (l, 0, j)),
        ],
        out_specs=pl.BlockSpec((1, ADA_ROWS, ADA_TN), lambda l, j: (l, 0, j)),
        out_shape=jax.ShapeDtypeStruct((depth, ADA_ROWS, n_out), F32),
        compiler_params=_cparams("arbitrary", "arbitrary"),
        name="ada_mod",
    )(c_all, w_ada, b_ada.reshape(depth, 1, n_out))


def _even_in_kernel(x_ref, mod_ref, w_ref, lng_ref, lnb_ref, ws_ref, bst_ref,
                    a_ref, q_ref, k_ref, v_ref, kt_ref, vt_ref):
    base = 2 * W_A
    row = lax.broadcasted_iota(jnp.int32, (CHUNK, CHUNK), 0)
    col = lax.broadcasted_iota(jnp.int32, (CHUNK, CHUNK), 1)
    causal = col <= row
    bst = bst_ref[...]

    def front(u):
        m = mod_ref[u]
        h = _rms_mod(x_ref[u], m[1:2], m[0:1]).astype(BF16)
        va = _layernorm(jax.nn.gelu(_dot(h, w_ref[:, W_A:2 * W_A])), lng_ref[...], lnb_ref[...])
        ua = jax.nn.gelu(_dot(h, w_ref[:, 0:W_A]))
        q = _dot(h, w_ref[:, base:base + W_B])
        k = _dot(h, w_ref[:, base + W_B:base + 2 * W_B])
        for s in range(N_SLABS):
            q_ref[s, u] = q[:, s * LANES:(s + 1) * LANES]
            k_ref[s, u] = k[:, s * LANES:(s + 1) * LANES]
        return h, va.astype(BF16), ua, k

    def back(u, h, vab, ua):
        for g in range(G_A):
            wg = jnp.where(causal, ws_ref[g], 0.0).astype(BF16)
            bias = bst[:, g:g + 1]
            lo, hi = g * A_GROUP, (g + 1) * A_GROUP
            for c in range(TM // CHUNK):
                r0, r1 = c * CHUNK, (c + 1) * CHUNK
                mix = _dot(wg, vab[r0:r1, lo:hi]) + bias
                a_ref[u, r0:r1, lo:hi] = (ua[r0:r1, lo:hi] * mix).astype(BF16)
        v = _dot(h, w_ref[:, base + 2 * W_B:base + 3 * W_B])
        for s in range(N_SLABS):
            v_ref[s, u] = v[:, s * LANES:(s + 1) * LANES]
        return v

    fronts = [front(u) for u in range(SEQ_PAIR)]
    vs = [back(u, fronts[u][0], fronts[u][1], fronts[u][2]) for u in range(SEQ_PAIR)]
    for u in range(SEQ_PAIR):
        kt_ref[u] = fronts[u][3].T
        vt_ref[u] = vs[u].T


def _even_in(x2d, mod, w_in, ln_g, ln_b, w_s, b_s_t):
    n_tok = x2d.shape[0]
    first_win_tile = TILES_PER_SEQ - WIN // TM
    pair_tile = lambda i: (i // TILES_PER_SEQ, i % TILES_PER_SEQ, 0)
    qkv_shape = jax.ShapeDtypeStruct((N_SLABS, BATCH, SEQ, LANES), F32)
    qkv_spec = pl.BlockSpec((N_SLABS, SEQ_PAIR, TM, LANES),
                            lambda i: (0, i // TILES_PER_SEQ, i % TILES_PER_SEQ, 0))
    win_shape = jax.ShapeDtypeStruct((BATCH, W_B, WIN), F32)
    win_spec = pl.BlockSpec(
        (SEQ_PAIR, W_B, TM),
        lambda i: (i // TILES_PER_SEQ, 0, jnp.maximum(i % TILES_PER_SEQ - first_win_tile, 0)))
    a_out, q, k, v, kt, vt = pl.pallas_call(
        _even_in_kernel,
        grid=(N_TILES // SEQ_PAIR,),
        in_specs=[
            pl.BlockSpec((SEQ_PAIR, TM, D_MODEL), pair_tile),
            pl.BlockSpec((SEQ_PAIR, 6, D_MODEL), lambda i: (i // TILES_PER_SEQ, 0, 0)),
            _const_spec((D_MODEL, N_IN_EVEN)),
            _const_spec((1, W_A)),
            _const_spec((1, W_A)),
            _const_spec((G_A, CHUNK, CHUNK)),
            _const_spec((CHUNK, G_A)),
        ],
        out_specs=[
            pl.BlockSpec((SEQ_PAIR, TM, W_A), pair_tile),
            qkv_spec, qkv_spec, qkv_spec, win_spec, win_spec,
        ],
        out_shape=[
            jax.ShapeDtypeStruct((BATCH, SEQ, W_A), BF16),
            qkv_shape, qkv_shape, qkv_shape, win_shape, win_shape,
        ],
        compiler_params=_cparams("arbitrary"),
        name="even_in",
    )(x2d.reshape(BATCH, SEQ, D_MODEL), mod, w_in, ln_g, ln_b, w_s, b_s_t)
    flat = lambda a: a.reshape(N_SLABS, n_tok, LANES)
    return a_out.reshape(n_tok, W_A), flat(q), flat(k), flat(v), kt, vt


ATT_BLK = 128
ATT_UNROLL = 16


def _attn_kernel(q_ref, k_ref, v_ref, wu_hbm, wd_hbm, o_ref, wub_hbm, wdb_hbm,
                 acc_ref, max_ref, den_ref, q4_ref, k4_ref, v4_ref,
                 wu_in, wd_in, wu_out, wd_out, in_sem, out_sem):
    step = pl.program_id(0) * pl.num_programs(1) + pl.program_id(1)
    n_steps = pl.num_programs(0) * pl.num_programs(1)

    def cast_in(c):
        return (pltpu.make_async_copy(wu_hbm.at[pl.ds(c * wu_in.shape[0], wu_in.shape[0])], wu_in,
                                      in_sem.at[0]),
                pltpu.make_async_copy(wd_hbm.at[pl.ds(c * wd_in.shape[0], wd_in.shape[0])], wd_in,
                                      in_sem.at[1]))

    def cast_out(c):
        return (pltpu.make_async_copy(wu_out, wub_hbm.at[pl.ds(c * wu_in.shape[0], wu_in.shape[0])],
                                      out_sem.at[0]),
                pltpu.make_async_copy(wd_out, wdb_hbm.at[pl.ds(c * wd_in.shape[0], wd_in.shape[0])],
                                      out_sem.at[1]))

    for cp in cast_in(step):
        cp.start()

    @pl.when(step > 0)
    def _():
        for cp in cast_out(step - 1):
            cp.wait()

    q2, k2, v2 = q_ref.at[0], k_ref.at[0], v_ref.at[0]
    seq4 = SEQ // 4
    for src, dst in ((q2, q4_ref), (k2, k4_ref), (v2, v4_ref)):
        for c0 in range(4):
            for r in range(0, seq4, 2 * ATT_BLK):
                dst[c0 * seq4 + r:c0 * seq4 + r + 2 * ATT_BLK, :] = (
                    src[pl.ds(c0 + 4 * r, 2 * ATT_BLK, stride=4), :])
    qi = lax.broadcasted_iota(jnp.int32, (ATT_BLK, 2 * ATT_BLK), 0)
    kj = lax.broadcasted_iota(jnp.int32, (ATT_BLK, 2 * ATT_BLK), 1)
    dist = qi + ATT_BLK - kj
    band = (dist >= 0) & (dist <= N_BACK)
    bias_full = jnp.where(band, 0.0, NEG_INF).astype(F32)
    bias_first = jnp.where(band & (kj >= ATT_BLK), 0.0, NEG_INF).astype(F32)
    lane = lax.broadcasted_iota(jnp.int32, (ATT_BLK, LANES), 1)
    head0 = lane < HD

    for p, d in enumerate(DILATIONS):
        n_blk = SEQ // (d * ATT_BLK)

        def unit(idx, carry, p=p, d=d, n_blk=n_blk):
            k_prev, v_prev = carry
            c = idx // n_blk
            b = idx % n_blk
            if d == 1:
                qs, ks, vs = q2, k2, v2
                rows = pl.ds(pl.multiple_of(ATT_BLK * b, ATT_BLK), ATT_BLK)
            elif d == 4:
                qs, ks, vs = q4_ref, k4_ref, v4_ref
                rows = pl.ds(pl.multiple_of(c * seq4 + ATT_BLK * b, ATT_BLK), ATT_BLK)
            else:
                qs, ks, vs = q4_ref, k4_ref, v4_ref
                rows = pl.ds((c % 4) * seq4 + c // 4 + 4 * ATT_BLK * b, ATT_BLK, stride=4)
            qb = qs[rows, :] * (HD ** -0.5 * LOG2_E)
            k_own = ks[rows, :].astype(BF16)
            v_own = vs[rows, :].astype(BF16)
            kb = jnp.concatenate([k_prev, k_own], axis=0)
            vb = jnp.concatenate([v_prev, v_own], axis=0)
            bias = jnp.where(b == 0, bias_first, bias_full)
            outs, mxs, dens = [], [], []
            for hh in range(2):
                sel = head0 if hh == 0 else jnp.logical_not(head0)
                qm = jnp.where(sel, qb, 0.0).astype(BF16)
                s = lax.dot_general(qm, kb, (((1,), (1,)), ((), ())),
                                    preferred_element_type=F32) + bias
                mx = jnp.max(s, axis=-1, keepdims=True)
                e = jnp.exp2(s - mx)
                dens.append(jnp.sum(e, axis=-1, keepdims=True))
                mxs.append(mx)
                outs.append(_dot(e.astype(BF16), vb))
            acc_ref[p, rows, :] = jnp.where(head0, outs[0], outs[1])
            max_ref[p, rows, :] = jnp.where(head0, mxs[0], mxs[1])
            den_ref[p, rows, :] = jnp.where(head0, dens[0], dens[1])
            return k_own, v_own

        zeros = jnp.zeros((ATT_BLK, LANES), BF16)
        lax.fori_loop(0, SEQ // ATT_BLK, unit, (zeros, zeros), unroll=ATT_UNROLL)

    def mix(t, carry):
        c0 = t // (seq4 // TM)
        r = (t % (seq4 // TM)) * TM
        rows4 = pl.ds(pl.multiple_of(c0 * seq4 + r, TM), TM)
        rows = pl.ds(c0 + 4 * r, TM, stride=4)
        m0, m1, m2 = max_ref[0, rows, :], max_ref[1, rows4, :], max_ref[2, rows4, :]
        mx = jnp.maximum(jnp.maximum(m0, m1), m2)
        e0, e1, e2 = jnp.exp2(m0 - mx), jnp.exp2(m1 - mx), jnp.exp2(m2 - mx)
        num = e0 * acc_ref[0, rows, :] + e1 * acc_ref[1, rows4, :] + e2 * acc_ref[2, rows4, :]
        den = e0 * den_ref[0, rows, :] + e1 * den_ref[1, rows4, :] + e2 * den_ref[2, rows4, :]
        acc_ref[0, rows, :] = num / den
        return carry

    lax.fori_loop(0, SEQ // TM, mix, 0)

    def emit(t, carry):
        rows = pl.ds(pl.multiple_of(t * TM, TM), TM)
        o_ref[rows, :] = acc_ref[0, rows, :].astype(BF16)
        return carry

    lax.fori_loop(0, SEQ // TM, emit, 0)

    for cp in cast_in(step):
        cp.wait()
    wu_out[...] = wu_in[...].astype(BF16)
    wd_out[...] = wd_in[...].astype(BF16)
    for cp in cast_out(step):
        cp.start()

    @pl.when(step == n_steps - 1)
    def _():
        for cp in cast_out(step):
            cp.wait()


def _attn(q, k, v, w_up, w_down):
    n_tok = q.shape[1]
    n_steps = BATCH * N_SLABS
    wu2 = w_up.reshape(-1, w_up.shape[-1])
    wd2 = w_down.reshape(-1, w_down.shape[-1])
    ru, rd = wu2.shape[0] // n_steps, wd2.shape[0] // n_steps
    assert ru * n_steps == wu2.shape[0] and rd * n_steps == wd2.shape[0] and ru % 16 == 0 and rd % 16 == 0
    spec = pl.BlockSpec((1, SEQ, LANES), lambda n, s: (s, n, 0))
    any_spec = pl.BlockSpec(memory_space=pl.ANY)
    b_out, wub, wdb = pl.pallas_call(
        _attn_kernel,
        grid=(BATCH, N_SLABS),
        in_specs=[spec, spec, spec, any_spec, any_spec],
        out_specs=[pl.BlockSpec((SEQ, LANES), lambda n, s: (n, s)), any_spec, any_spec],
        out_shape=[jax.ShapeDtypeStruct((n_tok, W_B), BF16),
                   jax.ShapeDtypeStruct(wu2.shape, BF16), jax.ShapeDtypeStruct(wd2.shape, BF16)],
        scratch_shapes=([pltpu.VMEM((len(DILATIONS), SEQ, LANES), F32)] * 3
                        + [pltpu.VMEM((SEQ, LANES), F32)] * 3
                        + [pltpu.VMEM((ru, wu2.shape[1]), F32), pltpu.VMEM((rd, wd2.shape[1]), F32),
                           pltpu.VMEM((ru, wu2.shape[1]), BF16), pltpu.VMEM((rd, wd2.shape[1]), BF16),
                           pltpu.SemaphoreType.DMA((2,)), pltpu.SemaphoreType.DMA((2,))]),
        compiler_params=_cparams("arbitrary", "arbitrary"),
        name="dil_attn",
    )(q, k, v, wu2, wd2)
    return b_out, wub.reshape(w_up.shape), wdb.reshape(w_down.shape)


S_ATT_HEADS = 4
S_ATT_PHASES = N_HEADS // S_ATT_HEADS
FF_SPLIT = 6


def _ffn_kernel(*refs, pre, final, att):
    refs = list(refs)
    x_ref, mod_ref = refs[0:2]
    del refs[0:2]
    if pre:
        a_ref, b_ref, wo_ref = refs[0:3]
        del refs[0:3]
    wup_ref, cw_ref, cb_ref, wdn_ref, fg_ref = refs[0:5]
    del refs[0:5]
    if att:
        (qs_ref, kn_ref, vn_ref, kt_hbm, vt_hbm, o_ref, st_ref, bs_ref, kto_hbm, vto_hbm,
         carry_ref, act_ref, kin_ref, vin_ref, kout_ref, vout_ref, in_sem, out_sem) = refs
    else:
        o_ref, st_ref, carry_ref, act_ref = refs
    i = pl.program_id(0)
    last = pl.num_programs(0) - 1

    @pl.when(i % TILES_PER_SEQ == 0)
    def _():
        carry_ref[...] = jnp.zeros_like(carry_ref)

    def copies_in(seq, ph):
        hs = pl.ds(ph * S_ATT_HEADS, S_ATT_HEADS)
        return (pltpu.make_async_copy(kt_hbm.at[seq, hs], kin_ref.at[ph], in_sem.at[ph, 0]),
                pltpu.make_async_copy(vt_hbm.at[seq, hs], vin_ref.at[ph], in_sem.at[ph, 1]))

    def copies_out(seq, ph):
        hs = pl.ds(ph * S_ATT_HEADS, S_ATT_HEADS)
        return (pltpu.make_async_copy(kout_ref.at[ph], kto_hbm.at[seq, hs], out_sem.at[ph, 0]),
                pltpu.make_async_copy(vout_ref.at[ph], vto_hbm.at[seq, hs], out_sem.at[ph, 1]))

    def start(cps):
        for c in cps:
            c.start()

    def wait(cps):
        for c in cps:
            c.wait()

    def window_phase(ph):
        lanes = slice(ph * S_ATT_HEADS * HD, (ph + 1) * S_ATT_HEADS * HD)
        row = pl.ds(i, 1)
        o_row, kt_new, vt_new = _s_attn_heads(qs_ref[row, lanes], kn_ref[row, lanes],
                                              vn_ref[row, lanes], kin_ref[ph], vin_ref[ph])
        bs_ref[row, lanes] = o_row
        kout_ref[ph] = kt_new
        vout_ref[ph] = vt_new

    if att:
        @pl.when(i == 0)
        def _():
            kout_ref[1] = jnp.zeros(kout_ref.shape[1:], F32)
            vout_ref[1] = jnp.zeros(vout_ref.shape[1:], F32)
            start(copies_in(0, 0))
            start(copies_out(0, 1))

        wait(copies_in(i, 0))
        start(copies_in(i, 1))

        @pl.when(i > 0)
        def _():
            wait(copies_out(i - 1, 0))

        window_phase(0)

    m = mod_ref[0]
    cw = cw_ref[...]
    cb = cb_ref[...]
    x = x_ref[...]
    if pre:
        x = x + m[2:3] * (_dot(a_ref[...], wo_ref[0:W_A, :]) + _dot(b_ref[...], wo_ref[W_A:W_A + W_B, :]))
    h = _rms_mod(x, m[4:5], m[3:4]).astype(BF16)
    for j in range(N_FF_CHUNKS):
        if att and j == FF_SPLIT:
            start(copies_out(i, 0))
            wait(copies_in(i, 1))
            start(copies_in(jnp.minimum(i + 1, last), 0))
            wait(copies_out(jnp.maximum(i - 1, 0), 1))
            window_phase(1)
        conv = []
        for half in range(2):
            c0 = half * D_FF + j * FF_CHUNK
            cols = slice(c0, c0 + FF_CHUNK)
            up = _dot(h, wup_ref[:, cols])
            tail = carry_ref[:, cols]
            carry_ref[:, cols] = up[TM - SUBLANES:, :]
            conv.append(cb[:, cols] + cw[0:1, cols] * _shift_rows(up, tail, 2)
                        + cw[1:2, cols] * _shift_rows(up, tail, 1) + cw[2:3, cols] * up)
        act = conv[0] * jax.nn.sigmoid(conv[0]) * conv[1]
        act_ref[:, j * FF_CHUNK:(j + 1) * FF_CHUNK] = act.astype(BF16)
    st_ref[0] = carry_ref[...]
    y = x + m[5:6] * _dot(act_ref[...], wdn_ref[...])
    if final:
        y = _rmsnorm(y) * fg_ref[...]
    o_ref[...] = y

    if att:
        start(copies_out(i, 1))

        @pl.when(i == last)
        def _():
            wait(copies_out(i, 0))
            wait(copies_out(i, 1))
            wait(copies_in(i, 0))


def _ffn(x2d, mod, layer, w_up, conv_w, conv_b, w_down, final_g, final, pre=None, att=None):
    n_tok = x2d.shape[0]
    f2 = 2 * D_FF
    row_spec = pl.BlockSpec((TM, D_MODEL), lambda i: (i, 0))
    mod_spec = pl.BlockSpec((1, 6, D_MODEL), lambda i: (i // TILES_PER_SEQ, 0, 0))
    args, specs = [x2d, mod], [row_spec, mod_spec]
    if pre is not None:
        args += list(pre)
        specs += [pl.BlockSpec((TM, W_A), lambda i: (i, 0)),
                  pl.BlockSpec((TM, W_B), lambda i: (i, 0)),
                  _const_spec((W_A + W_B, D_MODEL))]
    args += [w_up, conv_w, conv_b, w_down, final_g]
    specs += [_layer_spec((D_MODEL, f2), layer), _layer_spec((3, f2), layer),
              _layer_spec((1, f2), layer), _layer_spec((D_FF, D_MODEL), layer),
              _const_spec((1, D_MODEL))]
    out_specs = [row_spec, pl.BlockSpec((1, SUBLANES, f2), lambda i: (i // TILES_PER_SEQ, 0, 0))]
    out_shape = [jax.ShapeDtypeStruct((n_tok, D_MODEL), F32),
                 jax.ShapeDtypeStruct((BATCH, SUBLANES, f2), F32)]
    scratch = [pltpu.VMEM((SUBLANES, f2), F32), pltpu.VMEM((TM, D_FF), BF16)]
    if att is not None:
        q_s, kt = att[0], att[3]
        assert q_s.shape[0] == n_tok // TM, "one sample sequence per grid step"
        any_spec = pl.BlockSpec(memory_space=pl.ANY)
        args += list(att)
        specs += [_const_spec(q_s.shape)] * 3 + [any_spec, any_spec]
        out_specs += [pl.BlockSpec(q_s.shape, lambda i: (0, 0)), any_spec, any_spec]
        out_shape += [jax.ShapeDtypeStruct(q_s.shape, F32),
                      jax.ShapeDtypeStruct(kt.shape, F32), jax.ShapeDtypeStruct(kt.shape, F32)]
        win_buf = pltpu.VMEM((S_ATT_PHASES, S_ATT_HEADS, HD, WIN), F32)
        scratch += [win_buf] * 4 + [pltpu.SemaphoreType.DMA((S_ATT_PHASES, 2))] * 2
    return pl.pallas_call(
        functools.partial(_ffn_kernel, pre=pre is not None, final=final, att=att is not None),
        grid=(n_tok // TM,),
        in_specs=specs,
        out_specs=out_specs,
        out_shape=out_shape,
        scratch_shapes=scratch,
        compiler_params=_cparams("arbitrary"),
        name="conv_ffn_final" if final else "conv_ffn",
    )(*args)


SEG_LEN = TM // SUBLANES
RG_PAIR = 2


def _rg_gates(xc, wa_ref, ba, wx_ref, bx, lam):
    xcb = xc.astype(BF16)
    r_parts, i_parts = [], []
    for g in range(RG_HEADS):
        blk = xcb[:, g * RG_BLOCK:(g + 1) * RG_BLOCK]
        r_parts.append(_dot(blk, wa_ref[g]))
        i_parts.append(_dot(blk, wx_ref[g]))
    r = jax.nn.sigmoid(jnp.concatenate(r_parts, axis=-1) + ba)
    ig = jax.nn.sigmoid(jnp.concatenate(i_parts, axis=-1) + bx)
    log_a = r * ((-RG_C) * jax.nn.softplus(-lam))
    a = jnp.exp(log_a)
    b = jnp.sqrt(1.0 - a * a) * (ig * xc)
    return a, b


def _rglru_kernel(x_ref, mod_ref, perm_ref, unperm_ref, win_ref, cw_ref, cb_ref, wa_ref, ba_ref,
                  wx_ref, bx_ref, lam_ref, wout_ref, o_ref, cst_ref, hst_ref,
                  xtail_ref, hcar_ref, a_ref, b_ref, hs_ref, ac_ref):
    t = pl.program_id(0) % TILES_PER_SEQ

    @pl.when(t == 0)
    def _():
        xtail_ref[...] = jnp.zeros_like(xtail_ref)
        hcar_ref[...] = jnp.zeros_like(hcar_ref)

    sub = lax.broadcasted_iota(jnp.int32, (SUBLANES, D_RNN), 0)
    cw = cw_ref[...]

    def front(u):
        m = mod_ref[u]
        h = _rms_mod(x_ref[u], m[1:2], m[0:1]).astype(BF16)
        hp = _dot(perm_ref[...], h).astype(BF16)
        xr3 = _dot(hp, win_ref[:, D_RNN:2 * D_RNN]).reshape(SEG_LEN, SUBLANES, D_RNN)
        gate = _dot(hp, win_ref[:, 0:D_RNN])
        tail = xtail_ref[u]
        xc3 = cb_ref[...] + cw[3:4] * xr3
        for k in range(1, 4):
            wrap = [jnp.where(sub == 0, tail[SUBLANES - k + v:SUBLANES - k + v + 1, :],
                              pltpu.roll(xr3[SEG_LEN - k + v], 1, 0)) for v in range(k)]
            prev = jnp.concatenate([jnp.stack(wrap), xr3[0:SEG_LEN - k]], axis=0)
            xc3 = xc3 + cw[3 - k:4 - k] * prev
        last = jnp.zeros((SUBLANES, D_RNN), F32)
        for r in range(SUBLANES):
            last = jnp.where(sub == r, pltpu.roll(xr3[SEG_LEN - SUBLANES + r], r + 1, 0), last)
        xtail_ref[u] = last
        cst_ref[u] = last
        a, b = _rg_gates(xc3.reshape(TM, D_RNN), wa_ref, ba_ref[...], wx_ref, bx_ref[...],
                         lam_ref[...])
        a_ref[u] = a.reshape(SEG_LEN, SUBLANES, D_RNN)
        b_ref[u] = b.reshape(SEG_LEN, SUBLANES, D_RNN)
        return gate

    def scan(u):
        def step(v, carry):
            hl, ac = carry
            av = a_ref[u, v]
            hl = av * hl + b_ref[u, v]
            ac = av * ac
            hs_ref[u, v] = hl
            ac_ref[u, v] = ac
            return hl, ac

        zero = jnp.zeros((SUBLANES, D_RNN), F32)
        hl, ac = lax.fori_loop(0, SEG_LEN, step, (zero, zero + 1.0), unroll=True)
        h_in = jnp.where(sub == 0, hcar_ref[u], 0.0)
        for s in range(1, SUBLANES):
            h_in = jnp.where(sub == s, pltpu.roll(hl + ac * h_in, 1, 0), h_in)
        h_end = (hl + ac * h_in)[SUBLANES - 1:SUBLANES, :]
        hcar_ref[u] = h_end
        hst_ref[u] = h_end
        return hs_ref[u] + ac_ref[u] * h_in

    def back(u, gate, hs):
        y = (jax.nn.gelu(gate) * hs.reshape(TM, D_RNN)).astype(BF16)
        y = _dot(unperm_ref[...], y).astype(BF16)
        o_ref[u] = x_ref[u] + mod_ref[u][2:3] * _dot(y, wout_ref[...])

    gates = [front(u) for u in range(RG_PAIR)]
    for u in range(RG_PAIR):
        back(u, gates[u], scan(u))


def _rglru(x2d, mod, w_in, conv_w, conv_b, w_a, b_a, w_x, b_x, lam, w_out):
    n_tok = x2d.shape[0]
    x3 = x2d.reshape(BATCH, SEQ, D_MODEL)
    rho = jnp.arange(TM)
    perm = (jnp.arange(TM)[None, :] == ((rho % SUBLANES) * SEG_LEN + rho // SUBLANES)[:, None])
    perm = perm.astype(BF16)
    row_spec = pl.BlockSpec((RG_PAIR, TM, D_MODEL),
                            lambda i: (i // TILES_PER_SEQ, i % TILES_PER_SEQ, 0))

    def seq_spec(rows):
        return pl.BlockSpec((RG_PAIR, rows, D_RNN), lambda i: (i // TILES_PER_SEQ, 0, 0))

    out, cst, hst = pl.pallas_call(
        _rglru_kernel,
        grid=(n_tok // (RG_PAIR * TM),),
        in_specs=[
            row_spec,
            seq_spec(6),
            _const_spec((TM, TM)),
            _const_spec((TM, TM)),
            _const_spec((D_MODEL, 2 * D_RNN)),
            _const_spec((4, D_RNN)),
            _const_spec((1, D_RNN)),
            _const_spec((RG_HEADS, RG_BLOCK, RG_BLOCK)),
            _const_spec((1, D_RNN)),
            _const_spec((RG_HEADS, RG_BLOCK, RG_BLOCK)),
            _const_spec((1, D_RNN)),
            _const_spec((1, D_RNN)),
            _const_spec((D_RNN, D_MODEL)),
        ],
        out_specs=[row_spec, seq_spec(SUBLANES), seq_spec(1)],
        out_shape=[
            jax.ShapeDtypeStruct((BATCH, SEQ, D_MODEL), F32),
            jax.ShapeDtypeStruct((BATCH, SUBLANES, D_RNN), F32),
            jax.ShapeDtypeStruct((BATCH, 1, D_RNN), F32),
        ],
        scratch_shapes=[
            pltpu.VMEM((RG_PAIR, SUBLANES, D_RNN), F32),
            pltpu.VMEM((RG_PAIR, 1, D_RNN), F32),
        ] + [pltpu.VMEM((RG_PAIR, SEG_LEN, SUBLANES, D_RNN), F32)] * 4,
        compiler_params=_cparams("arbitrary"),
        name="rglru",
    )(x3, mod, perm, perm.T, w_in, conv_w, conv_b, w_a, b_a, w_x, b_x, lam, w_out)
    return out.reshape(n_tok, D_MODEL), cst, hst


def _s_even_in_kernel(x_ref, mod_ref, w_ref, lng_ref, lnb_ref, ws0_ref, bs0_ref,
                      va_ref, a_ref, q_ref, k_ref, v_ref):
    mod = mod_ref[...]
    h = _rms_mod(x_ref[...], mod[:, D_MODEL:2 * D_MODEL], mod[:, 0:D_MODEL]).astype(BF16)
    u = jax.nn.gelu(_dot(h, w_ref[:, 0:W_A]))
    va = _layernorm(jax.nn.gelu(_dot(h, w_ref[:, W_A:2 * W_A])), lng_ref[...], lnb_ref[...])
    va_ref[...] = va
    a_ref[...] = (u * (ws0_ref[...] * va + bs0_ref[...])).astype(BF16)
    base = 2 * W_A
    q_ref[...] = _dot(h, w_ref[:, base:base + W_B])
    k_ref[...] = _dot(h, w_ref[:, base + W_B:base + 2 * W_B])
    v_ref[...] = _dot(h, w_ref[:, base + 2 * W_B:base + 3 * W_B])


def _s_even_in(x, mod, w_in, ln_g, ln_b, ws0, bs0):
    n = x.shape[0]
    f = jax.ShapeDtypeStruct((n, W_B), F32)
    return pl.pallas_call(
        _s_even_in_kernel,
        out_shape=[f, jax.ShapeDtypeStruct((n, W_A), BF16), f, f, f],
        compiler_params=pltpu.CompilerParams(vmem_limit_bytes=VMEM_LIMIT),
        name="s_even_in",
    )(x, mod, w_in, ln_g, ln_b, ws0, bs0)


def _s_attn_heads(qrow, knrow, vnrow, kt3, vt3):
    nh = kt3.shape[0]
    n_hd = nh * HD
    qrow = qrow * (HD ** -0.5)
    hrow = lax.broadcasted_iota(jnp.int32, (SUBLANES, n_hd), 0)
    hcol = lax.broadcasted_iota(jnp.int32, (SUBLANES, n_hd), 1) // HD
    own = hrow == hcol
    qbd = jnp.where(own, jnp.broadcast_to(qrow, (SUBLANES, n_hd)), 0.0)
    kt = kt3.reshape(n_hd, WIN)
    vt = vt3.reshape(n_hd, WIN)
    s = _dot(qbd.astype(BF16), kt.astype(BF16))
    s_new = jnp.sum(qbd * knrow, axis=-1, keepdims=True)
    t = lax.broadcasted_iota(jnp.int32, (SUBLANES, WIN), 1)
    dist = WIN - t
    vtb = vt.astype(BF16)
    outs, lses = [], []
    for d in DILATIONS:
        valid = ((dist & (d - 1)) == 0) & (dist <= N_BACK * d)
        sm = jnp.where(valid, s, NEG_INF)
        mx = jnp.maximum(jnp.max(sm, axis=-1, keepdims=True), s_new)
        e = jnp.where(valid, jnp.exp(sm - mx), 0.0)
        e_new = jnp.exp(s_new - mx)
        den = jnp.sum(e, axis=-1, keepdims=True) + e_new
        o = lax.dot_general(e.astype(BF16), vtb, (((1,), (1,)), ((), ())),
                            preferred_element_type=F32)
        outs.append((o + e_new * vnrow) / den)
        lses.append(mx + jnp.log(den))
    mx = jnp.maximum(jnp.maximum(lses[0], lses[1]), lses[2])
    ws = [jnp.exp(l - mx) for l in lses]
    num = ws[0] * outs[0] + ws[1] * outs[1] + ws[2] * outs[2]
    mixed = num / (ws[0] + ws[1] + ws[2])
    o_row = jnp.sum(jnp.where(own, mixed, 0.0), axis=0, keepdims=True)

    r2 = lax.broadcasted_iota(jnp.int32, (n_hd, n_hd), 0)
    c2 = lax.broadcasted_iota(jnp.int32, (n_hd, n_hd), 1)
    diag = r2 == c2
    kcol = jnp.sum(jnp.where(diag, jnp.broadcast_to(knrow, (n_hd, n_hd)), 0.0), axis=-1, keepdims=True)
    vcol = jnp.sum(jnp.where(diag, jnp.broadcast_to(vnrow, (n_hd, n_hd)), 0.0), axis=-1, keepdims=True)
    tt = lax.broadcasted_iota(jnp.int32, (n_hd, WIN), 1)
    is_last = tt == WIN - 1
    kt_new = jnp.where(is_last, kcol, pltpu.roll(kt, WIN - 1, 1)).reshape(nh, HD, WIN)
    vt_new = jnp.where(is_last, vcol, pltpu.roll(vt, WIN - 1, 1)).reshape(nh, HD, WIN)
    return o_row, kt_new, vt_new


def _s_even_out_kernel(x_ref, mod_ref, a_ref, b_ref, w_ref, o_ref):
    g1 = mod_ref[:, 2 * D_MODEL:3 * D_MODEL]
    mo = _dot(a_ref[...], w_ref[0:W_A, :]) + _dot(b_ref[...].astype(BF16), w_ref[W_A:W_A + W_B, :])
    o_ref[...] = x_ref[...] + g1 * mo


def _s_even_out(x, mod, a_out, b_out, w_out):
    return pl.pallas_call(
        _s_even_out_kernel,
        out_shape=jax.ShapeDtypeStruct(x.shape, F32),
        compiler_params=pltpu.CompilerParams(vmem_limit_bytes=VMEM_LIMIT),
        name="s_even_out",
    )(x, mod, a_out, b_out, w_out)


def _s_ffn_kernel(x_ref, mod_ref, p2_ref, p1_ref, wup_ref, cw_ref, cb_ref, wdn_ref, fg_ref,
                  o_ref, up_ref, *, final):
    x = x_ref[...]
    mod = mod_ref[...]
    h = _rms_mod(x, mod[:, 4 * D_MODEL:5 * D_MODEL], mod[:, 3 * D_MODEL:4 * D_MODEL]).astype(BF16)
    up = _dot(h, wup_ref[...])
    up_ref[...] = up
    cw = cw_ref[...]
    conv = cb_ref[...] + cw[0:1] * p2_ref[...] + cw[1:2] * p1_ref[...] + cw[2:3] * up
    ca, cg = conv[:, 0:D_FF], conv[:, D_FF:2 * D_FF]
    act = (ca * jax.nn.sigmoid(ca) * cg).astype(BF16)
    y = x + mod[:, 5 * D_MODEL:6 * D_MODEL] * _dot(act, wdn_ref[...])
    if final:
        y = _rmsnorm(y) * fg_ref[...]
    o_ref[...] = y


def _s_ffn(x, mod, p2, p1, layer, w_up, conv_w, conv_b, w_down, final_g, final):
    n = x.shape[0]
    f2 = 2 * D_FF
    return pl.pallas_call(
        functools.partial(_s_ffn_kernel, final=final),
        grid=(1,),
        in_specs=[_const_spec(x.shape), _const_spec(mod.shape), _const_spec(p2.shape),
                  _const_spec(p1.shape), _layer_spec((D_MODEL, f2), layer),
                  _layer_spec((3, f2), layer), _layer_spec((1, f2), layer),
                  _layer_spec((D_FF, D_MODEL), layer), _const_spec((1, D_MODEL))],
        out_specs=[pl.BlockSpec(x.shape, lambda i: (0, 0)), pl.BlockSpec((n, f2), lambda i: (0, 0))],
        out_shape=[jax.ShapeDtypeStruct(x.shape, F32), jax.ShapeDtypeStruct((n, f2), F32)],
        compiler_params=_cparams("arbitrary"),
        name="s_conv_ffn_final" if final else "s_conv_ffn",
    )(x, mod, p2, p1, w_up, conv_w, conv_b, w_down, final_g)


def _s_rglru_kernel(x_ref, mod_ref, c0_ref, c1_ref, c2_ref, h0_ref, win_ref, cw_ref, cb_ref,
                    wa_ref, ba_ref, wx_ref, bx_ref, lam_ref, wout_ref, o_ref, xr_ref, hn_ref):
    x = x_ref[...]
    mod = mod_ref[...]
    h = _rms_mod(x, mod[:, D_MODEL:2 * D_MODEL], mod[:, 0:D_MODEL]).astype(BF16)
    gate = _dot(h, win_ref[:, 0:D_RNN])
    xr = _dot(h, win_ref[:, D_RNN:2 * D_RNN])
    xr_ref[...] = xr
    cw = cw_ref[...]
    xc = (cb_ref[...] + cw[0:1] * c0_ref[...] + cw[1:2] * c1_ref[...] + cw[2:3] * c2_ref[...]
          + cw[3:4] * xr)
    a, b = _rg_gates(xc, wa_ref, ba_ref[...], wx_ref, bx_ref[...], lam_ref[...])
    hn = a * h0_ref[...] + b
    hn_ref[...] = hn
    y = (jax.nn.gelu(gate) * hn).astype(BF16)
    o_ref[...] = x + mod[:, 2 * D_MODEL:3 * D_MODEL] * _dot(y, wout_ref[...])


def _s_rglru(x, mod, c0, c1, c2, h0, w_in, conv_w, conv_b, w_a, b_a, w_x, b_x, lam, w_out):
    f = jax.ShapeDtypeStruct(x.shape, F32)
    return pl.pallas_call(
        _s_rglru_kernel,
        out_shape=[f, f, f],
        compiler_params=pltpu.CompilerParams(vmem_limit_bytes=VMEM_LIMIT),
        name="s_rglru",
    )(x, mod, c0, c1, c2, h0, w_in, conv_w, conv_b, w_a, b_a, w_x, b_x, lam, w_out)


def kernel(x_prompt, x_sample, cache_win_k, cache_win_v, state_rglru_conv, state_rglru_h, state_ffn_conv, c_prompt, c_sample, w_ada, b_ada, w_in_even, ln_v_g, ln_v_b, w_sgu, b_sgu, w_out_even, w_in_odd, rg_conv_w, rg_conv_b, rg_w_a, rg_b_a, rg_w_x, rg_b_x, rg_lambda, w_out_odd, ffn_w_up, ffn_conv_w, ffn_conv_b, ffn_w_down, final_g):
    w_in_even_b = w_in_even[0].astype(BF16)
    w_out_even_b = w_out_even[0].astype(BF16)
    w_in_odd_b = w_in_odd[0].astype(BF16)
    w_out_odd_b = w_out_odd[0].astype(BF16)
    rg_w_a_b = rg_w_a[0].astype(BF16)
    rg_w_x_b = rg_w_x[0].astype(BF16)
    conv_b3 = ffn_conv_b.reshape(2, 1, 2 * D_FF)
    final_g2 = final_g.reshape(1, D_MODEL)

    pad = jnp.zeros((ADA_ROWS - BATCH - DEC_BATCH, D_MODEL), F32)
    mod = _ada(jnp.concatenate([c_prompt, c_sample, pad], axis=0), w_ada, b_ada)
    mod_p = mod[:, :BATCH].reshape(2, BATCH, 6, D_MODEL)
    mod_s = mod[:, BATCH:BATCH + DEC_BATCH]

    xs = x_sample.reshape(DEC_BATCH, D_MODEL)
    ws0 = jnp.repeat(w_sgu[0, :, 0, 0], A_GROUP).reshape(1, W_A)
    bs0 = jnp.repeat(b_sgu[0, :, 0], A_GROUP).reshape(1, W_A)
    va_s, a_s, q_s, k_s, v_s = _s_even_in(xs, mod_s[0], w_in_even_b, ln_v_g, ln_v_b, ws0, bs0)
    kt_c = cache_win_k[0].transpose(0, 2, 3, 1)
    vt_c = cache_win_v[0].transpose(0, 2, 3, 1)

    x = x_prompt.reshape(BATCH * SEQ, D_MODEL)
    a_out, q, k, v, kt_p, vt_p = _even_in(x, mod_p[0], w_in_even_b, ln_v_g, ln_v_b, w_sgu[0],
                                          b_sgu[0].T)
    b_out, w_up_b, w_down_b = _attn(q, k, v, ffn_w_up, ffn_w_down)
    x, ffn_st0 = _ffn(x, mod_p[0], 0, w_up_b, ffn_conv_w, conv_b3, w_down_b, final_g2, False,
                      pre=(a_out, b_out, w_out_even_b))
    x, rg_cst, rg_hst = _rglru(x, mod_p[1], w_in_odd_b, rg_conv_w[0], rg_conv_b, rg_w_a_b, rg_b_a,
                               rg_w_x_b, rg_b_x, rg_lambda, w_out_odd_b)
    y_p, ffn_st1, b_s, kt_n, vt_n = _ffn(x, mod_p[1], 1, w_up_b, ffn_conv_w, conv_b3, w_down_b,
                                         final_g2, True, att=(q_s, k_s, v_s, kt_c, vt_c))

    y_prompt = y_p.reshape(BATCH, SEQ, D_MODEL)
    to_win = lambda t: t.reshape(1, BATCH, N_HEADS, HD, WIN).transpose(0, 1, 4, 2, 3)
    win_k_prompt = to_win(kt_p)
    win_v_prompt = to_win(vt_p)
    rglru_conv_prompt = rg_cst[None, :, SUBLANES - 3:, :]
    rglru_h_prompt = rg_hst.reshape(1, BATCH, D_RNN)
    ffn_conv_prompt = jnp.stack([ffn_st0[:, SUBLANES - 2:, :], ffn_st1[:, SUBLANES - 2:, :]])

    xs = _s_even_out(xs, mod_s[0], a_s, b_s, w_out_even_b)
    st0 = state_ffn_conv[0]
    xs, up0 = _s_ffn(xs, mod_s[0], st0[:, 0], st0[:, 1], 0, w_up_b, ffn_conv_w, conv_b3, w_down_b,
                     final_g2, False)
    cst = state_rglru_conv[0]
    xs, xr_s, hn_s = _s_rglru(xs, mod_s[1], cst[:, 0], cst[:, 1], cst[:, 2], state_rglru_h[0],
                              w_in_odd_b, rg_conv_w[0], rg_conv_b, rg_w_a_b, rg_b_a, rg_w_x_b,
                              rg_b_x, rg_lambda, w_out_odd_b)
    st1 = state_ffn_conv[1]
    ys, up1 = _s_ffn(xs, mod_s[1], st1[:, 0], st1[:, 1], 1, w_up_b, ffn_conv_w, conv_b3, w_down_b,
                     final_g2, True)

    y_sample = ys.reshape(DEC_BATCH, 1, D_MODEL)
    chunk_v_sample = va_s.reshape(1, DEC_BATCH, 1, W_A)
    win_k_sample = kt_n.transpose(0, 3, 1, 2)[None]
    win_v_sample = vt_n.transpose(0, 3, 1, 2)[None]
    rglru_conv_sample = jnp.stack([cst[:, 1], cst[:, 2], xr_s], axis=1)[None]
    rglru_h_sample = hn_s[None]
    ffn_conv_sample = jnp.stack([jnp.stack([st0[:, 1], up0], axis=1),
                                 jnp.stack([st1[:, 1], up1], axis=1)])

    return (y_prompt, y_sample, win_k_prompt, win_v_prompt, rglru_conv_prompt, rglru_h_prompt,
            ffn_conv_prompt, chunk_v_sample, win_k_sample, win_v_sample, rglru_conv_sample,
            rglru_h_sample, ffn_conv_sample)
```

```python
import functools

import jax
import jax.numpy as jnp
from jax import lax
from jax.experimental import pallas as pl
from jax.experimental.pallas import tpu as pltpu

F32 = jnp.float32
BF16 = jnp.bfloat16

D_MODEL = 1024
BATCH = 4
SEQ = 4096
DEC_BATCH = 32
W_A = 512
A_GROUP = 128
G_A = 4
CHUNK = 128
W_B = 512
HD = 64
N_HEADS = 8
DILATIONS = (1, 4, 16)
N_BACK = 128
WIN = 2048
N_IN_EVEN = 2 * W_A + 3 * W_B
D_RNN = 1024
RG_BLOCK = 128
RG_HEADS = 8
RG_C = 8.0
D_FF = 2816
EPS = 1e-6
NEG_INF = -1e30
LOG2_E = 1.4426950408889634

LANES = 128
SUBLANES = 8
TM = 512
TILES_PER_SEQ = SEQ // TM
N_TILES = BATCH * TILES_PER_SEQ
FF_CHUNK = 256
N_FF_CHUNKS = D_FF // FF_CHUNK
N_SLABS = W_B // LANES
SEQ_PAIR = 2
VMEM_LIMIT = 56 * 1024 * 1024


def _cparams(*sem):
    return pltpu.CompilerParams(dimension_semantics=sem, vmem_limit_bytes=VMEM_LIMIT)


def _const_spec(shape):
    nd = len(shape)
    return pl.BlockSpec(shape, lambda *_: (0,) * nd, pipeline_mode=pl.Buffered(1))


def _layer_spec(shape, layer):
    nd = len(shape)
    return pl.BlockSpec((None,) + tuple(shape), lambda *_: (layer,) + (0,) * nd,
                        pipeline_mode=pl.Buffered(1))


def _shift_rows(cur, tail, k):
    rolled = pltpu.roll(cur, k, 0)
    r = lax.broadcasted_iota(jnp.int32, (SUBLANES, cur.shape[1]), 0)
    head = jnp.where(r < k, pltpu.roll(tail, k, 0), rolled[0:SUBLANES])
    return jnp.concatenate([head, rolled[SUBLANES:]], axis=0)


def _rms_mod(x, scale, shift):
    xn = x * lax.rsqrt(jnp.mean(x * x, axis=-1, keepdims=True) + EPS)
    return xn * (1.0 + scale) + shift


def _rmsnorm(x):
    return x * lax.rsqrt(jnp.mean(x * x, axis=-1, keepdims=True) + EPS)


def _layernorm(x, g, b):
    mu = jnp.mean(x, axis=-1, keepdims=True)
    xc = x - mu
    var = jnp.mean(xc * xc, axis=-1, keepdims=True)
    return xc * lax.rsqrt(var + EPS) * g + b


def _dot(a, b):
    return jnp.dot(a, b, preferred_element_type=F32)


ADA_ROWS = 40
ADA_TN = 1024


def _ada_kernel(c_ref, w_ref, b_ref, o_ref):
    c = c_ref[...]
    s = (c * jax.nn.sigmoid(c)).astype(BF16)
    o_ref[0] = _dot(s, w_ref[0].astype(BF16)) + b_ref[0]


def _ada(c_all, w_ada, b_ada):
    depth = w_ada.shape[0]
    n_out = w_ada.shape[2]
    return pl.pallas_call(
        _ada_kernel,
        grid=(depth, n_out // ADA_TN),
        in_specs=[
            pl.BlockSpec((ADA_ROWS, D_MODEL), lambda l, j: (0, 0)),
            pl.BlockSpec((1, D_MODEL, ADA_TN), lambda l, j: (l, 0, j)),
            pl.BlockSpec((1, 1, ADA_TN), lambda l, j: (l, 0, j)),
        ],
        out_specs=pl.BlockSpec((1, ADA_ROWS, ADA_TN), lambda l, j: (l, 0, j)),
        out_shape=jax.ShapeDtypeStruct((depth, ADA_ROWS, n_out), F32),
        compiler_params=_cparams("arbitrary", "arbitrary"),
        name="ada_mod",
    )(c_all, w_ada, b_ada.reshape(depth, 1, n_out))


def _even_in_kernel(x_ref, mod_ref, w_ref, lng_ref, lnb_ref, ws_ref, bst_ref,
                    a_ref, q_ref, k_ref, v_ref, kt_ref, vt_ref):
    base = 2 * W_A
    row = lax.broadcasted_iota(jnp.int32, (CHUNK, CHUNK), 0)
    col = lax.broadcasted_iota(jnp.int32, (CHUNK, CHUNK), 1)
    causal = col <= row
    bst = bst_ref[...]

    def front(u):
        m = mod_ref[u]
        h = _rms_mod(x_ref[u], m[1:2], m[0:1]).astype(BF16)
        va = _layernorm(jax.nn.gelu(_dot(h, w_ref[:, W_A:2 * W_A])), lng_ref[...], lnb_ref[...])
        ua = jax.nn.gelu(_dot(h, w_ref[:, 0:W_A]))
        q = _dot(h, w_ref[:, base:base + W_B])
        k = _dot(h, w_ref[:, base + W_B:base + 2 * W_B])
        for s in range(N_SLABS):
            q_ref[s, u] = q[:, s * LANES:(s + 1) * LANES]
            k_ref[s, u] = k[:, s * LANES:(s + 1) * LANES]
        return h, va.astype(BF16), ua, k

    def back(u, h, vab, ua):
        for g in range(G_A):
            wg = jnp.where(causal, ws_ref[g], 0.0).astype(BF16)
            bias = bst[:, g:g + 1]
            lo, hi = g * A_GROUP, (g + 1) * A_GROUP
            for c in range(TM // CHUNK):
                r0, r1 = c * CHUNK, (c + 1) * CHUNK
                mix = _dot(wg, vab[r0:r1, lo:hi]) + bias
                a_ref[u, r0:r1, lo:hi] = (ua[r0:r1, lo:hi] * mix).astype(BF16)
        v = _dot(h, w_ref[:, base + 2 * W_B:base + 3 * W_B])
        for s in range(N_SLABS):
            v_ref[s, u] = v[:, s * LANES:(s + 1) * LANES]
        return v

    fronts = [front(u) for u in range(SEQ_PAIR)]
    vs = [back(u, fronts[u][0], fronts[u][1], fronts[u][2]) for u in range(SEQ_PAIR)]
    for u in range(SEQ_PAIR):
        kt_ref[u] = fronts[u][3].T
        vt_ref[u] = vs[u].T


def _even_in(x2d, mod, w_in, ln_g, ln_b, w_s, b_s_t):
    n_tok = x2d.shape[0]
    first_win_tile = TILES_PER_SEQ - WIN // TM
    pair_tile = lambda i: (i // TILES_PER_SEQ, i % TILES_PER_SEQ, 0)
    qkv_shape = jax.ShapeDtypeStruct((N_SLABS, BATCH, SEQ, LANES), F32)
    qkv_spec = pl.BlockSpec((N_SLABS, SEQ_PAIR, TM, LANES),
                            lambda i: (0, i // TILES_PER_SEQ, i % TILES_PER_SEQ, 0))
    win_shape = jax.ShapeDtypeStruct((BATCH, W_B, WIN), F32)
    win_spec = pl.BlockSpec(
        (SEQ_PAIR, W_B, TM),
        lambda i: (i // TILES_PER_SEQ, 0, jnp.maximum(i % TILES_PER_SEQ - first_win_tile, 0)))
    a_out, q, k, v, kt, vt = pl.pallas_call(
        _even_in_kernel,
        grid=(N_TILES // SEQ_PAIR,),
        in_specs=[
            pl.BlockSpec((SEQ_PAIR, TM, D_MODEL), pair_tile),
            pl.BlockSpec((SEQ_PAIR, 6, D_MODEL), lambda i: (i // TILES_PER_SEQ, 0, 0)),
            _const_spec((D_MODEL, N_IN_EVEN)),
            _const_spec((1, W_A)),
            _const_spec((1, W_A)),
            _const_spec((G_A, CHUNK, CHUNK)),
            _const_spec((CHUNK, G_A)),
        ],
        out_specs=[
            pl.BlockSpec((SEQ_PAIR, TM, W_A), pair_tile),
            qkv_spec, qkv_spec, qkv_spec, win_spec, win_spec,
        ],
        out_shape=[
            jax.ShapeDtypeStruct((BATCH, SEQ, W_A), BF16),
            qkv_shape, qkv_shape, qkv_shape, win_shape, win_shape,
        ],
        compiler_params=_cparams("arbitrary"),
        name="even_in",
    )(x2d.reshape(BATCH, SEQ, D_MODEL), mod, w_in, ln_g, ln_b, w_s, b_s_t)
    flat = lambda a: a.reshape(N_SLABS, n_tok, LANES)
    return a_out.reshape(n_tok, W_A), flat(q), flat(k), flat(v), kt, vt


ATT_BLK = 128
ATT_UNROLL = 16


def _attn_kernel(q_ref, k_ref, v_ref, wu_hbm, wd_hbm, o_ref, wub_hbm, wdb_hbm,
                 acc_ref, max_ref, den_ref, q4_ref, k4_ref, v4_ref,
                 wu_in, wd_in, wu_out, wd_out, in_sem, out_sem):
    step = pl.program_id(0) * pl.num_programs(1) + pl.program_id(1)
    n_steps = pl.num_programs(0) * pl.num_programs(1)

    def cast_in(c):
        return (pltpu.make_async_copy(wu_hbm.at[pl.ds(c * wu_in.shape[0], wu_in.shape[0])], wu_in,
                                      in_sem.at[0]),
                pltpu.make_async_copy(wd_hbm.at[pl.ds(c * wd_in.shape[0], wd_in.shape[0])], wd_in,
                                      in_sem.at[1]))

    def cast_out(c):
        return (pltpu.make_async_copy(wu_out, wub_hbm.at[pl.ds(c * wu_in.shape[0], wu_in.shape[0])],
                                      out_sem.at[0]),
                pltpu.make_async_copy(wd_out, wdb_hbm.at[pl.ds(c * wd_in.shape[0], wd_in.shape[0])],
                                      out_sem.at[1]))

    for cp in cast_in(step):
        cp.start()

    q2, k2, v2 = q_ref.at[0], k_ref.at[0], v_ref.at[0]
    seq4 = SEQ // 4
    for src, dst in ((q2, q4_ref), (k2, k4_ref), (v2, v4_ref)):
        for c0 in range(4):
            for r in range(0, seq4, 2 * ATT_BLK):
                dst[c0 * seq4 + r:c0 * seq4 + r + 2 * ATT_BLK, :] = (
                    src[pl.ds(c0 + 4 * r, 2 * ATT_BLK, stride=4), :])
    qi = lax.broadcasted_iota(jnp.int32, (ATT_BLK, 2 * ATT_BLK), 0)
    kj = lax.broadcasted_iota(jnp.int32, (ATT_BLK, 2 * ATT_BLK), 1)
    dist = qi + ATT_BLK - kj
    band = (dist >= 0) & (dist <= N_BACK)
    bias_full = jnp.where(band, 0.0, NEG_INF).astype(F32)
    bias_first = jnp.where(band & (kj >= ATT_BLK), 0.0, NEG_INF).astype(F32)
    lane = lax.broadcasted_iota(jnp.int32, (ATT_BLK, LANES), 1)
    head0 = lane < HD

    for p, d in enumerate(DILATIONS):
        n_blk = SEQ // (d * ATT_BLK)

        def unit(idx, carry, p=p, d=d, n_blk=n_blk):
            k_prev, v_prev = carry
            c = idx // n_blk
            b = idx % n_blk
            if d == 1:
                qs, ks, vs = q2, k2, v2
                rows = pl.ds(pl.multiple_of(ATT_BLK * b, ATT_BLK), ATT_BLK)
            elif d == 4:
                qs, ks, vs = q4_ref, k4_ref, v4_ref
                rows = pl.ds(pl.multiple_of(c * seq4 + ATT_BLK * b, ATT_BLK), ATT_BLK)
            else:
                qs, ks, vs = q4_ref, k4_ref, v4_ref
                rows = pl.ds((c % 4) * seq4 + c // 4 + 4 * ATT_BLK * b, ATT_BLK, stride=4)
            qb = qs[rows, :] * (HD ** -0.5 * LOG2_E)
            k_own = ks[rows, :].astype(BF16)
            v_own = vs[rows, :].astype(BF16)
            kb = jnp.concatenate([k_prev, k_own], axis=0)
            vb = jnp.concatenate([v_prev, v_own], axis=0)
            bias = jnp.where(b == 0, bias_first, bias_full)
            outs, mxs, dens = [], [], []
            for hh in range(2):
                sel = head0 if hh == 0 else jnp.logical_not(head0)
                qm = jnp.where(sel, qb, 0.0).astype(BF16)
                s = lax.dot_general(qm, kb, (((1,), (1,)), ((), ())),
                                    preferred_element_type=F32) + bias
                mx = jnp.max(s, axis=-1, keepdims=True)
                e = jnp.exp2(s - mx)
                dens.append(jnp.sum(e, axis=-1, keepdims=True))
                mxs.append(mx)
                outs.append(_dot(e.astype(BF16), vb))
            acc_ref[p, rows, :] = jnp.where(head0, outs[0], outs[1])
            max_ref[p, rows, :] = jnp.where(head0, mxs[0], mxs[1])
            den_ref[p, rows, :] = jnp.where(head0, dens[0], dens[1])
            return k_own, v_own

        zeros = jnp.zeros((ATT_BLK, LANES), BF16)
        lax.fori_loop(0, SEQ // ATT_BLK, unit, (zeros, zeros), unroll=ATT_UNROLL)

    def mix(t, carry):
        c0 = t // (seq4 // TM)
        r = (t % (seq4 // TM)) * TM
        rows4 = pl.ds(pl.multiple_of(c0 * seq4 + r, TM), TM)
        rows = pl.ds(c0 + 4 * r, TM, stride=4)
        m0, m1, m2 = max_ref[0, rows, :], max_ref[1, rows4, :], max_ref[2, rows4, :]
        mx = jnp.maximum(jnp.maximum(m0, m1), m2)
        e0, e1, e2 = jnp.exp2(m0 - mx), jnp.exp2(m1 - mx), jnp.exp2(m2 - mx)
        num = e0 * acc_ref[0, rows, :] + e1 * acc_ref[1, rows4, :] + e2 * acc_ref[2, rows4, :]
        den = e0 * den_ref[0, rows, :] + e1 * den_ref[1, rows4, :] + e2 * den_ref[2, rows4, :]
        acc_ref[0, rows, :] = num / den
        return carry

    lax.fori_loop(0, SEQ // TM, mix, 0)

    def emit(t, carry):
        rows = pl.ds(pl.multiple_of(t * TM, TM), TM)
        o_ref[rows, :] = acc_ref[0, rows, :].astype(BF16)
        return carry

    lax.fori_loop(0, SEQ // TM, emit, 0)

    for cp in cast_in(step):
        cp.wait()

    @pl.when(step > 0)
    def _():
        for cp in cast_out(step - 1):
            cp.wait()

    wu_out[...] = wu_in[...].astype(BF16)
    wd_out[...] = wd_in[...].astype(BF16)
    for cp in cast_out(step):
        cp.start()

    @pl.when(step == n_steps - 1)
    def _():
        for cp in cast_out(step):
            cp.wait()


def _attn(q, k, v, w_up, w_down):
    n_tok = q.shape[1]
    n_steps = BATCH * N_SLABS
    wu2 = w_up.reshape(-1, w_up.shape[-1])
    wd2 = w_down.reshape(-1, w_down.shape[-1])
    ru, rd = wu2.shape[0] // n_steps, wd2.shape[0] // n_steps
    assert ru * n_steps == wu2.shape[0] and rd * n_steps == wd2.shape[0] and ru % 16 == 0 and rd % 16 == 0
    spec = pl.BlockSpec((1, SEQ, LANES), lambda n, s: (s, n, 0))
    any_spec = pl.BlockSpec(memory_space=pl.ANY)
    b_out, wub, wdb = pl.pallas_call(
        _attn_kernel,
        grid=(BATCH, N_SLABS),
        in_specs=[spec, spec, spec, any_spec, any_spec],
        out_specs=[pl.BlockSpec((SEQ, LANES), lambda n, s: (n, s)), any_spec, any_spec],
        out_shape=[jax.ShapeDtypeStruct((n_tok, W_B), BF16),
                   jax.ShapeDtypeStruct(wu2.shape, BF16), jax.ShapeDtypeStruct(wd2.shape, BF16)],
        scratch_shapes=([pltpu.VMEM((len(DILATIONS), SEQ, LANES), F32)] * 3
                        + [pltpu.VMEM((SEQ, LANES), F32)] * 3
                        + [pltpu.VMEM((ru, wu2.shape[1]), F32), pltpu.VMEM((rd, wd2.shape[1]), F32),
                           pltpu.VMEM((ru, wu2.shape[1]), BF16), pltpu.VMEM((rd, wd2.shape[1]), BF16),
                           pltpu.SemaphoreType.DMA((2,)), pltpu.SemaphoreType.DMA((2,))]),
        compiler_params=_cparams("arbitrary", "arbitrary"),
        name="dil_attn",
    )(q, k, v, wu2, wd2)
    return b_out, wub.reshape(w_up.shape), wdb.reshape(w_down.shape)


S_ATT_HEADS = 4
S_ATT_PHASES = N_HEADS // S_ATT_HEADS
FF_SPLIT = 6


def _ffn_kernel(*refs, pre, final, att):
    refs = list(refs)
    x_ref, mod_ref = refs[0:2]
    del refs[0:2]
    if pre:
        a_ref, b_ref, wo_ref = refs[0:3]
        del refs[0:3]
    wup_ref, cw_ref, cb_ref, wdn_ref, fg_ref = refs[0:5]
    del refs[0:5]
    if att:
        (qs_ref, kn_ref, vn_ref, kt_hbm, vt_hbm, o_ref, st_ref, bs_ref, kto_hbm, vto_hbm,
         carry_ref, act_ref, kin_ref, vin_ref, kout_ref, vout_ref, in_sem, out_sem) = refs
    else:
        o_ref, st_ref, carry_ref, act_ref = refs
    i = pl.program_id(0)
    last = pl.num_programs(0) - 1

    @pl.when(i % TILES_PER_SEQ == 0)
    def _():
        carry_ref[...] = jnp.zeros_like(carry_ref)

    def copies_in(seq, ph):
        hs = pl.ds(ph * S_ATT_HEADS, S_ATT_HEADS)
        return (pltpu.make_async_copy(kt_hbm.at[seq, hs], kin_ref.at[ph], in_sem.at[ph, 0]),
                pltpu.make_async_copy(vt_hbm.at[seq, hs], vin_ref.at[ph], in_sem.at[ph, 1]))

    def copies_out(seq, ph):
        hs = pl.ds(ph * S_ATT_HEADS, S_ATT_HEADS)
        return (pltpu.make_async_copy(kout_ref.at[ph], kto_hbm.at[seq, hs], out_sem.at[ph, 0]),
                pltpu.make_async_copy(vout_ref.at[ph], vto_hbm.at[seq, hs], out_sem.at[ph, 1]))

    def start(cps):
        for c in cps:
            c.start()

    def wait(cps):
        for c in cps:
            c.wait()

    def window_phase(ph):
        lanes = slice(ph * S_ATT_HEADS * HD, (ph + 1) * S_ATT_HEADS * HD)
        row = pl.ds(i, 1)
        o_row, kt_new, vt_new = _s_attn_heads(qs_ref[row, lanes], kn_ref[row, lanes],
                                              vn_ref[row, lanes], kin_ref[ph], vin_ref[ph])
        bs_ref[row, lanes] = o_row
        kout_ref[ph] = kt_new
        vout_ref[ph] = vt_new

    if att:
        @pl.when(i == 0)
        def _():
            kout_ref[1] = jnp.zeros(kout_ref.shape[1:], F32)
            vout_ref[1] = jnp.zeros(vout_ref.shape[1:], F32)
            start(copies_in(0, 0))
            start(copies_out(0, 1))

        wait(copies_in(i, 0))
        start(copies_in(i, 1))

        @pl.when(i > 0)
        def _():
            wait(copies_out(i - 1, 0))

        window_phase(0)

    m = mod_ref[0]
    cw = cw_ref[...]
    cb = cb_ref[...]
    x = x_ref[...]
    if pre:
        x = x + m[2:3] * (_dot(a_ref[...], wo_ref[0:W_A, :]) + _dot(b_ref[...], wo_ref[W_A:W_A + W_B, :]))
    h = _rms_mod(x, m[4:5], m[3:4]).astype(BF16)
    for j in range(N_FF_CHUNKS):
        if att and j == FF_SPLIT:
            start(copies_out(i, 0))
            wait(copies_in(i, 1))
            start(copies_in(jnp.minimum(i + 1, last), 0))
            wait(copies_out(jnp.maximum(i - 1, 0), 1))
            window_phase(1)
        conv = []
        for half in range(2):
            c0 = half * D_FF + j * FF_CHUNK
            cols = slice(c0, c0 + FF_CHUNK)
            up = _dot(h, wup_ref[:, cols])
            tail = carry_ref[:, cols]
            carry_ref[:, cols] = up[TM - SUBLANES:, :]
            conv.append(cb[:, cols] + cw[0:1, cols] * _shift_rows(up, tail, 2)
                        + cw[1:2, cols] * _shift_rows(up, tail, 1) + cw[2:3, cols] * up)
        act = conv[0] * jax.nn.sigmoid(conv[0]) * conv[1]
        act_ref[:, j * FF_CHUNK:(j + 1) * FF_CHUNK] = act.astype(BF16)
    st_ref[0] = carry_ref[...]
    y = x + m[5:6] * _dot(act_ref[...], wdn_ref[...])
    if final:
        y = _rmsnorm(y) * fg_ref[...]
    o_ref[...] = y

    if att:
        start(copies_out(i, 1))

        @pl.when(i == last)
        def _():
            wait(copies_out(i, 0))
            wait(copies_out(i, 1))
            wait(copies_in(i, 0))


def _ffn(x2d, mod, layer, w_up, conv_w, conv_b, w_down, final_g, final, pre=None, att=None):
    n_tok = x2d.shape[0]
    f2 = 2 * D_FF
    row_spec = pl.BlockSpec((TM, D_MODEL), lambda i: (i, 0))
    mod_spec = pl.BlockSpec((1, 6, D_MODEL), lambda i: (i // TILES_PER_SEQ, 0, 0))
    args, specs = [x2d, mod], [row_spec, mod_spec]
    if pre is not None:
        args += list(pre)
        specs += [pl.BlockSpec((TM, W_A), lambda i: (i, 0)),
                  pl.BlockSpec((TM, W_B), lambda i: (i, 0)),
                  _const_spec((W_A + W_B, D_MODEL))]
    args += [w_up, conv_w, conv_b, w_down, final_g]
    specs += [_layer_spec((D_MODEL, f2), layer), _layer_spec((3, f2), layer),
              _layer_spec((1, f2), layer), _layer_spec((D_FF, D_MODEL), layer),
              _const_spec((1, D_MODEL))]
    out_specs = [row_spec, pl.BlockSpec((1, SUBLANES, f2), lambda i: (i // TILES_PER_SEQ, 0, 0))]
    out_shape = [jax.ShapeDtypeStruct((n_tok, D_MODEL), F32),
                 jax.ShapeDtypeStruct((BATCH, SUBLANES, f2), F32)]
    scratch = [pltpu.VMEM((SUBLANES, f2), F32), pltpu.VMEM((TM, D_FF), BF16)]
    if att is not None:
        q_s, kt = att[0], att[3]
        assert q_s.shape[0] == n_tok // TM, "one sample sequence per grid step"
        any_spec = pl.BlockSpec(memory_space=pl.ANY)
        args += list(att)
        specs += [_const_spec(q_s.shape)] * 3 + [any_spec, any_spec]
        out_specs += [pl.BlockSpec(q_s.shape, lambda i: (0, 0)), any_spec, any_spec]
        out_shape += [jax.ShapeDtypeStruct(q_s.shape, F32),
                      jax.ShapeDtypeStruct(kt.shape, F32), jax.ShapeDtypeStruct(kt.shape, F32)]
        win_buf = pltpu.VMEM((S_ATT_PHASES, S_ATT_HEADS, HD, WIN), F32)
        scratch += [win_buf] * 4 + [pltpu.SemaphoreType.DMA((S_ATT_PHASES, 2))] * 2
    return pl.pallas_call(
        functools.partial(_ffn_kernel, pre=pre is not None, final=final, att=att is not None),
        grid=(n_tok // TM,),
        in_specs=specs,
        out_specs=out_specs,
        out_shape=out_shape,
        scratch_shapes=scratch,
        compiler_params=_cparams("arbitrary"),
        name="conv_ffn_final" if final else "conv_ffn",
    )(*args)


SEG_LEN = TM // SUBLANES
RG_PAIR = 2


def _rg_gates(xc, wa_ref, ba, wx_ref, bx, lam):
    xcb = xc.astype(BF16)
    r_parts, i_parts = [], []
    for g in range(RG_HEADS):
        blk = xcb[:, g * RG_BLOCK:(g + 1) * RG_BLOCK]
        r_parts.append(_dot(blk, wa_ref[g]))
        i_parts.append(_dot(blk, wx_ref[g]))
    r = jax.nn.sigmoid(jnp.concatenate(r_parts, axis=-1) + ba)
    ig = jax.nn.sigmoid(jnp.concatenate(i_parts, axis=-1) + bx)
    log_a = r * ((-RG_C) * jax.nn.softplus(-lam))
    a = jnp.exp(log_a)
    b = jnp.sqrt(1.0 - a * a) * (ig * xc)
    return a, b


def _rglru_kernel(x_ref, mod_ref, perm_ref, unperm_ref, win_ref, cw_ref, cb_ref, wa_ref, ba_ref,
                  wx_ref, bx_ref, lam_ref, wout_ref, o_ref, cst_ref, hst_ref,
                  xtail_ref, hcar_ref, a_ref, b_ref, hs_ref, ac_ref):
    t = pl.program_id(0) % TILES_PER_SEQ

    @pl.when(t == 0)
    def _():
        xtail_ref[...] = jnp.zeros_like(xtail_ref)
        hcar_ref[...] = jnp.zeros_like(hcar_ref)

    sub = lax.broadcasted_iota(jnp.int32, (SUBLANES, D_RNN), 0)
    cw = cw_ref[...]

    def front(u):
        m = mod_ref[u]
        h = _rms_mod(x_ref[u], m[1:2], m[0:1]).astype(BF16)
        hp = _dot(perm_ref[...], h).astype(BF16)
        xr3 = _dot(hp, win_ref[:, D_RNN:2 * D_RNN]).reshape(SEG_LEN, SUBLANES, D_RNN)
        gate = _dot(hp, win_ref[:, 0:D_RNN])
        tail = xtail_ref[u]
        xc3 = cb_ref[...] + cw[3:4] * xr3
        for k in range(1, 4):
            wrap = [jnp.where(sub == 0, tail[SUBLANES - k + v:SUBLANES - k + v + 1, :],
                              pltpu.roll(xr3[SEG_LEN - k + v], 1, 0)) for v in range(k)]
            prev = jnp.concatenate([jnp.stack(wrap), xr3[0:SEG_LEN - k]], axis=0)
            xc3 = xc3 + cw[3 - k:4 - k] * prev
        last = jnp.zeros((SUBLANES, D_RNN), F32)
        for r in range(SUBLANES):
            last = jnp.where(sub == r, pltpu.roll(xr3[SEG_LEN - SUBLANES + r], r + 1, 0), last)
        xtail_ref[u] = last
        cst_ref[u] = last
        a, b = _rg_gates(xc3.reshape(TM, D_RNN), wa_ref, ba_ref[...], wx_ref, bx_ref[...],
                         lam_ref[...])
        a_ref[u] = a.reshape(SEG_LEN, SUBLANES, D_RNN)
        b_ref[u] = b.reshape(SEG_LEN, SUBLANES, D_RNN)
        return gate

    def scan(u):
        def step(v, carry):
            hl, ac = carry
            av = a_ref[u, v]
            hl = av * hl + b_ref[u, v]
            ac = av * ac
            hs_ref[u, v] = hl
            ac_ref[u, v] = ac
            return hl, ac

        zero = jnp.zeros((SUBLANES, D_RNN), F32)
        hl, ac = lax.fori_loop(0, SEG_LEN, step, (zero, zero + 1.0), unroll=True)
        h_in = jnp.where(sub == 0, hcar_ref[u], 0.0)
        for s in range(1, SUBLANES):
            h_in = jnp.where(sub == s, pltpu.roll(hl + ac * h_in, 1, 0), h_in)
        h_end = (hl + ac * h_in)[SUBLANES - 1:SUBLANES, :]
        hcar_ref[u] = h_end
        hst_ref[u] = h_end
        return hs_ref[u] + ac_ref[u] * h_in

    def back(u, gate, hs):
        y = (jax.nn.gelu(gate) * hs.reshape(TM, D_RNN)).astype(BF16)
        y = _dot(unperm_ref[...], y).astype(BF16)
        o_ref[u] = x_ref[u] + mod_ref[u][2:3] * _dot(y, wout_ref[...])

    gates = [front(u) for u in range(RG_PAIR)]
    for u in range(RG_PAIR):
        back(u, gates[u], scan(u))


def _rglru(x2d, mod, w_in, conv_w, conv_b, w_a, b_a, w_x, b_x, lam, w_out):
    n_tok = x2d.shape[0]
    x3 = x2d.reshape(BATCH, SEQ, D_MODEL)
    rho = jnp.arange(TM)
    perm = (jnp.arange(TM)[None, :] == ((rho % SUBLANES) * SEG_LEN + rho // SUBLANES)[:, None])
    perm = perm.astype(BF16)
    row_spec = pl.BlockSpec((RG_PAIR, TM, D_MODEL),
                            lambda i: (i // TILES_PER_SEQ, i % TILES_PER_SEQ, 0))

    def seq_spec(rows):
        return pl.BlockSpec((RG_PAIR, rows, D_RNN), lambda i: (i // TILES_PER_SEQ, 0, 0))

    out, cst, hst = pl.pallas_call(
        _rglru_kernel,
        grid=(n_tok // (RG_PAIR * TM),),
        in_specs=[
            row_spec,
            seq_spec(6),
            _const_spec((TM, TM)),
            _const_spec((TM, TM)),
            _const_spec((D_MODEL, 2 * D_RNN)),
            _const_spec((4, D_RNN)),
            _const_spec((1, D_RNN)),
            _const_spec((RG_HEADS, RG_BLOCK, RG_BLOCK)),
            _const_spec((1, D_RNN)),
            _const_spec((RG_HEADS, RG_BLOCK, RG_BLOCK)),
            _const_spec((1, D_RNN)),
            _const_spec((1, D_RNN)),
            _const_spec((D_RNN, D_MODEL)),
        ],
        out_specs=[row_spec, seq_spec(SUBLANES), seq_spec(1)],
        out_shape=[
            jax.ShapeDtypeStruct((BATCH, SEQ, D_MODEL), F32),
            jax.ShapeDtypeStruct((BATCH, SUBLANES, D_RNN), F32),
            jax.ShapeDtypeStruct((BATCH, 1, D_RNN), F32),
        ],
        scratch_shapes=[
            pltpu.VMEM((RG_PAIR, SUBLANES, D_RNN), F32),
            pltpu.VMEM((RG_PAIR, 1, D_RNN), F32),
        ] + [pltpu.VMEM((RG_PAIR, SEG_LEN, SUBLANES, D_RNN), F32)] * 4,
        compiler_params=_cparams("arbitrary"),
        name="rglru",
    )(x3, mod, perm, perm.T, w_in, conv_w, conv_b, w_a, b_a, w_x, b_x, lam, w_out)
    return out.reshape(n_tok, D_MODEL), cst, hst


def _s_even_in_kernel(x_ref, mod_ref, w_ref, lng_ref, lnb_ref, ws0_ref, bs0_ref,
                      va_ref, a_ref, q_ref, k_ref, v_ref):
    mod = mod_ref[...]
    h = _rms_mod(x_ref[...], mod[:, D_MODEL:2 * D_MODEL], mod[:, 0:D_MODEL]).astype(BF16)
    u = jax.nn.gelu(_dot(h, w_ref[:, 0:W_A]))
    va = _layernorm(jax.nn.gelu(_dot(h, w_ref[:, W_A:2 * W_A])), lng_ref[...], lnb_ref[...])
    va_ref[...] = va
    a_ref[...] = (u * (ws0_ref[...] * va + bs0_ref[...])).astype(BF16)
    base = 2 * W_A
    q_ref[...] = _dot(h, w_ref[:, base:base + W_B])
    k_ref[...] = _dot(h, w_ref[:, base + W_B:base + 2 * W_B])
    v_ref[...] = _dot(h, w_ref[:, base + 2 * W_B:base + 3 * W_B])


def _s_even_in(x, mod, w_in, ln_g, ln_b, ws0, bs0):
    n = x.shape[0]
    f = jax.ShapeDtypeStruct((n, W_B), F32)
    return pl.pallas_call(
        _s_even_in_kernel,
        out_shape=[f, jax.ShapeDtypeStruct((n, W_A), BF16), f, f, f],
        compiler_params=pltpu.CompilerParams(vmem_limit_bytes=VMEM_LIMIT),
        name="s_even_in",
    )(x, mod, w_in, ln_g, ln_b, ws0, bs0)


def _s_attn_heads(qrow, knrow, vnrow, kt3, vt3):
    nh = kt3.shape[0]
    n_hd = nh * HD
    qrow = qrow * (HD ** -0.5)
    hrow = lax.broadcasted_iota(jnp.int32, (SUBLANES, n_hd), 0)
    hcol = lax.broadcasted_iota(jnp.int32, (SUBLANES, n_hd), 1) // HD
    own = hrow == hcol
    qbd = jnp.where(own, jnp.broadcast_to(qrow, (SUBLANES, n_hd)), 0.0)
    kt = kt3.reshape(n_hd, WIN)
    vt = vt3.reshape(n_hd, WIN)
    s = _dot(qbd.astype(BF16), kt.astype(BF16))
    s_new = jnp.sum(qbd * knrow, axis=-1, keepdims=True)
    t = lax.broadcasted_iota(jnp.int32, (SUBLANES, WIN), 1)
    dist = WIN - t
    vtb = vt.astype(BF16)
    outs, lses = [], []
    for d in DILATIONS:
        valid = ((dist & (d - 1)) == 0) & (dist <= N_BACK * d)
        sm = jnp.where(valid, s, NEG_INF)
        mx = jnp.maximum(jnp.max(sm, axis=-1, keepdims=True), s_new)
        e = jnp.where(valid, jnp.exp(sm - mx), 0.0)
        e_new = jnp.exp(s_new - mx)
        den = jnp.sum(e, axis=-1, keepdims=True) + e_new
        o = lax.dot_general(e.astype(BF16), vtb, (((1,), (1,)), ((), ())),
                            preferred_element_type=F32)
        outs.append((o + e_new * vnrow) / den)
        lses.append(mx + jnp.log(den))
    mx = jnp.maximum(jnp.maximum(lses[0], lses[1]), lses[2])
    ws = [jnp.exp(l - mx) for l in lses]
    num = ws[0] * outs[0] + ws[1] * outs[1] + ws[2] * outs[2]
    mixed = num / (ws[0] + ws[1] + ws[2])
    o_row = jnp.sum(jnp.where(own, mixed, 0.0), axis=0, keepdims=True)

    r2 = lax.broadcasted_iota(jnp.int32, (n_hd, n_hd), 0)
    c2 = lax.broadcasted_iota(jnp.int32, (n_hd, n_hd), 1)
    diag = r2 == c2
    kcol = jnp.sum(jnp.where(diag, jnp.broadcast_to(knrow, (n_hd, n_hd)), 0.0), axis=-1, keepdims=True)
    vcol = jnp.sum(jnp.where(diag, jnp.broadcast_to(vnrow, (n_hd, n_hd)), 0.0), axis=-1, keepdims=True)
    tt = lax.broadcasted_iota(jnp.int32, (n_hd, WIN), 1)
    is_last = tt == WIN - 1
    kt_new = jnp.where(is_last, kcol, pltpu.roll(kt, WIN - 1, 1)).reshape(nh, HD, WIN)
    vt_new = jnp.where(is_last, vcol, pltpu.roll(vt, WIN - 1, 1)).reshape(nh, HD, WIN)
    return o_row, kt_new, vt_new


def _s_even_out_kernel(x_ref, mod_ref, a_ref, b_ref, w_ref, o_ref):
    g1 = mod_ref[:, 2 * D_MODEL:3 * D_MODEL]
    mo = _dot(a_ref[...], w_ref[0:W_A, :]) + _dot(b_ref[...].astype(BF16), w_ref[W_A:W_A + W_B, :])
    o_ref[...] = x_ref[...] + g1 * mo


def _s_even_out(x, mod, a_out, b_out, w_out):
    return pl.pallas_call(
        _s_even_out_kernel,
        out_shape=jax.ShapeDtypeStruct(x.shape, F32),
        compiler_params=pltpu.CompilerParams(vmem_limit_bytes=VMEM_LIMIT),
        name="s_even_out",
    )(x, mod, a_out, b_out, w_out)


def _s_ffn_kernel(x_ref, mod_ref, p2_ref, p1_ref, wup_ref, cw_ref, cb_ref, wdn_ref, fg_ref,
                  o_ref, up_ref, *, final):
    x = x_ref[...]
    mod = mod_ref[...]
    h = _rms_mod(x, mod[:, 4 * D_MODEL:5 * D_MODEL], mod[:, 3 * D_MODEL:4 * D_MODEL]).astype(BF16)
    up = _dot(h, wup_ref[...])
    up_ref[...] = up
    cw = cw_ref[...]
    conv = cb_ref[...] + cw[0:1] * p2_ref[...] + cw[1:2] * p1_ref[...] + cw[2:3] * up
    ca, cg = conv[:, 0:D_FF], conv[:, D_FF:2 * D_FF]
    act = (ca * jax.nn.sigmoid(ca) * cg).astype(BF16)
    y = x + mod[:, 5 * D_MODEL:6 * D_MODEL] * _dot(act, wdn_ref[...])
    if final:
        y = _rmsnorm(y) * fg_ref[...]
    o_ref[...] = y


def _s_ffn(x, mod, p2, p1, layer, w_up, conv_w, conv_b, w_down, final_g, final):
    n = x.shape[0]
    f2 = 2 * D_FF
    return pl.pallas_call(
        functools.partial(_s_ffn_kernel, final=final),
        grid=(1,),
        in_specs=[_const_spec(x.shape), _const_spec(mod.shape), _const_spec(p2.shape),
                  _const_spec(p1.shape), _layer_spec((D_MODEL, f2), layer),
                  _layer_spec((3, f2), layer), _layer_spec((1, f2), layer),
                  _layer_spec((D_FF, D_MODEL), layer), _const_spec((1, D_MODEL))],
        out_specs=[pl.BlockSpec(x.shape, lambda i: (0, 0)), pl.BlockSpec((n, f2), lambda i: (0, 0))],
        out_shape=[jax.ShapeDtypeStruct(x.shape, F32), jax.ShapeDtypeStruct((n, f2), F32)],
        compiler_params=_cparams("arbitrary"),
        name="s_conv_ffn_final" if final else "s_conv_ffn",
    )(x, mod, p2, p1, w_up, conv_w, conv_b, w_down, final_g)


def _s_rglru_kernel(x_ref, mod_ref, c0_ref, c1_ref, c2_ref, h0_ref, win_ref, cw_ref, cb_ref,
                    wa_ref, ba_ref, wx_ref, bx_ref, lam_ref, wout_ref, o_ref, xr_ref, hn_ref):
    x = x_ref[...]
    mod = mod_ref[...]
    h = _rms_mod(x, mod[:, D_MODEL:2 * D_MODEL], mod[:, 0:D_MODEL]).astype(BF16)
    gate = _dot(h, win_ref[:, 0:D_RNN])
    xr = _dot(h, win_ref[:, D_RNN:2 * D_RNN])
    xr_ref[...] = xr
    cw = cw_ref[...]
    xc = (cb_ref[...] + cw[0:1] * c0_ref[...] + cw[1:2] * c1_ref[...] + cw[2:3] * c2_ref[...]
          + cw[3:4] * xr)
    a, b = _rg_gates(xc, wa_ref, ba_ref[...], wx_ref, bx_ref[...], lam_ref[...])
    hn = a * h0_ref[...] + b
    hn_ref[...] = hn
    y = (jax.nn.gelu(gate) * hn).astype(BF16)
    o_ref[...] = x + mod[:, 2 * D_MODEL:3 * D_MODEL] * _dot(y, wout_ref[...])


def _s_rglru(x, mod, c0, c1, c2, h0, w_in, conv_w, conv_b, w_a, b_a, w_x, b_x, lam, w_out):
    f = jax.ShapeDtypeStruct(x.shape, F32)
    return pl.pallas_call(
        _s_rglru_kernel,
        out_shape=[f, f, f],
        compiler_params=pltpu.CompilerParams(vmem_limit_bytes=VMEM_LIMIT),
        name="s_rglru",
    )(x, mod, c0, c1, c2, h0, w_in, conv_w, conv_b, w_a, b_a, w_x, b_x, lam, w_out)


def kernel(x_prompt, x_sample, cache_win_k, cache_win_v, state_rglru_conv, state_rglru_h, state_ffn_conv, c_prompt, c_sample, w_ada, b_ada, w_in_even, ln_v_g, ln_v_b, w_sgu, b_sgu, w_out_even, w_in_odd, rg_conv_w, rg_conv_b, rg_w_a, rg_b_a, rg_w_x, rg_b_x, rg_lambda, w_out_odd, ffn_w_up, ffn_conv_w, ffn_conv_b, ffn_w_down, final_g):
    w_in_even_b = w_in_even[0].astype(BF16)
    w_out_even_b = w_out_even[0].astype(BF16)
    w_in_odd_b = w_in_odd[0].astype(BF16)
    w_out_odd_b = w_out_odd[0].astype(BF16)
    rg_w_a_b = rg_w_a[0].astype(BF16)
    rg_w_x_b = rg_w_x[0].astype(BF16)
    conv_b3 = ffn_conv_b.reshape(2, 1, 2 * D_FF)
    final_g2 = final_g.reshape(1, D_MODEL)

    pad = jnp.zeros((ADA_ROWS - BATCH - DEC_BATCH, D_MODEL), F32)
    mod = _ada(jnp.concatenate([c_prompt, c_sample, pad], axis=0), w_ada, b_ada)
    mod_p = mod[:, :BATCH].reshape(2, BATCH, 6, D_MODEL)
    mod_s = mod[:, BATCH:BATCH + DEC_BATCH]

    xs = x_sample.reshape(DEC_BATCH, D_MODEL)
    ws0 = jnp.repeat(w_sgu[0, :, 0, 0], A_GROUP).reshape(1, W_A)
    bs0 = jnp.repeat(b_sgu[0, :, 0], A_GROUP).reshape(1, W_A)
    va_s, a_s, q_s, k_s, v_s = _s_even_in(xs, mod_s[0], w_in_even_b, ln_v_g, ln_v_b, ws0, bs0)
    kt_c = cache_win_k[0].transpose(0, 2, 3, 1)
    vt_c = cache_win_v[0].transpose(0, 2, 3, 1)

    x = x_prompt.reshape(BATCH * SEQ, D_MODEL)
    a_out, q, k, v, kt_p, vt_p = _even_in(x, mod_p[0], w_in_even_b, ln_v_g, ln_v_b, w_sgu[0],
                                          b_sgu[0].T)
    b_out, w_up_b, w_down_b = _attn(q, k, v, ffn_w_up, ffn_w_down)
    x, ffn_st0 = _ffn(x, mod_p[0], 0, w_up_b, ffn_conv_w, conv_b3, w_down_b, final_g2, False,
                      pre=(a_out, b_out, w_out_even_b))
    x, rg_cst, rg_hst = _rglru(x, mod_p[1], w_in_odd_b, rg_conv_w[0], rg_conv_b, rg_w_a_b, rg_b_a,
                               rg_w_x_b, rg_b_x, rg_lambda, w_out_odd_b)
    y_p, ffn_st1, b_s, kt_n, vt_n = _ffn(x, mod_p[1], 1, w_up_b, ffn_conv_w, conv_b3, w_down_b,
                                         final_g2, True, att=(q_s, k_s, v_s, kt_c, vt_c))

    y_prompt = y_p.reshape(BATCH, SEQ, D_MODEL)
    to_win = lambda t: t.reshape(1, BATCH, N_HEADS, HD, WIN).transpose(0, 1, 4, 2, 3)
    win_k_prompt = to_win(kt_p)
    win_v_prompt = to_win(vt_p)
    rglru_conv_prompt = rg_cst[None, :, SUBLANES - 3:, :]
    rglru_h_prompt = rg_hst.reshape(1, BATCH, D_RNN)
    ffn_conv_prompt = jnp.stack([ffn_st0[:, SUBLANES - 2:, :], ffn_st1[:, SUBLANES - 2:, :]])

    xs = _s_even_out(xs, mod_s[0], a_s, b_s, w_out_even_b)
    st0 = state_ffn_conv[0]
    xs, up0 = _s_ffn(xs, mod_s[0], st0[:, 0], st0[:, 1], 0, w_up_b, ffn_conv_w, conv_b3, w_down_b,
                     final_g2, False)
    cst = state_rglru_conv[0]
    xs, xr_s, hn_s = _s_rglru(xs, mod_s[1], cst[:, 0], cst[:, 1], cst[:, 2], state_rglru_h[0],
                              w_in_odd_b, rg_conv_w[0], rg_conv_b, rg_w_a_b, rg_b_a, rg_w_x_b,
                              rg_b_x, rg_lambda, w_out_odd_b)
    st1 = state_ffn_conv[1]
    ys, up1 = _s_ffn(xs, mod_s[1], st1[:, 0], st1[:, 1], 1, w_up_b, ffn_conv_w, conv_b3, w_down_b,
                     final_g2, True)

    y_sample = ys.reshape(DEC_BATCH, 1, D_MODEL)
    chunk_v_sample = va_s.reshape(1, DEC_BATCH, 1, W_A)
    win_k_sample = kt_n.transpose(0, 3, 1, 2)[None]
    win_v_sample = vt_n.transpose(0, 3, 1, 2)[None]
    rglru_conv_sample = jnp.stack([cst[:, 1], cst[:, 2], xr_s], axis=1)[None]
    rglru_h_sample = hn_s[None]
    ffn_conv_sample = jnp.stack([jnp.stack([st0[:, 1], up0], axis=1),
                                 jnp.stack([st1[:, 1], up1], axis=1)])

    return (y_prompt, y_sample, win_k_prompt, win_v_prompt, rglru_conv_prompt, rglru_h_prompt,
            ffn_conv_prompt, chunk_v_sample, win_k_sample, win_v_sample, rglru_conv_sample,
            rglru_h_sample, ffn_conv_sample)
```

```python
import functools

import jax
import jax.numpy as jnp
from jax import lax
from jax.experimental import pallas as pl
from jax.experimental.pallas import tpu as pltpu

F32 = jnp.float32
BF16 = jnp.bfloat16

D_MODEL = 1024
BATCH = 4
SEQ = 4096
DEC_BATCH = 32
W_A = 512
A_GROUP = 128
G_A = 4
CHUNK = 128
W_B = 512
HD = 64
N_HEADS = 8
DILATIONS = (1, 4, 16)
N_BACK = 128
WIN = 2048
N_IN_EVEN = 2 * W_A + 3 * W_B
D_RNN = 1024
RG_BLOCK = 128
RG_HEADS = 8
RG_C = 8.0
D_FF = 2816
EPS = 1e-6
NEG_INF = -1e30
LOG2_E = 1.4426950408889634

LANES = 128
SUBLANES = 8
TM = 512
TILES_PER_SEQ = SEQ // TM
N_TILES = BATCH * TILES_PER_SEQ
FF_CHUNK = 256
N_FF_CHUNKS = D_FF // FF_CHUNK
N_SLABS = W_B // LANES
SEQ_PAIR = 2
VMEM_LIMIT = 56 * 1024 * 1024


def _cparams(*sem):
    return pltpu.CompilerParams(dimension_semantics=sem, vmem_limit_bytes=VMEM_LIMIT)


def _const_spec(shape):
    nd = len(shape)
    return pl.BlockSpec(shape, lambda *_: (0,) * nd, pipeline_mode=pl.Buffered(1))


def _layer_spec(shape, layer):
    nd = len(shape)
    return pl.BlockSpec((None,) + tuple(shape), lambda *_: (layer,) + (0,) * nd,
                        pipeline_mode=pl.Buffered(1))


def _shift_rows(cur, tail, k):
    rolled = pltpu.roll(cur, k, 0)
    r = lax.broadcasted_iota(jnp.int32, (SUBLANES, cur.shape[1]), 0)
    head = jnp.where(r < k, pltpu.roll(tail, k, 0), rolled[0:SUBLANES])
    return jnp.concatenate([head, rolled[SUBLANES:]], axis=0)


def _rms_mod(x, scale, shift):
    xn = x * lax.rsqrt(jnp.mean(x * x, axis=-1, keepdims=True) + EPS)
    return xn * (1.0 + scale) + shift


def _rmsnorm(x):
    return x * lax.rsqrt(jnp.mean(x * x, axis=-1, keepdims=True) + EPS)


def _layernorm(x, g, b):
    mu = jnp.mean(x, axis=-1, keepdims=True)
    xc = x - mu
    var = jnp.mean(xc * xc, axis=-1, keepdims=True)
    return xc * lax.rsqrt(var + EPS) * g + b


def _dot(a, b):
    return jnp.dot(a, b, preferred_element_type=F32)


ADA_ROWS = 40
ADA_TN = 1024


def _ada_kernel(c_ref, w_ref, b_ref, o_ref):
    c = c_ref[...]
    s = (c * jax.nn.sigmoid(c)).astype(BF16)
    o_ref[0] = _dot(s, w_ref[0].astype(BF16)) + b_ref[0]


def _ada(c_all, w_ada, b_ada):
    depth = w_ada.shape[0]
    n_out = w_ada.shape[2]
    return pl.pallas_call(
        _ada_kernel,
        grid=(depth, n_out // ADA_TN),
        in_specs=[
            pl.BlockSpec((ADA_ROWS, D_MODEL), lambda l, j: (0, 0)),
            pl.BlockSpec((1, D_MODEL, ADA_TN), lambda l, j: (l, 0, j)),
            pl.BlockSpec((1, 1, ADA_TN), lambda l, j: (l, 0, j)),
        ],
        out_specs=pl.BlockSpec((1, ADA_ROWS, ADA_TN), lambda l, j: (l, 0, j)),
        out_shape=jax.ShapeDtypeStruct((depth, ADA_ROWS, n_out), F32),
        compiler_params=_cparams("arbitrary", "arbitrary"),
        name="ada_mod",
    )(c_all, w_ada, b_ada.reshape(depth, 1, n_out))


def _even_in_kernel(x_ref, mod_ref, w_ref, lng_ref, lnb_ref, ws_ref, bst_ref,
                    a_ref, q_ref, k_ref, v_ref, kt_ref, vt_ref):
    base = 2 * W_A
    row = lax.broadcasted_iota(jnp.int32, (CHUNK, CHUNK), 0)
    col = lax.broadcasted_iota(jnp.int32, (CHUNK, CHUNK), 1)
    causal = col <= row
    bst = bst_ref[...]

    def front(u):
        m = mod_ref[u]
        h = _rms_mod(x_ref[u], m[1:2], m[0:1]).astype(BF16)
        va = _layernorm(jax.nn.gelu(_dot(h, w_ref[:, W_A:2 * W_A])), lng_ref[...], lnb_ref[...])
        ua = jax.nn.gelu(_dot(h, w_ref[:, 0:W_A]))
        q = _dot(h, w_ref[:, base:base + W_B])
        k = _dot(h, w_ref[:, base + W_B:base + 2 * W_B])
        for s in range(N_SLABS):
            q_ref[s, u] = q[:, s * LANES:(s + 1) * LANES]
            k_ref[s, u] = k[:, s * LANES:(s + 1) * LANES]
        return h, va.astype(BF16), ua, k

    def back(u, h, vab, ua):
        for g in range(G_A):
            wg = jnp.where(causal, ws_ref[g], 0.0).astype(BF16)
            bias = bst[:, g:g + 1]
            lo, hi = g * A_GROUP, (g + 1) * A_GROUP
            for c in range(TM // CHUNK):
                r0, r1 = c * CHUNK, (c + 1) * CHUNK
                mix = _dot(wg, vab[r0:r1, lo:hi]) + bias
                a_ref[u, r0:r1, lo:hi] = (ua[r0:r1, lo:hi] * mix).astype(BF16)
        v = _dot(h, w_ref[:, base + 2 * W_B:base + 3 * W_B])
        for s in range(N_SLABS):
            v_ref[s, u] = v[:, s * LANES:(s + 1) * LANES]
        return v

    fronts = [front(u) for u in range(SEQ_PAIR)]
    vs = [back(u, fronts[u][0], fronts[u][1], fronts[u][2]) for u in range(SEQ_PAIR)]
    for u in range(SEQ_PAIR):
        kt_ref[u] = fronts[u][3].T
        vt_ref[u] = vs[u].T


def _even_in(x2d, mod, w_in, ln_g, ln_b, w_s, b_s_t):
    n_tok = x2d.shape[0]
    first_win_tile = TILES_PER_SEQ - WIN // TM
    pair_tile = lambda i: (i // TILES_PER_SEQ, i % TILES_PER_SEQ, 0)
    qkv_shape = jax.ShapeDtypeStruct((N_SLABS, BATCH, SEQ, LANES), F32)
    qkv_spec = pl.BlockSpec((N_SLABS, SEQ_PAIR, TM, LANES),
                            lambda i: (0, i // TILES_PER_SEQ, i % TILES_PER_SEQ, 0))
    win_shape = jax.ShapeDtypeStruct((BATCH, W_B, WIN), F32)
    win_spec = pl.BlockSpec(
        (SEQ_PAIR, W_B, TM),
        lambda i: (i // TILES_PER_SEQ, 0, jnp.maximum(i % TILES_PER_SEQ - first_win_tile, 0)))
    a_out, q, k, v, kt, vt = pl.pallas_call(
        _even_in_kernel,
        grid=(N_TILES // SEQ_PAIR,),
        in_specs=[
            pl.BlockSpec((SEQ_PAIR, TM, D_MODEL), pair_tile),
            pl.BlockSpec((SEQ_PAIR, 6, D_MODEL), lambda i: (i // TILES_PER_SEQ, 0, 0)),
            _const_spec((D_MODEL, N_IN_EVEN)),
            _const_spec((1, W_A)),
            _const_spec((1, W_A)),
            _const_spec((G_A, CHUNK, CHUNK)),
            _const_spec((CHUNK, G_A)),
        ],
        out_specs=[
            pl.BlockSpec((SEQ_PAIR, TM, W_A), pair_tile),
            qkv_spec, qkv_spec, qkv_spec, win_spec, win_spec,
        ],
        out_shape=[
            jax.ShapeDtypeStruct((BATCH, SEQ, W_A), BF16),
            qkv_shape, qkv_shape, qkv_shape, win_shape, win_shape,
        ],
        compiler_params=_cparams("arbitrary"),
        name="even_in",
    )(x2d.reshape(BATCH, SEQ, D_MODEL), mod, w_in, ln_g, ln_b, w_s, b_s_t)
    flat = lambda a: a.reshape(N_SLABS, n_tok, LANES)
    return a_out.reshape(n_tok, W_A), flat(q), flat(k), flat(v), kt, vt


ATT_BLK = 128
ATT_UNROLL = 32


def _attn_kernel(q_ref, k_ref, v_ref, wu_hbm, wd_hbm, o_ref, wub_hbm, wdb_hbm,
                 acc_ref, max_ref, den_ref, q4_ref, k4_ref, v4_ref,
                 wu_in, wd_in, wu_out, wd_out, in_sem, out_sem):
    step = pl.program_id(0) * pl.num_programs(1) + pl.program_id(1)
    n_steps = pl.num_programs(0) * pl.num_programs(1)

    def cast_in(c):
        return (pltpu.make_async_copy(wu_hbm.at[pl.ds(c * wu_in.shape[0], wu_in.shape[0])], wu_in,
                                      in_sem.at[0]),
                pltpu.make_async_copy(wd_hbm.at[pl.ds(c * wd_in.shape[0], wd_in.shape[0])], wd_in,
                                      in_sem.at[1]))

    def cast_out(c):
        return (pltpu.make_async_copy(wu_out, wub_hbm.at[pl.ds(c * wu_in.shape[0], wu_in.shape[0])],
                                      out_sem.at[0]),
                pltpu.make_async_copy(wd_out, wdb_hbm.at[pl.ds(c * wd_in.shape[0], wd_in.shape[0])],
                                      out_sem.at[1]))

    for cp in cast_in(step):
        cp.start()

    q2, k2, v2 = q_ref.at[0], k_ref.at[0], v_ref.at[0]
    seq4 = SEQ // 4
    for src, dst in ((q2, q4_ref), (k2, k4_ref), (v2, v4_ref)):
        for c0 in range(4):
            for r in range(0, seq4, 2 * ATT_BLK):
                dst[c0 * seq4 + r:c0 * seq4 + r + 2 * ATT_BLK, :] = (
                    src[pl.ds(c0 + 4 * r, 2 * ATT_BLK, stride=4), :])
    qi = lax.broadcasted_iota(jnp.int32, (ATT_BLK, 2 * ATT_BLK), 0)
    kj = lax.broadcasted_iota(jnp.int32, (ATT_BLK, 2 * ATT_BLK), 1)
    dist = qi + ATT_BLK - kj
    band = (dist >= 0) & (dist <= N_BACK)
    bias_full = jnp.where(band, 0.0, NEG_INF).astype(F32)
    bias_first = jnp.where(band & (kj >= ATT_BLK), 0.0, NEG_INF).astype(F32)
    lane = lax.broadcasted_iota(jnp.int32, (ATT_BLK, LANES), 1)
    head0 = lane < HD

    for p, d in enumerate(DILATIONS):
        n_blk = SEQ // (d * ATT_BLK)

        def unit(idx, carry, p=p, d=d, n_blk=n_blk):
            k_prev, v_prev = carry
            c = idx // n_blk
            b = idx % n_blk
            if d == 1:
                qs, ks, vs = q2, k2, v2
                rows = pl.ds(pl.multiple_of(ATT_BLK * b, ATT_BLK), ATT_BLK)
            elif d == 4:
                qs, ks, vs = q4_ref, k4_ref, v4_ref
                rows = pl.ds(pl.multiple_of(c * seq4 + ATT_BLK * b, ATT_BLK), ATT_BLK)
            else:
                qs, ks, vs = q4_ref, k4_ref, v4_ref
                rows = pl.ds((c % 4) * seq4 + c // 4 + 4 * ATT_BLK * b, ATT_BLK, stride=4)
            qb = qs[rows, :] * (HD ** -0.5 * LOG2_E)
            k_own = ks[rows, :].astype(BF16)
            v_own = vs[rows, :].astype(BF16)
            kb = jnp.concatenate([k_prev, k_own], axis=0)
            vb = jnp.concatenate([v_prev, v_own], axis=0)
            bias = jnp.where(b == 0, bias_first, bias_full)
            outs, mxs, dens = [], [], []
            for hh in range(2):
                sel = head0 if hh == 0 else jnp.logical_not(head0)
                qm = jnp.where(sel, qb, 0.0).astype(BF16)
                s = lax.dot_general(qm, kb, (((1,), (1,)), ((), ())),
                                    preferred_element_type=F32) + bias
                mx = jnp.max(s, axis=-1, keepdims=True)
                e = jnp.exp2(s - mx)
                dens.append(jnp.sum(e, axis=-1, keepdims=True))
                mxs.append(mx)
                outs.append(_dot(e.astype(BF16), vb))
            acc_ref[p, rows, :] = jnp.where(head0, outs[0], outs[1])
            max_ref[p, rows, :] = jnp.where(head0, mxs[0], mxs[1])
            den_ref[p, rows, :] = jnp.where(head0, dens[0], dens[1])
            return k_own, v_own

        zeros = jnp.zeros((ATT_BLK, LANES), BF16)
        lax.fori_loop(0, SEQ // ATT_BLK, unit, (zeros, zeros), unroll=ATT_UNROLL)

    def mix(t, carry):
        c0 = t // (seq4 // TM)
        r = (t % (seq4 // TM)) * TM
        rows4 = pl.ds(pl.multiple_of(c0 * seq4 + r, TM), TM)
        rows = pl.ds(c0 + 4 * r, TM, stride=4)
        m0, m1, m2 = max_ref[0, rows, :], max_ref[1, rows4, :], max_ref[2, rows4, :]
        mx = jnp.maximum(jnp.maximum(m0, m1), m2)
        e0, e1, e2 = jnp.exp2(m0 - mx), jnp.exp2(m1 - mx), jnp.exp2(m2 - mx)
        num = e0 * acc_ref[0, rows, :] + e1 * acc_ref[1, rows4, :] + e2 * acc_ref[2, rows4, :]
        den = e0 * den_ref[0, rows, :] + e1 * den_ref[1, rows4, :] + e2 * den_ref[2, rows4, :]
        acc_ref[0, rows, :] = num / den
        return carry

    lax.fori_loop(0, SEQ // TM, mix, 0)

    def emit(t, carry):
        rows = pl.ds(pl.multiple_of(t * TM, TM), TM)
        o_ref[rows, :] = acc_ref[0, rows, :].astype(BF16)
        return carry

    lax.fori_loop(0, SEQ // TM, emit, 0)

    for cp in cast_in(step):
        cp.wait()

    @pl.when(step > 0)
    def _():
        for cp in cast_out(step - 1):
            cp.wait()

    wu_out[...] = wu_in[...].astype(BF16)
    wd_out[...] = wd_in[...].astype(BF16)
    for cp in cast_out(step):
        cp.start()

    @pl.when(step == n_steps - 1)
    def _():
        for cp in cast_out(step):
            cp.wait()


def _attn(q, k, v, w_up, w_down):
    n_tok = q.shape[1]
    n_steps = BATCH * N_SLABS
    wu2 = w_up.reshape(-1, w_up.shape[-1])
    wd2 = w_down.reshape(-1, w_down.shape[-1])
    ru, rd = wu2.shape[0] // n_steps, wd2.shape[0] // n_steps
    assert ru * n_steps == wu2.shape[0] and rd * n_steps == wd2.shape[0] and ru % 16 == 0 and rd % 16 == 0
    spec = pl.BlockSpec((1, SEQ, LANES), lambda n, s: (s, n, 0))
    any_spec = pl.BlockSpec(memory_space=pl.ANY)
    b_out, wub, wdb = pl.pallas_call(
        _attn_kernel,
        grid=(BATCH, N_SLABS),
        in_specs=[spec, spec, spec, any_spec, any_spec],
        out_specs=[pl.BlockSpec((SEQ, LANES), lambda n, s: (n, s)), any_spec, any_spec],
        out_shape=[jax.ShapeDtypeStruct((n_tok, W_B), BF16),
                   jax.ShapeDtypeStruct(wu2.shape, BF16), jax.ShapeDtypeStruct(wd2.shape, BF16)],
        scratch_shapes=([pltpu.VMEM((len(DILATIONS), SEQ, LANES), F32)] * 3
                        + [pltpu.VMEM((SEQ, LANES), F32)] * 3
                        + [pltpu.VMEM((ru, wu2.shape[1]), F32), pltpu.VMEM((rd, wd2.shape[1]), F32),
                           pltpu.VMEM((ru, wu2.shape[1]), BF16), pltpu.VMEM((rd, wd2.shape[1]), BF16),
                           pltpu.SemaphoreType.DMA((2,)), pltpu.SemaphoreType.DMA((2,))]),
        compiler_params=_cparams("arbitrary", "arbitrary"),
        name="dil_attn",
    )(q, k, v, wu2, wd2)
    return b_out, wub.reshape(w_up.shape), wdb.reshape(w_down.shape)


S_ATT_HEADS = 4
S_ATT_PHASES = N_HEADS // S_ATT_HEADS
FF_SPLIT = 6


def _ffn_kernel(*refs, pre, final, att):
    refs = list(refs)
    x_ref, mod_ref = refs[0:2]
    del refs[0:2]
    if pre:
        a_ref, b_ref, wo_ref = refs[0:3]
        del refs[0:3]
    wup_ref, cw_ref, cb_ref, wdn_ref, fg_ref = refs[0:5]
    del refs[0:5]
    if att:
        (qs_ref, kn_ref, vn_ref, kt_hbm, vt_hbm, o_ref, st_ref, bs_ref, kto_hbm, vto_hbm,
         carry_ref, act_ref, kin_ref, vin_ref, kout_ref, vout_ref, in_sem, out_sem) = refs
    else:
        o_ref, st_ref, carry_ref, act_ref = refs
    i = pl.program_id(0)
    last = pl.num_programs(0) - 1

    @pl.when(i % TILES_PER_SEQ == 0)
    def _():
        carry_ref[...] = jnp.zeros_like(carry_ref)

    def copies_in(seq, ph):
        hs = pl.ds(ph * S_ATT_HEADS, S_ATT_HEADS)
        return (pltpu.make_async_copy(kt_hbm.at[seq, hs], kin_ref.at[ph], in_sem.at[ph, 0]),
                pltpu.make_async_copy(vt_hbm.at[seq, hs], vin_ref.at[ph], in_sem.at[ph, 1]))

    def copies_out(seq, ph):
        hs = pl.ds(ph * S_ATT_HEADS, S_ATT_HEADS)
        return (pltpu.make_async_copy(kout_ref.at[ph], kto_hbm.at[seq, hs], out_sem.at[ph, 0]),
                pltpu.make_async_copy(vout_ref.at[ph], vto_hbm.at[seq, hs], out_sem.at[ph, 1]))

    def start(cps):
        for c in cps:
            c.start()

    def wait(cps):
        for c in cps:
            c.wait()

    def window_phase(ph):
        lanes = slice(ph * S_ATT_HEADS * HD, (ph + 1) * S_ATT_HEADS * HD)
        row = pl.ds(i, 1)
        o_row, kt_new, vt_new = _s_attn_heads(qs_ref[row, lanes], kn_ref[row, lanes],
                                              vn_ref[row, lanes], kin_ref[ph], vin_ref[ph])
        bs_ref[row, lanes] = o_row
        kout_ref[ph] = kt_new
        vout_ref[ph] = vt_new

    if att:
        @pl.when(i == 0)
        def _():
            kout_ref[1] = jnp.zeros(kout_ref.shape[1:], F32)
            vout_ref[1] = jnp.zeros(vout_ref.shape[1:], F32)
            start(copies_in(0, 0))
            start(copies_out(0, 1))

        wait(copies_in(i, 0))
        start(copies_in(i, 1))

        @pl.when(i > 0)
        def _():
            wait(copies_out(i - 1, 0))

        window_phase(0)

    m = mod_ref[0]
    cw = cw_ref[...]
    cb = cb_ref[...]
    x = x_ref[...]
    if pre:
        x = x + m[2:3] * (_dot(a_ref[...], wo_ref[0:W_A, :]) + _dot(b_ref[...], wo_ref[W_A:W_A + W_B, :]))
    h = _rms_mod(x, m[4:5], m[3:4]).astype(BF16)
    for j in range(N_FF_CHUNKS):
        if att and j == FF_SPLIT:
            start(copies_out(i, 0))
            wait(copies_in(i, 1))
            start(copies_in(jnp.minimum(i + 1, last), 0))
            wait(copies_out(jnp.maximum(i - 1, 0), 1))
            window_phase(1)
        conv = []
        for half in range(2):
            c0 = half * D_FF + j * FF_CHUNK
            cols = slice(c0, c0 + FF_CHUNK)
            up = _dot(h, wup_ref[:, cols])
            tail = carry_ref[:, cols]
            carry_ref[:, cols] = up[TM - SUBLANES:, :]
            conv.append(cb[:, cols] + cw[0:1, cols] * _shift_rows(up, tail, 2)
                        + cw[1:2, cols] * _shift_rows(up, tail, 1) + cw[2:3, cols] * up)
        act = conv[0] * jax.nn.sigmoid(conv[0]) * conv[1]
        act_ref[:, j * FF_CHUNK:(j + 1) * FF_CHUNK] = act.astype(BF16)
    st_ref[0] = carry_ref[...]
    y = x + m[5:6] * _dot(act_ref[...], wdn_ref[...])
    if final:
        y = _rmsnorm(y) * fg_ref[...]
    o_ref[...] = y

    if att:
        start(copies_out(i, 1))

        @pl.when(i == last)
        def _():
            wait(copies_out(i, 0))
            wait(copies_out(i, 1))
            wait(copies_in(i, 0))


def _ffn(x2d, mod, layer, w_up, conv_w, conv_b, w_down, final_g, final, pre=None, att=None):
    n_tok = x2d.shape[0]
    f2 = 2 * D_FF
    row_spec = pl.BlockSpec((TM, D_MODEL), lambda i: (i, 0))
    mod_spec = pl.BlockSpec((1, 6, D_MODEL), lambda i: (i // TILES_PER_SEQ, 0, 0))
    args, specs = [x2d, mod], [row_spec, mod_spec]
    if pre is not None:
        args += list(pre)
        specs += [pl.BlockSpec((TM, W_A), lambda i: (i, 0)),
                  pl.BlockSpec((TM, W_B), lambda i: (i, 0)),
                  _const_spec((W_A + W_B, D_MODEL))]
    args += [w_up, conv_w, conv_b, w_down, final_g]
    specs += [_layer_spec((D_MODEL, f2), layer), _layer_spec((3, f2), layer),
              _layer_spec((1, f2), layer), _layer_spec((D_FF, D_MODEL), layer),
              _const_spec((1, D_MODEL))]
    out_specs = [row_spec, pl.BlockSpec((1, SUBLANES, f2), lambda i: (i // TILES_PER_SEQ, 0, 0))]
    out_shape = [jax.ShapeDtypeStruct((n_tok, D_MODEL), F32),
                 jax.ShapeDtypeStruct((BATCH, SUBLANES, f2), F32)]
    scratch = [pltpu.VMEM((SUBLANES, f2), F32), pltpu.VMEM((TM, D_FF), BF16)]
    if att is not None:
        q_s, kt = att[0], att[3]
        assert q_s.shape[0] == n_tok // TM, "one sample sequence per grid step"
        any_spec = pl.BlockSpec(memory_space=pl.ANY)
        args += list(att)
        specs += [_const_spec(q_s.shape)] * 3 + [any_spec, any_spec]
        out_specs += [pl.BlockSpec(q_s.shape, lambda i: (0, 0)), any_spec, any_spec]
        out_shape += [jax.ShapeDtypeStruct(q_s.shape, F32),
                      jax.ShapeDtypeStruct(kt.shape, F32), jax.ShapeDtypeStruct(kt.shape, F32)]
        win_buf = pltpu.VMEM((S_ATT_PHASES, S_ATT_HEADS, HD, WIN), F32)
        scratch += [win_buf] * 4 + [pltpu.SemaphoreType.DMA((S_ATT_PHASES, 2))] * 2
    return pl.pallas_call(
        functools.partial(_ffn_kernel, pre=pre is not None, final=final, att=att is not None),
        grid=(n_tok // TM,),
        in_specs=specs,
        out_specs=out_specs,
        out_shape=out_shape,
        scratch_shapes=scratch,
        compiler_params=_cparams("arbitrary"),
        name="conv_ffn_final" if final else "conv_ffn",
    )(*args)


SEG_LEN = TM // SUBLANES
RG_PAIR = 2


def _rg_gates(xc, wa_ref, ba, wx_ref, bx, lam):
    xcb = xc.astype(BF16)
    r_parts, i_parts = [], []
    for g in range(RG_HEADS):
        blk = xcb[:, g * RG_BLOCK:(g + 1) * RG_BLOCK]
        r_parts.append(_dot(blk, wa_ref[g]))
        i_parts.append(_dot(blk, wx_ref[g]))
    r = jax.nn.sigmoid(jnp.concatenate(r_parts, axis=-1) + ba)
    ig = jax.nn.sigmoid(jnp.concatenate(i_parts, axis=-1) + bx)
    log_a = r * ((-RG_C) * jax.nn.softplus(-lam))
    a = jnp.exp(log_a)
    b = jnp.sqrt(1.0 - a * a) * (ig * xc)
    return a, b


def _rglru_kernel(x_ref, mod_ref, perm_ref, unperm_ref, win_ref, cw_ref, cb_ref, wa_ref, ba_ref,
                  wx_ref, bx_ref, lam_ref, wout_ref, o_ref, cst_ref, hst_ref,
                  xtail_ref, hcar_ref, a_ref, b_ref, hs_ref, ac_ref):
    t = pl.program_id(0) % TILES_PER_SEQ

    @pl.when(t == 0)
    def _():
        xtail_ref[...] = jnp.zeros_like(xtail_ref)
        hcar_ref[...] = jnp.zeros_like(hcar_ref)

    sub = lax.broadcasted_iota(jnp.int32, (SUBLANES, D_RNN), 0)
    cw = cw_ref[...]

    def front(u):
        m = mod_ref[u]
        h = _rms_mod(x_ref[u], m[1:2], m[0:1]).astype(BF16)
        hp = _dot(perm_ref[...], h).astype(BF16)
        xr3 = _dot(hp, win_ref[:, D_RNN:2 * D_RNN]).reshape(SEG_LEN, SUBLANES, D_RNN)
        gate = _dot(hp, win_ref[:, 0:D_RNN])
        tail = xtail_ref[u]
        xc3 = cb_ref[...] + cw[3:4] * xr3
        for k in range(1, 4):
            wrap = [jnp.where(sub == 0, tail[SUBLANES - k + v:SUBLANES - k + v + 1, :],
                              pltpu.roll(xr3[SEG_LEN - k + v], 1, 0)) for v in range(k)]
            prev = jnp.concatenate([jnp.stack(wrap), xr3[0:SEG_LEN - k]], axis=0)
            xc3 = xc3 + cw[3 - k:4 - k] * prev
        last = jnp.zeros((SUBLANES, D_RNN), F32)
        for r in range(SUBLANES):
            last = jnp.where(sub == r, pltpu.roll(xr3[SEG_LEN - SUBLANES + r], r + 1, 0), last)
        xtail_ref[u] = last
        cst_ref[u] = last
        a, b = _rg_gates(xc3.reshape(TM, D_RNN), wa_ref, ba_ref[...], wx_ref, bx_ref[...],
                         lam_ref[...])
        a_ref[u] = a.reshape(SEG_LEN, SUBLANES, D_RNN)
        b_ref[u] = b.reshape(SEG_LEN, SUBLANES, D_RNN)
        return gate

    def scan(u):
        def step(v, carry):
            hl, ac = carry
            av = a_ref[u, v]
            hl = av * hl + b_ref[u, v]
            ac = av * ac
            hs_ref[u, v] = hl
            ac_ref[u, v] = ac
            return hl, ac

        zero = jnp.zeros((SUBLANES, D_RNN), F32)
        hl, ac = lax.fori_loop(0, SEG_LEN, step, (zero, zero + 1.0), unroll=True)
        h_in = jnp.where(sub == 0, hcar_ref[u], 0.0)
        for s in range(1, SUBLANES):
            h_in = jnp.where(sub == s, pltpu.roll(hl + ac * h_in, 1, 0), h_in)
        h_end = (hl + ac * h_in)[SUBLANES - 1:SUBLANES, :]
        hcar_ref[u] = h_end
        hst_ref[u] = h_end
        return hs_ref[u] + ac_ref[u] * h_in

    def back(u, gate, hs):
        y = (jax.nn.gelu(gate) * hs.reshape(TM, D_RNN)).astype(BF16)
        y = _dot(unperm_ref[...], y).astype(BF16)
        o_ref[u] = x_ref[u] + mod_ref[u][2:3] * _dot(y, wout_ref[...])

    gates = [front(u) for u in range(RG_PAIR)]
    for u in range(RG_PAIR):
        back(u, gates[u], scan(u))


def _rglru(x2d, mod, w_in, conv_w, conv_b, w_a, b_a, w_x, b_x, lam, w_out):
    n_tok = x2d.shape[0]
    x3 = x2d.reshape(BATCH, SEQ, D_MODEL)
    rho = jnp.arange(TM)
    perm = (jnp.arange(TM)[None, :] == ((rho % SUBLANES) * SEG_LEN + rho // SUBLANES)[:, None])
    perm = perm.astype(BF16)
    row_spec = pl.BlockSpec((RG_PAIR, TM, D_MODEL),
                            lambda i: (i // TILES_PER_SEQ, i % TILES_PER_SEQ, 0))

    def seq_spec(rows):
        return pl.BlockSpec((RG_PAIR, rows, D_RNN), lambda i: (i // TILES_PER_SEQ, 0, 0))

    out, cst, hst = pl.pallas_call(
        _rglru_kernel,
        grid=(n_tok // (RG_PAIR * TM),),
        in_specs=[
            row_spec,
            seq_spec(6),
            _const_spec((TM, TM)),
            _const_spec((TM, TM)),
            _const_spec((D_MODEL, 2 * D_RNN)),
            _const_spec((4, D_RNN)),
            _const_spec((1, D_RNN)),
            _const_spec((RG_HEADS, RG_BLOCK, RG_BLOCK)),
            _const_spec((1, D_RNN)),
            _const_spec((RG_HEADS, RG_BLOCK, RG_BLOCK)),
            _const_spec((1, D_RNN)),
            _const_spec((1, D_RNN)),
            _const_spec((D_RNN, D_MODEL)),
        ],
        out_specs=[row_spec, seq_spec(SUBLANES), seq_spec(1)],
        out_shape=[
            jax.ShapeDtypeStruct((BATCH, SEQ, D_MODEL), F32),
            jax.ShapeDtypeStruct((BATCH, SUBLANES, D_RNN), F32),
            jax.ShapeDtypeStruct((BATCH, 1, D_RNN), F32),
        ],
        scratch_shapes=[
            pltpu.VMEM((RG_PAIR, SUBLANES, D_RNN), F32),
            pltpu.VMEM((RG_PAIR, 1, D_RNN), F32),
        ] + [pltpu.VMEM((RG_PAIR, SEG_LEN, SUBLANES, D_RNN), F32)] * 4,
        compiler_params=_cparams("arbitrary"),
        name="rglru",
    )(x3, mod, perm, perm.T, w_in, conv_w, conv_b, w_a, b_a, w_x, b_x, lam, w_out)
    return out.reshape(n_tok, D_MODEL), cst, hst


def _s_even_in_kernel(x_ref, mod_ref, w_ref, lng_ref, lnb_ref, ws0_ref, bs0_ref,
                      va_ref, a_ref, q_ref, k_ref, v_ref):
    mod = mod_ref[...]
    h = _rms_mod(x_ref[...], mod[:, D_MODEL:2 * D_MODEL], mod[:, 0:D_MODEL]).astype(BF16)
    u = jax.nn.gelu(_dot(h, w_ref[:, 0:W_A]))
    va = _layernorm(jax.nn.gelu(_dot(h, w_ref[:, W_A:2 * W_A])), lng_ref[...], lnb_ref[...])
    va_ref[...] = va
    a_ref[...] = (u * (ws0_ref[...] * va + bs0_ref[...])).astype(BF16)
    base = 2 * W_A
    q_ref[...] = _dot(h, w_ref[:, base:base + W_B])
    k_ref[...] = _dot(h, w_ref[:, base + W_B:base + 2 * W_B])
    v_ref[...] = _dot(h, w_ref[:, base + 2 * W_B:base + 3 * W_B])


def _s_even_in(x, mod, w_in, ln_g, ln_b, ws0, bs0):
    n = x.shape[0]
    f = jax.ShapeDtypeStruct((n, W_B), F32)
    return pl.pallas_call(
        _s_even_in_kernel,
        out_shape=[f, jax.ShapeDtypeStruct((n, W_A), BF16), f, f, f],
        compiler_params=pltpu.CompilerParams(vmem_limit_bytes=VMEM_LIMIT),
        name="s_even_in",
    )(x, mod, w_in, ln_g, ln_b, ws0, bs0)


def _s_attn_heads(qrow, knrow, vnrow, kt3, vt3):
    nh = kt3.shape[0]
    n_hd = nh * HD
    qrow = qrow * (HD ** -0.5)
    hrow = lax.broadcasted_iota(jnp.int32, (SUBLANES, n_hd), 0)
    hcol = lax.broadcasted_iota(jnp.int32, (SUBLANES, n_hd), 1) // HD
    own = hrow == hcol
    qbd = jnp.where(own, jnp.broadcast_to(qrow, (SUBLANES, n_hd)), 0.0)
    kt = kt3.reshape(n_hd, WIN)
    vt = vt3.reshape(n_hd, WIN)
    s = _dot(qbd.astype(BF16), kt.astype(BF16))
    s_new = jnp.sum(qbd * knrow, axis=-1, keepdims=True)
    t = lax.broadcasted_iota(jnp.int32, (SUBLANES, WIN), 1)
    dist = WIN - t
    vtb = vt.astype(BF16)
    outs, lses = [], []
    for d in DILATIONS:
        valid = ((dist & (d - 1)) == 0) & (dist <= N_BACK * d)
        sm = jnp.where(valid, s, NEG_INF)
        mx = jnp.maximum(jnp.max(sm, axis=-1, keepdims=True), s_new)
        e = jnp.where(valid, jnp.exp(sm - mx), 0.0)
        e_new = jnp.exp(s_new - mx)
        den = jnp.sum(e, axis=-1, keepdims=True) + e_new
        o = lax.dot_general(e.astype(BF16), vtb, (((1,), (1,)), ((), ())),
                            preferred_element_type=F32)
        outs.append((o + e_new * vnrow) / den)
        lses.append(mx + jnp.log(den))
    mx = jnp.maximum(jnp.maximum(lses[0], lses[1]), lses[2])
    ws = [jnp.exp(l - mx) for l in lses]
    num = ws[0] * outs[0] + ws[1] * outs[1] + ws[2] * outs[2]
    mixed = num / (ws[0] + ws[1] + ws[2])
    o_row = jnp.sum(jnp.where(own, mixed, 0.0), axis=0, keepdims=True)

    r2 = lax.broadcasted_iota(jnp.int32, (n_hd, n_hd), 0)
    c2 = lax.broadcasted_iota(jnp.int32, (n_hd, n_hd), 1)
    diag = r2 == c2
    kcol = jnp.sum(jnp.where(diag, jnp.broadcast_to(knrow, (n_hd, n_hd)), 0.0), axis=-1, keepdims=True)
    vcol = jnp.sum(jnp.where(diag, jnp.broadcast_to(vnrow, (n_hd, n_hd)), 0.0), axis=-1, keepdims=True)
    tt = lax.broadcasted_iota(jnp.int32, (n_hd, WIN), 1)
    is_last = tt == WIN - 1
    kt_new = jnp.where(is_last, kcol, pltpu.roll(kt, WIN - 1, 1)).reshape(nh, HD, WIN)
    vt_new = jnp.where(is_last, vcol, pltpu.roll(vt, WIN - 1, 1)).reshape(nh, HD, WIN)
    return o_row, kt_new, vt_new


def _s_even_out_kernel(x_ref, mod_ref, a_ref, b_ref, w_ref, o_ref):
    g1 = mod_ref[:, 2 * D_MODEL:3 * D_MODEL]
    mo = _dot(a_ref[...], w_ref[0:W_A, :]) + _dot(b_ref[...].astype(BF16), w_ref[W_A:W_A + W_B, :])
    o_ref[...] = x_ref[...] + g1 * mo


def _s_even_out(x, mod, a_out, b_out, w_out):
    return pl.pallas_call(
        _s_even_out_kernel,
        out_shape=jax.ShapeDtypeStruct(x.shape, F32),
        compiler_params=pltpu.CompilerParams(vmem_limit_bytes=VMEM_LIMIT),
        name="s_even_out",
    )(x, mod, a_out, b_out, w_out)


def _s_ffn_kernel(x_ref, mod_ref, p2_ref, p1_ref, wup_ref, cw_ref, cb_ref, wdn_ref, fg_ref,
                  o_ref, up_ref, *, final):
    x = x_ref[...]
    mod = mod_ref[...]
    h = _rms_mod(x, mod[:, 4 * D_MODEL:5 * D_MODEL], mod[:, 3 * D_MODEL:4 * D_MODEL]).astype(BF16)
    up = _dot(h, wup_ref[...])
    up_ref[...] = up
    cw = cw_ref[...]
    conv = cb_ref[...] + cw[0:1] * p2_ref[...] + cw[1:2] * p1_ref[...] + cw[2:3] * up
    ca, cg = conv[:, 0:D_FF], conv[:, D_FF:2 * D_FF]
    act = (ca * jax.nn.sigmoid(ca) * cg).astype(BF16)
    y = x + mod[:, 5 * D_MODEL:6 * D_MODEL] * _dot(act, wdn_ref[...])
    if final:
        y = _rmsnorm(y) * fg_ref[...]
    o_ref[...] = y


def _s_ffn(x, mod, p2, p1, layer, w_up, conv_w, conv_b, w_down, final_g, final):
    n = x.shape[0]
    f2 = 2 * D_FF
    return pl.pallas_call(
        functools.partial(_s_ffn_kernel, final=final),
        grid=(1,),
        in_specs=[_const_spec(x.shape), _const_spec(mod.shape), _const_spec(p2.shape),
                  _const_spec(p1.shape), _layer_spec((D_MODEL, f2), layer),
                  _layer_spec((3, f2), layer), _layer_spec((1, f2), layer),
                  _layer_spec((D_FF, D_MODEL), layer), _const_spec((1, D_MODEL))],
        out_specs=[pl.BlockSpec(x.shape, lambda i: (0, 0)), pl.BlockSpec((n, f2), lambda i: (0, 0))],
        out_shape=[jax.ShapeDtypeStruct(x.shape, F32), jax.ShapeDtypeStruct((n, f2), F32)],
        compiler_params=_cparams("arbitrary"),
        name="s_conv_ffn_final" if final else "s_conv_ffn",
    )(x, mod, p2, p1, w_up, conv_w, conv_b, w_down, final_g)


def _s_rglru_kernel(x_ref, mod_ref, c0_ref, c1_ref, c2_ref, h0_ref, win_ref, cw_ref, cb_ref,
                    wa_ref, ba_ref, wx_ref, bx_ref, lam_ref, wout_ref, o_ref, xr_ref, hn_ref):
    x = x_ref[...]
    mod = mod_ref[...]
    h = _rms_mod(x, mod[:, D_MODEL:2 * D_MODEL], mod[:, 0:D_MODEL]).astype(BF16)
    gate = _dot(h, win_ref[:, 0:D_RNN])
    xr = _dot(h, win_ref[:, D_RNN:2 * D_RNN])
    xr_ref[...] = xr
    cw = cw_ref[...]
    xc = (cb_ref[...] + cw[0:1] * c0_ref[...] + cw[1:2] * c1_ref[...] + cw[2:3] * c2_ref[...]
          + cw[3:4] * xr)
    a, b = _rg_gates(xc, wa_ref, ba_ref[...], wx_ref, bx_ref[...], lam_ref[...])
    hn = a * h0_ref[...] + b
    hn_ref[...] = hn
    y = (jax.nn.gelu(gate) * hn).astype(BF16)
    o_ref[...] = x + mod[:, 2 * D_MODEL:3 * D_MODEL] * _dot(y, wout_ref[...])


def _s_rglru(x, mod, c0, c1, c2, h0, w_in, conv_w, conv_b, w_a, b_a, w_x, b_x, lam, w_out):
    f = jax.ShapeDtypeStruct(x.shape, F32)
    return pl.pallas_call(
        _s_rglru_kernel,
        out_shape=[f, f, f],
        compiler_params=pltpu.CompilerParams(vmem_limit_bytes=VMEM_LIMIT),
        name="s_rglru",
    )(x, mod, c0, c1, c2, h0, w_in, conv_w, conv_b, w_a, b_a, w_x, b_x, lam, w_out)


def kernel(x_prompt, x_sample, cache_win_k, cache_win_v, state_rglru_conv, state_rglru_h, state_ffn_conv, c_prompt, c_sample, w_ada, b_ada, w_in_even, ln_v_g, ln_v_b, w_sgu, b_sgu, w_out_even, w_in_odd, rg_conv_w, rg_conv_b, rg_w_a, rg_b_a, rg_w_x, rg_b_x, rg_lambda, w_out_odd, ffn_w_up, ffn_conv_w, ffn_conv_b, ffn_w_down, final_g):
    w_in_even_b = w_in_even[0].astype(BF16)
    w_out_even_b = w_out_even[0].astype(BF16)
    w_in_odd_b = w_in_odd[0].astype(BF16)
    w_out_odd_b = w_out_odd[0].astype(BF16)
    rg_w_a_b = rg_w_a[0].astype(BF16)
    rg_w_x_b = rg_w_x[0].astype(BF16)
    conv_b3 = ffn_conv_b.reshape(2, 1, 2 * D_FF)
    final_g2 = final_g.reshape(1, D_MODEL)

    pad = jnp.zeros((ADA_ROWS - BATCH - DEC_BATCH, D_MODEL), F32)
    mod = _ada(jnp.concatenate([c_prompt, c_sample, pad], axis=0), w_ada, b_ada)
    mod_p = mod[:, :BATCH].reshape(2, BATCH, 6, D_MODEL)
    mod_s = mod[:, BATCH:BATCH + DEC_BATCH]

    xs = x_sample.reshape(DEC_BATCH, D_MODEL)
    ws0 = jnp.repeat(w_sgu[0, :, 0, 0], A_GROUP).reshape(1, W_A)
    bs0 = jnp.repeat(b_sgu[0, :, 0], A_GROUP).reshape(1, W_A)
    va_s, a_s, q_s, k_s, v_s = _s_even_in(xs, mod_s[0], w_in_even_b, ln_v_g, ln_v_b, ws0, bs0)
    kt_c = cache_win_k[0].transpose(0, 2, 3, 1)
    vt_c = cache_win_v[0].transpose(0, 2, 3, 1)

    x = x_prompt.reshape(BATCH * SEQ, D_MODEL)
    a_out, q, k, v, kt_p, vt_p = _even_in(x, mod_p[0], w_in_even_b, ln_v_g, ln_v_b, w_sgu[0],
                                          b_sgu[0].T)
    b_out, w_up_b, w_down_b = _attn(q, k, v, ffn_w_up, ffn_w_down)
    x, ffn_st0 = _ffn(x, mod_p[0], 0, w_up_b, ffn_conv_w, conv_b3, w_down_b, final_g2, False,
                      pre=(a_out, b_out, w_out_even_b))
    x, rg_cst, rg_hst = _rglru(x, mod_p[1], w_in_odd_b, rg_conv_w[0], rg_conv_b, rg_w_a_b, rg_b_a,
                               rg_w_x_b, rg_b_x, rg_lambda, w_out_odd_b)
    y_p, ffn_st1, b_s, kt_n, vt_n = _ffn(x, mod_p[1], 1, w_up_b, ffn_conv_w, conv_b3, w_down_b,
                                         final_g2, True, att=(q_s, k_s, v_s, kt_c, vt_c))

    y_prompt = y_p.reshape(BATCH, SEQ, D_MODEL)
    to_win = lambda t: t.reshape(1, BATCH, N_HEADS, HD, WIN).transpose(0, 1, 4, 2, 3)
    win_k_prompt = to_win(kt_p)
    win_v_prompt = to_win(vt_p)
    rglru_conv_prompt = rg_cst[None, :, SUBLANES - 3:, :]
    rglru_h_prompt = rg_hst.reshape(1, BATCH, D_RNN)
    ffn_conv_prompt = jnp.stack([ffn_st0[:, SUBLANES - 2:, :], ffn_st1[:, SUBLANES - 2:, :]])

    xs = _s_even_out(xs, mod_s[0], a_s, b_s, w_out_even_b)
    st0 = state_ffn_conv[0]
    xs, up0 = _s_ffn(xs, mod_s[0], st0[:, 0], st0[:, 1], 0, w_up_b, ffn_conv_w, conv_b3, w_down_b,
                     final_g2, False)
    cst = state_rglru_conv[0]
    xs, xr_s, hn_s = _s_rglru(xs, mod_s[1], cst[:, 0], cst[:, 1], cst[:, 2], state_rglru_h[0],
                              w_in_odd_b, rg_conv_w[0], rg_conv_b, rg_w_a_b, rg_b_a, rg_w_x_b,
                              rg_b_x, rg_lambda, w_out_odd_b)
    st1 = state_ffn_conv[1]
    ys, up1 = _s_ffn(xs, mod_s[1], st1[:, 0], st1[:, 1], 1, w_up_b, ffn_conv_w, conv_b3, w_down_b,
                     final_g2, True)

    y_sample = ys.reshape(DEC_BATCH, 1, D_MODEL)
    chunk_v_sample = va_s.reshape(1, DEC_BATCH, 1, W_A)
    win_k_sample = kt_n.transpose(0, 3, 1, 2)[None]
    win_v_sample = vt_n.transpose(0, 3, 1, 2)[None]
    rglru_conv_sample = jnp.stack([cst[:, 1], cst[:, 2], xr_s], axis=1)[None]
    rglru_h_sample = hn_s[None]
    ffn_conv_sample = jnp.stack([jnp.stack([st0[:, 1], up0], axis=1),
                                 jnp.stack([st1[:, 1], up1], axis=1)])

    return (y_prompt, y_sample, win_k_prompt, win_v_prompt, rglru_conv_prompt, rglru_h_prompt,
            ffn_conv_prompt, chunk_v_sample, win_k_sample, win_v_sample, rglru_conv_sample,
            rglru_h_sample, ffn_conv_sample)
```

```python
import functools

import jax
import jax.numpy as jnp
from jax import lax
from jax.experimental import pallas as pl
from jax.experimental.pallas import tpu as pltpu

F32 = jnp.float32
BF16 = jnp.bfloat16

D_MODEL = 1024
BATCH = 4
SEQ = 4096
DEC_BATCH = 32
W_A = 512
A_GROUP = 128
G_A = 4
CHUNK = 128
W_B = 512
HD = 64
N_HEADS = 8
DILATIONS = (1, 4, 16)
N_BACK = 128
WIN = 2048
N_IN_EVEN = 2 * W_A + 3 * W_B
D_RNN = 1024
RG_BLOCK = 128
RG_HEADS = 8
RG_C = 8.0
D_FF = 2816
EPS = 1e-6
NEG_INF = -1e30
LOG2_E = 1.4426950408889634

LANES = 128
SUBLANES = 8
TM = 512
TILES_PER_SEQ = SEQ // TM
N_TILES = BATCH * TILES_PER_SEQ
FF_CHUNK = 256
N_FF_CHUNKS = D_FF // FF_CHUNK
N_SLABS = W_B // LANES
SEQ_PAIR = 2
VMEM_LIMIT = 56 * 1024 * 1024


def _cparams(*sem):
    return pltpu.CompilerParams(dimension_semantics=sem, vmem_limit_bytes=VMEM_LIMIT)


def _const_spec(shape):
    nd = len(shape)
    return pl.BlockSpec(shape, lambda *_: (0,) * nd, pipeline_mode=pl.Buffered(1))


def _layer_spec(shape, layer):
    nd = len(shape)
    return pl.BlockSpec((None,) + tuple(shape), lambda *_: (layer,) + (0,) * nd,
                        pipeline_mode=pl.Buffered(1))


def _shift_rows(cur, tail, k):
    rolled = pltpu.roll(cur, k, 0)
    r = lax.broadcasted_iota(jnp.int32, (SUBLANES, cur.shape[1]), 0)
    head = jnp.where(r < k, pltpu.roll(tail, k, 0), rolled[0:SUBLANES])
    return jnp.concatenate([head, rolled[SUBLANES:]], axis=0)


def _rms_mod(x, scale, shift):
    xn = x * lax.rsqrt(jnp.mean(x * x, axis=-1, keepdims=True) + EPS)
    return xn * (1.0 + scale) + shift


def _rmsnorm(x):
    return x * lax.rsqrt(jnp.mean(x * x, axis=-1, keepdims=True) + EPS)


def _layernorm(x, g, b):
    mu = jnp.mean(x, axis=-1, keepdims=True)
    xc = x - mu
    var = jnp.mean(xc * xc, axis=-1, keepdims=True)
    return xc * lax.rsqrt(var + EPS) * g + b


def _dot(a, b):
    return jnp.dot(a, b, preferred_element_type=F32)


ADA_ROWS = 40
ADA_TN = 1024


def _ada_kernel(c_ref, w_ref, b_ref, o_ref):
    c = c_ref[...]
    s = (c * jax.nn.sigmoid(c)).astype(BF16)
    o_ref[0] = _dot(s, w_ref[0].astype(BF16)) + b_ref[0]


def _ada(c_all, w_ada, b_ada):
    depth = w_ada.shape[0]
    n_out = w_ada.shape[2]
    return pl.pallas_call(
        _ada_kernel,
        grid=(depth, n_out // ADA_TN),
        in_specs=[
            pl.BlockSpec((ADA_ROWS, D_MODEL), lambda l, j: (0, 0)),
            pl.BlockSpec((1, D_MODEL, ADA_TN), lambda l, j: (l, 0, j)),
            pl.BlockSpec((1, 1, ADA_TN), lambda l, j: (l, 0, j)),
        ],
        out_specs=pl.BlockSpec((1, ADA_ROWS, ADA_TN), lambda l, j: (l, 0, j)),
        out_shape=jax.ShapeDtypeStruct((depth, ADA_ROWS, n_out), F32),
        compiler_params=_cparams("arbitrary", "arbitrary"),
        name="ada_mod",
    )(c_all, w_ada, b_ada.reshape(depth, 1, n_out))


def _even_in_kernel(x_ref, mod_ref, w_ref, lng_ref, lnb_ref, ws_ref, bst_ref,
                    a_ref, q_ref, k_ref, v_ref, kt_ref, vt_ref):
    base = 2 * W_A
    row = lax.broadcasted_iota(jnp.int32, (CHUNK, CHUNK), 0)
    col = lax.broadcasted_iota(jnp.int32, (CHUNK, CHUNK), 1)
    causal = col <= row
    bst = bst_ref[...]

    def front(u):
        m = mod_ref[u]
        h = _rms_mod(x_ref[u], m[1:2], m[0:1]).astype(BF16)
        va = _layernorm(jax.nn.gelu(_dot(h, w_ref[:, W_A:2 * W_A])), lng_ref[...], lnb_ref[...])
        ua = jax.nn.gelu(_dot(h, w_ref[:, 0:W_A]))
        q = _dot(h, w_ref[:, base:base + W_B])
        k = _dot(h, w_ref[:, base + W_B:base + 2 * W_B])
        for s in range(N_SLABS):
            q_ref[s, u] = q[:, s * LANES:(s + 1) * LANES]
            k_ref[s, u] = k[:, s * LANES:(s + 1) * LANES]
        return h, va.astype(BF16), ua, k

    def back(u, h, vab, ua):
        for g in range(G_A):
            wg = jnp.where(causal, ws_ref[g], 0.0).astype(BF16)
            bias = bst[:, g:g + 1]
            lo, hi = g * A_GROUP, (g + 1) * A_GROUP
            for c in range(TM // CHUNK):
                r0, r1 = c * CHUNK, (c + 1) * CHUNK
                mix = _dot(wg, vab[r0:r1, lo:hi]) + bias
                a_ref[u, r0:r1, lo:hi] = (ua[r0:r1, lo:hi] * mix).astype(BF16)
        v = _dot(h, w_ref[:, base + 2 * W_B:base + 3 * W_B])
        for s in range(N_SLABS):
            v_ref[s, u] = v[:, s * LANES:(s + 1) * LANES]
        return v

    fronts = [front(u) for u in range(SEQ_PAIR)]
    vs = [back(u, fronts[u][0], fronts[u][1], fronts[u][2]) for u in range(SEQ_PAIR)]
    for u in range(SEQ_PAIR):
        kt_ref[u] = fronts[u][3].T
        vt_ref[u] = vs[u].T


def _even_in(x2d, mod, w_in, ln_g, ln_b, w_s, b_s_t):
    n_tok = x2d.shape[0]
    first_win_tile = TILES_PER_SEQ - WIN // TM
    pair_tile = lambda i: (i // TILES_PER_SEQ, i % TILES_PER_SEQ, 0)
    qkv_shape = jax.ShapeDtypeStruct((N_SLABS, BATCH, SEQ, LANES), F32)
    qkv_spec = pl.BlockSpec((N_SLABS, SEQ_PAIR, TM, LANES),
                            lambda i: (0, i // TILES_PER_SEQ, i % TILES_PER_SEQ, 0))
    win_shape = jax.ShapeDtypeStruct((BATCH, W_B, WIN), F32)
    win_spec = pl.BlockSpec(
        (SEQ_PAIR, W_B, TM),
        lambda i: (i // TILES_PER_SEQ, 0, jnp.maximum(i % TILES_PER_SEQ - first_win_tile, 0)))
    a_out, q, k, v, kt, vt = pl.pallas_call(
        _even_in_kernel,
        grid=(N_TILES // SEQ_PAIR,),
        in_specs=[
            pl.BlockSpec((SEQ_PAIR, TM, D_MODEL), pair_tile),
            pl.BlockSpec((SEQ_PAIR, 6, D_MODEL), lambda i: (i // TILES_PER_SEQ, 0, 0)),
            _const_spec((D_MODEL, N_IN_EVEN)),
            _const_spec((1, W_A)),
            _const_spec((1, W_A)),
            _const_spec((G_A, CHUNK, CHUNK)),
            _const_spec((CHUNK, G_A)),
        ],
        out_specs=[
            pl.BlockSpec((SEQ_PAIR, TM, W_A), pair_tile),
            qkv_spec, qkv_spec, qkv_spec, win_spec, win_spec,
        ],
        out_shape=[
            jax.ShapeDtypeStruct((BATCH, SEQ, W_A), BF16),
            qkv_shape, qkv_shape, qkv_shape, win_shape, win_shape,
        ],
        compiler_params=_cparams("arbitrary"),
        name="even_in",
    )(x2d.reshape(BATCH, SEQ, D_MODEL), mod, w_in, ln_g, ln_b, w_s, b_s_t)
    flat = lambda a: a.reshape(N_SLABS, n_tok, LANES)
    return a_out.reshape(n_tok, W_A), flat(q), flat(k), flat(v), kt, vt


ATT_BLK = 128
ATT_UNROLL = 32


def _attn_kernel(q_ref, k_ref, v_ref, wu_hbm, wd_hbm, o_ref, wub_hbm, wdb_hbm,
                 acc_ref, max_ref, den_ref, q4_ref, k4_ref, v4_ref,
                 wu_in, wd_in, wu_out, wd_out, in_sem, out_sem):
    step = pl.program_id(0) * pl.num_programs(1) + pl.program_id(1)
    n_steps = pl.num_programs(0) * pl.num_programs(1)

    def cast_in(c):
        return (pltpu.make_async_copy(wu_hbm.at[pl.ds(c * wu_in.shape[0], wu_in.shape[0])], wu_in,
                                      in_sem.at[0]),
                pltpu.make_async_copy(wd_hbm.at[pl.ds(c * wd_in.shape[0], wd_in.shape[0])], wd_in,
                                      in_sem.at[1]))

    def cast_out(c):
        return (pltpu.make_async_copy(wu_out, wub_hbm.at[pl.ds(c * wu_in.shape[0], wu_in.shape[0])],
                                      out_sem.at[0]),
                pltpu.make_async_copy(wd_out, wdb_hbm.at[pl.ds(c * wd_in.shape[0], wd_in.shape[0])],
                                      out_sem.at[1]))

    for cp in cast_in(step):
        cp.start()

    q2, k2, v2 = q_ref.at[0], k_ref.at[0], v_ref.at[0]
    seq4 = SEQ // 4
    for src, dst in ((q2, q4_ref), (k2, k4_ref), (v2, v4_ref)):
        for c0 in range(4):
            for r in range(0, seq4, 2 * ATT_BLK):
                dst[c0 * seq4 + r:c0 * seq4 + r + 2 * ATT_BLK, :] = (
                    src[pl.ds(c0 + 4 * r, 2 * ATT_BLK, stride=4), :])
    qi = lax.broadcasted_iota(jnp.int32, (ATT_BLK, 2 * ATT_BLK), 0)
    kj = lax.broadcasted_iota(jnp.int32, (ATT_BLK, 2 * ATT_BLK), 1)
    dist = qi + ATT_BLK - kj
    band = (dist >= 0) & (dist <= N_BACK)
    bias_full = jnp.where(band, 0.0, NEG_INF).astype(F32)
    bias_first = jnp.where(band & (kj >= ATT_BLK), 0.0, NEG_INF).astype(F32)
    lane = lax.broadcasted_iota(jnp.int32, (ATT_BLK, LANES), 1)
    head0 = lane < HD

    for p, d in enumerate(DILATIONS):
        n_blk = SEQ // (d * ATT_BLK)

        def unit(idx, carry, p=p, d=d, n_blk=n_blk):
            k_prev, v_prev = carry
            c = idx // n_blk
            b = idx % n_blk
            if d == 1:
                qs, ks, vs = q2, k2, v2
                rows = pl.ds(pl.multiple_of(ATT_BLK * b, ATT_BLK), ATT_BLK)
            elif d == 4:
                qs, ks, vs = q4_ref, k4_ref, v4_ref
                rows = pl.ds(pl.multiple_of(c * seq4 + ATT_BLK * b, ATT_BLK), ATT_BLK)
            else:
                qs, ks, vs = q4_ref, k4_ref, v4_ref
                rows = pl.ds((c % 4) * seq4 + c // 4 + 4 * ATT_BLK * b, ATT_BLK, stride=4)
            qb = qs[rows, :] * (HD ** -0.5 * LOG2_E)
            k_own = ks[rows, :].astype(BF16)
            v_own = vs[rows, :].astype(BF16)
            kb = jnp.concatenate([k_prev, k_own], axis=0)
            vb = jnp.concatenate([v_prev, v_own], axis=0)
            bias = jnp.where(b == 0, bias_first, bias_full)
            outs, mxs, dens = [], [], []
            for hh in range(2):
                sel = head0 if hh == 0 else jnp.logical_not(head0)
                qm = jnp.where(sel, qb, 0.0).astype(BF16)
                s = lax.dot_general(qm, kb, (((1,), (1,)), ((), ())),
                                    preferred_element_type=F32) + bias
                mx = jnp.max(s, axis=-1, keepdims=True)
                e = jnp.exp2(s - mx)
                dens.append(jnp.sum(e, axis=-1, keepdims=True))
                mxs.append(mx)
                outs.append(_dot(e.astype(BF16), vb))
            acc_ref[p, rows, :] = jnp.where(head0, outs[0], outs[1])
            max_ref[p, rows, :] = jnp.where(head0, mxs[0], mxs[1])
            den_ref[p, rows, :] = jnp.where(head0, dens[0], dens[1])
            return k_own, v_own

        zeros = jnp.zeros((ATT_BLK, LANES), BF16)
        lax.fori_loop(0, SEQ // ATT_BLK, unit, (zeros, zeros), unroll=ATT_UNROLL)

    def mix(t, carry):
        c0 = t // (seq4 // TM)
        r = (t % (seq4 // TM)) * TM
        rows4 = pl.ds(pl.multiple_of(c0 * seq4 + r, TM), TM)
        rows = pl.ds(c0 + 4 * r, TM, stride=4)
        m0, m1, m2 = max_ref[0, rows, :], max_ref[1, rows4, :], max_ref[2, rows4, :]
        mx = jnp.maximum(jnp.maximum(m0, m1), m2)
        e0, e1, e2 = jnp.exp2(m0 - mx), jnp.exp2(m1 - mx), jnp.exp2(m2 - mx)
        num = e0 * acc_ref[0, rows, :] + e1 * acc_ref[1, rows4, :] + e2 * acc_ref[2, rows4, :]
        den = e0 * den_ref[0, rows, :] + e1 * den_ref[1, rows4, :] + e2 * den_ref[2, rows4, :]
        acc_ref[0, rows, :] = num / den
        return carry

    lax.fori_loop(0, SEQ // TM, mix, 0, unroll=True)

    def emit(t, carry):
        rows = pl.ds(pl.multiple_of(t * TM, TM), TM)
        o_ref[rows, :] = acc_ref[0, rows, :].astype(BF16)
        return carry

    lax.fori_loop(0, SEQ // TM, emit, 0)

    for cp in cast_in(step):
        cp.wait()

    @pl.when(step > 0)
    def _():
        for cp in cast_out(step - 1):
            cp.wait()

    wu_out[...] = wu_in[...].astype(BF16)
    wd_out[...] = wd_in[...].astype(BF16)
    for cp in cast_out(step):
        cp.start()

    @pl.when(step == n_steps - 1)
    def _():
        for cp in cast_out(step):
            cp.wait()


def _attn(q, k, v, w_up, w_down):
    n_tok = q.shape[1]
    n_steps = BATCH * N_SLABS
    wu2 = w_up.reshape(-1, w_up.shape[-1])
    wd2 = w_down.reshape(-1, w_down.shape[-1])
    ru, rd = wu2.shape[0] // n_steps, wd2.shape[0] // n_steps
    assert ru * n_steps == wu2.shape[0] and rd * n_steps == wd2.shape[0] and ru % 16 == 0 and rd % 16 == 0
    spec = pl.BlockSpec((1, SEQ, LANES), lambda n, s: (s, n, 0))
    any_spec = pl.BlockSpec(memory_space=pl.ANY)
    b_out, wub, wdb = pl.pallas_call(
        _attn_kernel,
        grid=(BATCH, N_SLABS),
        in_specs=[spec, spec, spec, any_spec, any_spec],
        out_specs=[pl.BlockSpec((SEQ, LANES), lambda n, s: (n, s)), any_spec, any_spec],
        out_shape=[jax.ShapeDtypeStruct((n_tok, W_B), BF16),
                   jax.ShapeDtypeStruct(wu2.shape, BF16), jax.ShapeDtypeStruct(wd2.shape, BF16)],
        scratch_shapes=([pltpu.VMEM((len(DILATIONS), SEQ, LANES), F32)] * 3
                        + [pltpu.VMEM((SEQ, LANES), F32)] * 3
                        + [pltpu.VMEM((ru, wu2.shape[1]), F32), pltpu.VMEM((rd, wd2.shape[1]), F32),
                           pltpu.VMEM((ru, wu2.shape[1]), BF16), pltpu.VMEM((rd, wd2.shape[1]), BF16),
                           pltpu.SemaphoreType.DMA((2,)), pltpu.SemaphoreType.DMA((2,))]),
        compiler_params=_cparams("arbitrary", "arbitrary"),
        name="dil_attn",
    )(q, k, v, wu2, wd2)
    return b_out, wub.reshape(w_up.shape), wdb.reshape(w_down.shape)


S_ATT_HEADS = 4
S_ATT_PHASES = N_HEADS // S_ATT_HEADS
FF_SPLIT = 6


def _ffn_kernel(*refs, pre, final, att):
    refs = list(refs)
    x_ref, mod_ref = refs[0:2]
    del refs[0:2]
    if pre:
        a_ref, b_ref, wo_ref = refs[0:3]
        del refs[0:3]
    wup_ref, cw_ref, cb_ref, wdn_ref, fg_ref = refs[0:5]
    del refs[0:5]
    if att:
        (qs_ref, kn_ref, vn_ref, kt_hbm, vt_hbm, o_ref, st_ref, bs_ref, kto_hbm, vto_hbm,
         carry_ref, act_ref, kin_ref, vin_ref, kout_ref, vout_ref, in_sem, out_sem) = refs
    else:
        o_ref, st_ref, carry_ref, act_ref = refs
    i = pl.program_id(0)
    last = pl.num_programs(0) - 1

    @pl.when(i % TILES_PER_SEQ == 0)
    def _():
        carry_ref[...] = jnp.zeros_like(carry_ref)

    def copies_in(seq, ph):
        hs = pl.ds(ph * S_ATT_HEADS, S_ATT_HEADS)
        return (pltpu.make_async_copy(kt_hbm.at[seq, hs], kin_ref.at[ph], in_sem.at[ph, 0]),
                pltpu.make_async_copy(vt_hbm.at[seq, hs], vin_ref.at[ph], in_sem.at[ph, 1]))

    def copies_out(seq, ph):
        hs = pl.ds(ph * S_ATT_HEADS, S_ATT_HEADS)
        return (pltpu.make_async_copy(kout_ref.at[ph], kto_hbm.at[seq, hs], out_sem.at[ph, 0]),
                pltpu.make_async_copy(vout_ref.at[ph], vto_hbm.at[seq, hs], out_sem.at[ph, 1]))

    def start(cps):
        for c in cps:
            c.start()

    def wait(cps):
        for c in cps:
            c.wait()

    def window_phase(ph):
        lanes = slice(ph * S_ATT_HEADS * HD, (ph + 1) * S_ATT_HEADS * HD)
        row = pl.ds(i, 1)
        o_row, kt_new, vt_new = _s_attn_heads(qs_ref[row, lanes], kn_ref[row, lanes],
                                              vn_ref[row, lanes], kin_ref[ph], vin_ref[ph])
        bs_ref[row, lanes] = o_row
        kout_ref[ph] = kt_new
        vout_ref[ph] = vt_new

    if att:
        @pl.when(i == 0)
        def _():
            kout_ref[1] = jnp.zeros(kout_ref.shape[1:], F32)
            vout_ref[1] = jnp.zeros(vout_ref.shape[1:], F32)
            start(copies_in(0, 0))
            start(copies_out(0, 1))

        wait(copies_in(i, 0))
        start(copies_in(i, 1))

        @pl.when(i > 0)
        def _():
            wait(copies_out(i - 1, 0))

        window_phase(0)

    m = mod_ref[0]
    cw = cw_ref[...]
    cb = cb_ref[...]
    x = x_ref[...]
    if pre:
        x = x + m[2:3] * (_dot(a_ref[...], wo_ref[0:W_A, :]) + _dot(b_ref[...], wo_ref[W_A:W_A + W_B, :]))
    h = _rms_mod(x, m[4:5], m[3:4]).astype(BF16)
    for j in range(N_FF_CHUNKS):
        if att and j == FF_SPLIT:
            start(copies_out(i, 0))
            wait(copies_in(i, 1))
            start(copies_in(jnp.minimum(i + 1, last), 0))
            wait(copies_out(jnp.maximum(i - 1, 0), 1))
            window_phase(1)
        conv = []
        for half in range(2):
            c0 = half * D_FF + j * FF_CHUNK
            cols = slice(c0, c0 + FF_CHUNK)
            up = _dot(h, wup_ref[:, cols])
            tail = carry_ref[:, cols]
            carry_ref[:, cols] = up[TM - SUBLANES:, :]
            conv.append(cb[:, cols] + cw[0:1, cols] * _shift_rows(up, tail, 2)
                        + cw[1:2, cols] * _shift_rows(up, tail, 1) + cw[2:3, cols] * up)
        act = conv[0] * jax.nn.sigmoid(conv[0]) * conv[1]
        act_ref[:, j * FF_CHUNK:(j + 1) * FF_CHUNK] = act.astype(BF16)
    st_ref[0] = carry_ref[...]
    y = x + m[5:6] * _dot(act_ref[...], wdn_ref[...])
    if final:
        y = _rmsnorm(y) * fg_ref[...]
    o_ref[...] = y

    if att:
        start(copies_out(i, 1))

        @pl.when(i == last)
        def _():
            wait(copies_out(i, 0))
            wait(copies_out(i, 1))
            wait(copies_in(i, 0))


def _ffn(x2d, mod, layer, w_up, conv_w, conv_b, w_down, final_g, final, pre=None, att=None):
    n_tok = x2d.shape[0]
    f2 = 2 * D_FF
    row_spec = pl.BlockSpec((TM, D_MODEL), lambda i: (i, 0))
    mod_spec = pl.BlockSpec((1, 6, D_MODEL), lambda i: (i // TILES_PER_SEQ, 0, 0))
    args, specs = [x2d, mod], [row_spec, mod_spec]
    if pre is not None:
        args += list(pre)
        specs += [pl.BlockSpec((TM, W_A), lambda i: (i, 0)),
                  pl.BlockSpec((TM, W_B), lambda i: (i, 0)),
                  _const_spec((W_A + W_B, D_MODEL))]
    args += [w_up, conv_w, conv_b, w_down, final_g]
    specs += [_layer_spec((D_MODEL, f2), layer), _layer_spec((3, f2), layer),
              _layer_spec((1, f2), layer), _layer_spec((D_FF, D_MODEL), layer),
              _const_spec((1, D_MODEL))]
    out_specs = [row_spec, pl.BlockSpec((1, SUBLANES, f2), lambda i: (i // TILES_PER_SEQ, 0, 0))]
    out_shape = [jax.ShapeDtypeStruct((n_tok, D_MODEL), F32),
                 jax.ShapeDtypeStruct((BATCH, SUBLANES, f2), F32)]
    scratch = [pltpu.VMEM((SUBLANES, f2), F32), pltpu.VMEM((TM, D_FF), BF16)]
    if att is not None:
        q_s, kt = att[0], att[3]
        assert q_s.shape[0] == n_tok // TM, "one sample sequence per grid step"
        any_spec = pl.BlockSpec(memory_space=pl.ANY)
        args += list(att)
        specs += [_const_spec(q_s.shape)] * 3 + [any_spec, any_spec]
        out_specs += [pl.BlockSpec(q_s.shape, lambda i: (0, 0)), any_spec, any_spec]
        out_shape += [jax.ShapeDtypeStruct(q_s.shape, F32),
                      jax.ShapeDtypeStruct(kt.shape, F32), jax.ShapeDtypeStruct(kt.shape, F32)]
        win_buf = pltpu.VMEM((S_ATT_PHASES, S_ATT_HEADS, HD, WIN), F32)
        scratch += [win_buf] * 4 + [pltpu.SemaphoreType.DMA((S_ATT_PHASES, 2))] * 2
    return pl.pallas_call(
        functools.partial(_ffn_kernel, pre=pre is not None, final=final, att=att is not None),
        grid=(n_tok // TM,),
        in_specs=specs,
        out_specs=out_specs,
        out_shape=out_shape,
        scratch_shapes=scratch,
        compiler_params=_cparams("arbitrary"),
        name="conv_ffn_final" if final else "conv_ffn",
    )(*args)


SEG_LEN = TM // SUBLANES
RG_PAIR = 2


def _rg_gates(xc, wa_ref, ba, wx_ref, bx, lam):
    xcb = xc.astype(BF16)
    r_parts, i_parts = [], []
    for g in range(RG_HEADS):
        blk = xcb[:, g * RG_BLOCK:(g + 1) * RG_BLOCK]
        r_parts.append(_dot(blk, wa_ref[g]))
        i_parts.append(_dot(blk, wx_ref[g]))
    r = jax.nn.sigmoid(jnp.concatenate(r_parts, axis=-1) + ba)
    ig = jax.nn.sigmoid(jnp.concatenate(i_parts, axis=-1) + bx)
    log_a = r * ((-RG_C) * jax.nn.softplus(-lam))
    a = jnp.exp(log_a)
    b = jnp.sqrt(1.0 - a * a) * (ig * xc)
    return a, b


def _rglru_kernel(x_ref, mod_ref, perm_ref, unperm_ref, win_ref, cw_ref, cb_ref, wa_ref, ba_ref,
                  wx_ref, bx_ref, lam_ref, wout_ref, o_ref, cst_ref, hst_ref,
                  xtail_ref, hcar_ref, a_ref, b_ref, hs_ref, ac_ref):
    t = pl.program_id(0) % TILES_PER_SEQ

    @pl.when(t == 0)
    def _():
        xtail_ref[...] = jnp.zeros_like(xtail_ref)
        hcar_ref[...] = jnp.zeros_like(hcar_ref)

    sub = lax.broadcasted_iota(jnp.int32, (SUBLANES, D_RNN), 0)
    cw = cw_ref[...]

    def front(u):
        m = mod_ref[u]
        h = _rms_mod(x_ref[u], m[1:2], m[0:1]).astype(BF16)
        hp = _dot(perm_ref[...], h).astype(BF16)
        xr3 = _dot(hp, win_ref[:, D_RNN:2 * D_RNN]).reshape(SEG_LEN, SUBLANES, D_RNN)
        gate = _dot(hp, win_ref[:, 0:D_RNN])
        tail = xtail_ref[u]
        xc3 = cb_ref[...] + cw[3:4] * xr3
        for k in range(1, 4):
            wrap = [jnp.where(sub == 0, tail[SUBLANES - k + v:SUBLANES - k + v + 1, :],
                              pltpu.roll(xr3[SEG_LEN - k + v], 1, 0)) for v in range(k)]
            prev = jnp.concatenate([jnp.stack(wrap), xr3[0:SEG_LEN - k]], axis=0)
            xc3 = xc3 + cw[3 - k:4 - k] * prev
        last = jnp.zeros((SUBLANES, D_RNN), F32)
        for r in range(SUBLANES):
            last = jnp.where(sub == r, pltpu.roll(xr3[SEG_LEN - SUBLANES + r], r + 1, 0), last)
        xtail_ref[u] = last
        cst_ref[u] = last
        a, b = _rg_gates(xc3.reshape(TM, D_RNN), wa_ref, ba_ref[...], wx_ref, bx_ref[...],
                         lam_ref[...])
        a_ref[u] = a.reshape(SEG_LEN, SUBLANES, D_RNN)
        b_ref[u] = b.reshape(SEG_LEN, SUBLANES, D_RNN)
        return gate

    def scan(u):
        def step(v, carry):
            hl, ac = carry
            av = a_ref[u, v]
            hl = av * hl + b_ref[u, v]
            ac = av * ac
            hs_ref[u, v] = hl
            ac_ref[u, v] = ac
            return hl, ac

        zero = jnp.zeros((SUBLANES, D_RNN), F32)
        hl, ac = lax.fori_loop(0, SEG_LEN, step, (zero, zero + 1.0), unroll=True)
        h_in = jnp.where(sub == 0, hcar_ref[u], 0.0)
        for s in range(1, SUBLANES):
            h_in = jnp.where(sub == s, pltpu.roll(hl + ac * h_in, 1, 0), h_in)
        h_end = (hl + ac * h_in)[SUBLANES - 1:SUBLANES, :]
        hcar_ref[u] = h_end
        hst_ref[u] = h_end
        return hs_ref[u] + ac_ref[u] * h_in

    def back(u, gate, hs):
        y = (jax.nn.gelu(gate) * hs.reshape(TM, D_RNN)).astype(BF16)
        y = _dot(unperm_ref[...], y).astype(BF16)
        o_ref[u] = x_ref[u] + mod_ref[u][2:3] * _dot(y, wout_ref[...])

    gates = [front(u) for u in range(RG_PAIR)]
    for u in range(RG_PAIR):
        back(u, gates[u], scan(u))


def _rglru(x2d, mod, w_in, conv_w, conv_b, w_a, b_a, w_x, b_x, lam, w_out):
    n_tok = x2d.shape[0]
    x3 = x2d.reshape(BATCH, SEQ, D_MODEL)
    rho = jnp.arange(TM)
    perm = (jnp.arange(TM)[None, :] == ((rho % SUBLANES) * SEG_LEN + rho // SUBLANES)[:, None])
    perm = perm.astype(BF16)
    row_spec = pl.BlockSpec((RG_PAIR, TM, D_MODEL),
                            lambda i: (i // TILES_PER_SEQ, i % TILES_PER_SEQ, 0))

    def seq_spec(rows):
        return pl.BlockSpec((RG_PAIR, rows, D_RNN), lambda i: (i // TILES_PER_SEQ, 0, 0))

    out, cst, hst = pl.pallas_call(
        _rglru_kernel,
        grid=(n_tok // (RG_PAIR * TM),),
        in_specs=[
            row_spec,
            seq_spec(6),
            _const_spec((TM, TM)),
            _const_spec((TM, TM)),
            _const_spec((D_MODEL, 2 * D_RNN)),
            _const_spec((4, D_RNN)),
            _const_spec((1, D_RNN)),
            _const_spec((RG_HEADS, RG_BLOCK, RG_BLOCK)),
            _const_spec((1, D_RNN)),
            _const_spec((RG_HEADS, RG_BLOCK, RG_BLOCK)),
            _const_spec((1, D_RNN)),
            _const_spec((1, D_RNN)),
            _const_spec((D_RNN, D_MODEL)),
        ],
        out_specs=[row_spec, seq_spec(SUBLANES), seq_spec(1)],
        out_shape=[
            jax.ShapeDtypeStruct((BATCH, SEQ, D_MODEL), F32),
            jax.ShapeDtypeStruct((BATCH, SUBLANES, D_RNN), F32),
            jax.ShapeDtypeStruct((BATCH, 1, D_RNN), F32),
        ],
        scratch_shapes=[
            pltpu.VMEM((RG_PAIR, SUBLANES, D_RNN), F32),
            pltpu.VMEM((RG_PAIR, 1, D_RNN), F32),
        ] + [pltpu.VMEM((RG_PAIR, SEG_LEN, SUBLANES, D_RNN), F32)] * 4,
        compiler_params=_cparams("arbitrary"),
        name="rglru",
    )(x3, mod, perm, perm.T, w_in, conv_w, conv_b, w_a, b_a, w_x, b_x, lam, w_out)
    return out.reshape(n_tok, D_MODEL), cst, hst


def _s_even_in_kernel(x_ref, mod_ref, w_ref, lng_ref, lnb_ref, ws0_ref, bs0_ref,
                      va_ref, a_ref, q_ref, k_ref, v_ref):
    mod = mod_ref[...]
    h = _rms_mod(x_ref[...], mod[:, D_MODEL:2 * D_MODEL], mod[:, 0:D_MODEL]).astype(BF16)
    u = jax.nn.gelu(_dot(h, w_ref[:, 0:W_A]))
    va = _layernorm(jax.nn.gelu(_dot(h, w_ref[:, W_A:2 * W_A])), lng_ref[...], lnb_ref[...])
    va_ref[...] = va
    a_ref[...] = (u * (ws0_ref[...] * va + bs0_ref[...])).astype(BF16)
    base = 2 * W_A
    q_ref[...] = _dot(h, w_ref[:, base:base + W_B])
    k_ref[...] = _dot(h, w_ref[:, base + W_B:base + 2 * W_B])
    v_ref[...] = _dot(h, w_ref[:, base + 2 * W_B:base + 3 * W_B])


def _s_even_in(x, mod, w_in, ln_g, ln_b, ws0, bs0):
    n = x.shape[0]
    f = jax.ShapeDtypeStruct((n, W_B), F32)
    return pl.pallas_call(
        _s_even_in_kernel,
        out_shape=[f, jax.ShapeDtypeStruct((n, W_A), BF16), f, f, f],
        compiler_params=pltpu.CompilerParams(vmem_limit_bytes=VMEM_LIMIT),
        name="s_even_in",
    )(x, mod, w_in, ln_g, ln_b, ws0, bs0)


def _s_attn_heads(qrow, knrow, vnrow, kt3, vt3):
    nh = kt3.shape[0]
    n_hd = nh * HD
    qrow = qrow * (HD ** -0.5)
    hrow = lax.broadcasted_iota(jnp.int32, (SUBLANES, n_hd), 0)
    hcol = lax.broadcasted_iota(jnp.int32, (SUBLANES, n_hd), 1) // HD
    own = hrow == hcol
    qbd = jnp.where(own, jnp.broadcast_to(qrow, (SUBLANES, n_hd)), 0.0)
    kt = kt3.reshape(n_hd, WIN)
    vt = vt3.reshape(n_hd, WIN)
    s = _dot(qbd.astype(BF16), kt.astype(BF16))
    s_new = jnp.sum(qbd * knrow, axis=-1, keepdims=True)
    t = lax.broadcasted_iota(jnp.int32, (SUBLANES, WIN), 1)
    dist = WIN - t
    vtb = vt.astype(BF16)
    outs, lses = [], []
    for d in DILATIONS:
        valid = ((dist & (d - 1)) == 0) & (dist <= N_BACK * d)
        sm = jnp.where(valid, s, NEG_INF)
        mx = jnp.maximum(jnp.max(sm, axis=-1, keepdims=True), s_new)
        e = jnp.where(valid, jnp.exp(sm - mx), 0.0)
        e_new = jnp.exp(s_new - mx)
        den = jnp.sum(e, axis=-1, keepdims=True) + e_new
        o = lax.dot_general(e.astype(BF16), vtb, (((1,), (1,)), ((), ())),
                            preferred_element_type=F32)
        outs.append((o + e_new * vnrow) / den)
        lses.append(mx + jnp.log(den))
    mx = jnp.maximum(jnp.maximum(lses[0], lses[1]), lses[2])
    ws = [jnp.exp(l - mx) for l in lses]
    num = ws[0] * outs[0] + ws[1] * outs[1] + ws[2] * outs[2]
    mixed = num / (ws[0] + ws[1] + ws[2])
    o_row = jnp.sum(jnp.where(own, mixed, 0.0), axis=0, keepdims=True)

    r2 = lax.broadcasted_iota(jnp.int32, (n_hd, n_hd), 0)
    c2 = lax.broadcasted_iota(jnp.int32, (n_hd, n_hd), 1)
    diag = r2 == c2
    kcol = jnp.sum(jnp.where(diag, jnp.broadcast_to(knrow, (n_hd, n_hd)), 0.0), axis=-1, keepdims=True)
    vcol = jnp.sum(jnp.where(diag, jnp.broadcast_to(vnrow, (n_hd, n_hd)), 0.0), axis=-1, keepdims=True)
    tt = lax.broadcasted_iota(jnp.int32, (n_hd, WIN), 1)
    is_last = tt == WIN - 1
    kt_new = jnp.where(is_last, kcol, pltpu.roll(kt, WIN - 1, 1)).reshape(nh, HD, WIN)
    vt_new = jnp.where(is_last, vcol, pltpu.roll(vt, WIN - 1, 1)).reshape(nh, HD, WIN)
    return o_row, kt_new, vt_new


def _s_even_out_kernel(x_ref, mod_ref, a_ref, b_ref, w_ref, o_ref):
    g1 = mod_ref[:, 2 * D_MODEL:3 * D_MODEL]
    mo = _dot(a_ref[...], w_ref[0:W_A, :]) + _dot(b_ref[...].astype(BF16), w_ref[W_A:W_A + W_B, :])
    o_ref[...] = x_ref[...] + g1 * mo


def _s_even_out(x, mod, a_out, b_out, w_out):
    return pl.pallas_call(
        _s_even_out_kernel,
        out_shape=jax.ShapeDtypeStruct(x.shape, F32),
        compiler_params=pltpu.CompilerParams(vmem_limit_bytes=VMEM_LIMIT),
        name="s_even_out",
    )(x, mod, a_out, b_out, w_out)


def _s_ffn_kernel(x_ref, mod_ref, p2_ref, p1_ref, wup_ref, cw_ref, cb_ref, wdn_ref, fg_ref,
                  o_ref, up_ref, *, final):
    x = x_ref[...]
    mod = mod_ref[...]
    h = _rms_mod(x, mod[:, 4 * D_MODEL:5 * D_MODEL], mod[:, 3 * D_MODEL:4 * D_MODEL]).astype(BF16)
    up = _dot(h, wup_ref[...])
    up_ref[...] = up
    cw = cw_ref[...]
    conv = cb_ref[...] + cw[0:1] * p2_ref[...] + cw[1:2] * p1_ref[...] + cw[2:3] * up
    ca, cg = conv[:, 0:D_FF], conv[:, D_FF:2 * D_FF]
    act = (ca * jax.nn.sigmoid(ca) * cg).astype(BF16)
    y = x + mod[:, 5 * D_MODEL:6 * D_MODEL] * _dot(act, wdn_ref[...])
    if final:
        y = _rmsnorm(y) * fg_ref[...]
    o_ref[...] = y


def _s_ffn(x, mod, p2, p1, layer, w_up, conv_w, conv_b, w_down, final_g, final):
    n = x.shape[0]
    f2 = 2 * D_FF
    return pl.pallas_call(
        functools.partial(_s_ffn_kernel, final=final),
        grid=(1,),
        in_specs=[_const_spec(x.shape), _const_spec(mod.shape), _const_spec(p2.shape),
                  _const_spec(p1.shape), _layer_spec((D_MODEL, f2), layer),
                  _layer_spec((3, f2), layer), _layer_spec((1, f2), layer),
                  _layer_spec((D_FF, D_MODEL), layer), _const_spec((1, D_MODEL))],
        out_specs=[pl.BlockSpec(x.shape, lambda i: (0, 0)), pl.BlockSpec((n, f2), lambda i: (0, 0))],
        out_shape=[jax.ShapeDtypeStruct(x.shape, F32), jax.ShapeDtypeStruct((n, f2), F32)],
        compiler_params=_cparams("arbitrary"),
        name="s_conv_ffn_final" if final else "s_conv_ffn",
    )(x, mod, p2, p1, w_up, conv_w, conv_b, w_down, final_g)


def _s_rglru_kernel(x_ref, mod_ref, c0_ref, c1_ref, c2_ref, h0_ref, win_ref, cw_ref, cb_ref,
                    wa_ref, ba_ref, wx_ref, bx_ref, lam_ref, wout_ref, o_ref, xr_ref, hn_ref):
    x = x_ref[...]
    mod = mod_ref[...]
    h = _rms_mod(x, mod[:, D_MODEL:2 * D_MODEL], mod[:, 0:D_MODEL]).astype(BF16)
    gate = _dot(h, win_ref[:, 0:D_RNN])
    xr = _dot(h, win_ref[:, D_RNN:2 * D_RNN])
    xr_ref[...] = xr
    cw = cw_ref[...]
    xc = (cb_ref[...] + cw[0:1] * c0_ref[...] + cw[1:2] * c1_ref[...] + cw[2:3] * c2_ref[...]
          + cw[3:4] * xr)
    a, b = _rg_gates(xc, wa_ref, ba_ref[...], wx_ref, bx_ref[...], lam_ref[...])
    hn = a * h0_ref[...] + b
    hn_ref[...] = hn
    y = (jax.nn.gelu(gate) * hn).astype(BF16)
    o_ref[...] = x + mod[:, 2 * D_MODEL:3 * D_MODEL] * _dot(y, wout_ref[...])


def _s_rglru(x, mod, c0, c1, c2, h0, w_in, conv_w, conv_b, w_a, b_a, w_x, b_x, lam, w_out):
    f = jax.ShapeDtypeStruct(x.shape, F32)
    return pl.pallas_call(
        _s_rglru_kernel,
        out_shape=[f, f, f],
        compiler_params=pltpu.CompilerParams(vmem_limit_bytes=VMEM_LIMIT),
        name="s_rglru",
    )(x, mod, c0, c1, c2, h0, w_in, conv_w, conv_b, w_a, b_a, w_x, b_x, lam, w_out)


def kernel(x_prompt, x_sample, cache_win_k, cache_win_v, state_rglru_conv, state_rglru_h, state_ffn_conv, c_prompt, c_sample, w_ada, b_ada, w_in_even, ln_v_g, ln_v_b, w_sgu, b_sgu, w_out_even, w_in_odd, rg_conv_w, rg_conv_b, rg_w_a, rg_b_a, rg_w_x, rg_b_x, rg_lambda, w_out_odd, ffn_w_up, ffn_conv_w, ffn_conv_b, ffn_w_down, final_g):
    w_in_even_b = w_in_even[0].astype(BF16)
    w_out_even_b = w_out_even[0].astype(BF16)
    w_in_odd_b = w_in_odd[0].astype(BF16)
    w_out_odd_b = w_out_odd[0].astype(BF16)
    rg_w_a_b = rg_w_a[0].astype(BF16)
    rg_w_x_b = rg_w_x[0].astype(BF16)
    conv_b3 = ffn_conv_b.reshape(2, 1, 2 * D_FF)
    final_g2 = final_g.reshape(1, D_MODEL)

    pad = jnp.zeros((ADA_ROWS - BATCH - DEC_BATCH, D_MODEL), F32)
    mod = _ada(jnp.concatenate([c_prompt, c_sample, pad], axis=0), w_ada, b_ada)
    mod_p = mod[:, :BATCH].reshape(2, BATCH, 6, D_MODEL)
    mod_s = mod[:, BATCH:BATCH + DEC_BATCH]

    xs = x_sample.reshape(DEC_BATCH, D_MODEL)
    ws0 = jnp.repeat(w_sgu[0, :, 0, 0], A_GROUP).reshape(1, W_A)
    bs0 = jnp.repeat(b_sgu[0, :, 0], A_GROUP).reshape(1, W_A)
    va_s, a_s, q_s, k_s, v_s = _s_even_in(xs, mod_s[0], w_in_even_b, ln_v_g, ln_v_b, ws0, bs0)
    kt_c = cache_win_k[0].transpose(0, 2, 3, 1)
    vt_c = cache_win_v[0].transpose(0, 2, 3, 1)

    x = x_prompt.reshape(BATCH * SEQ, D_MODEL)
    a_out, q, k, v, kt_p, vt_p = _even_in(x, mod_p[0], w_in_even_b, ln_v_g, ln_v_b, w_sgu[0],
                                          b_sgu[0].T)
    b_out, w_up_b, w_down_b = _attn(q, k, v, ffn_w_up, ffn_w_down)
    x, ffn_st0 = _ffn(x, mod_p[0], 0, w_up_b, ffn_conv_w, conv_b3, w_down_b, final_g2, False,
                      pre=(a_out, b_out, w_out_even_b))
    x, rg_cst, rg_hst = _rglru(x, mod_p[1], w_in_odd_b, rg_conv_w[0], rg_conv_b, rg_w_a_b, rg_b_a,
                               rg_w_x_b, rg_b_x, rg_lambda, w_out_odd_b)
    y_p, ffn_st1, b_s, kt_n, vt_n = _ffn(x, mod_p[1], 1, w_up_b, ffn_conv_w, conv_b3, w_down_b,
                                         final_g2, True, att=(q_s, k_s, v_s, kt_c, vt_c))

    y_prompt = y_p.reshape(BATCH, SEQ, D_MODEL)
    to_win = lambda t: t.reshape(1, BATCH, N_HEADS, HD, WIN).transpose(0, 1, 4, 2, 3)
    win_k_prompt = to_win(kt_p)
    win_v_prompt = to_win(vt_p)
    rglru_conv_prompt = rg_cst[None, :, SUBLANES - 3:, :]
    rglru_h_prompt = rg_hst.reshape(1, BATCH, D_RNN)
    ffn_conv_prompt = jnp.stack([ffn_st0[:, SUBLANES - 2:, :], ffn_st1[:, SUBLANES - 2:, :]])

    xs = _s_even_out(xs, mod_s[0], a_s, b_s, w_out_even_b)
    st0 = state_ffn_conv[0]
    xs, up0 = _s_ffn(xs, mod_s[0], st0[:, 0], st0[:, 1], 0, w_up_b, ffn_conv_w, conv_b3, w_down_b,
                     final_g2, False)
    cst = state_rglru_conv[0]
    xs, xr_s, hn_s = _s_rglru(xs, mod_s[1], cst[:, 0], cst[:, 1], cst[:, 2], state_rglru_h[0],
                              w_in_odd_b, rg_conv_w[0], rg_conv_b, rg_w_a_b, rg_b_a, rg_w_x_b,
                              rg_b_x, rg_lambda, w_out_odd_b)
    st1 = state_ffn_conv[1]
    ys, up1 = _s_ffn(xs, mod_s[1], st1[:, 0], st1[:, 1], 1, w_up_b, ffn_conv_w, conv_b3, w_down_b,
                     final_g2, True)

    y_sample = ys.reshape(DEC_BATCH, 1, D_MODEL)
    chunk_v_sample = va_s.reshape(1, DEC_BATCH, 1, W_A)
    win_k_sample = kt_n.transpose(0, 3, 1, 2)[None]
    win_v_sample = vt_n.transpose(0, 3, 1, 2)[None]
    rglru_conv_sample = jnp.stack([cst[:, 1], cst[:, 2], xr_s], axis=1)[None]
    rglru_h_sample = hn_s[None]
    ffn_conv_sample = jnp.stack([jnp.stack([st0[:, 1], up0], axis=1),
                                 jnp.stack([st1[:, 1], up1], axis=1)])

    return (y_prompt, y_sample, win_k_prompt, win_v_prompt, rglru_conv_prompt, rglru_h_prompt,
            ffn_conv_prompt, chunk_v_sample, win_k_sample, win_v_sample, rglru_conv_sample,
            rglru_h_sample, ffn_conv_sample)
```
